```python
import math
import jax, jax.numpy as jnp
from jax import lax
import numpy as np

D_MODEL = 1024
BATCH = 2
SEQ = 8192
DEPTH = 4

GRID_W = 64
CTX_LEN = 256
D_MIX = D_MODEL
N_GROUPS = 4
GROUP_W = D_MIX // N_GROUPS
HEAD_DIM = 64
N_HEADS_G = GROUP_W // HEAD_DIM
CHUNK = 64
Q_BLOCK = 128
EPS = 1e-6
ROPE_BASE = 10000.0
W_MOD_SCALE = 0.5
DA_QK = HEAD_DIM // 2
GLA_DK = HEAD_DIM // 2
GLA_RANK = 16
GLA_TAU = 16.0
GDN_CONV = 5
M_COLS = 5 * GROUP_W + 4 * N_HEADS_G
A_COLS = 4 * GROUP_W
G_COLS = 2 * N_HEADS_G * GLA_DK + 2 * GROUP_W + 2 * GLA_RANK
D_COLS = 4 * GROUP_W + 4 * N_HEADS_G
P_IN = M_COLS + A_COLS + G_COLS + D_COLS
F32 = jnp.float32

kernel_name = 'hybrid_quad_mixer_diffusion_block'


def _split(u, sizes):
    idx = np.cumsum(sizes)[:-1].tolist()
    return jnp.split(u, idx, axis=-1)


def _rms(x, w):
    xf = x.astype(F32)
    y = xf * lax.rsqrt(jnp.mean(xf * xf, axis=-1, keepdims=True) + EPS) * w.astype(F32)
    return y.astype(x.dtype)


def _l2n(x):
    xf = x.astype(F32)
    return xf * lax.rsqrt(jnp.sum(xf * xf, axis=-1, keepdims=True) + EPS)


def _heads(a):
    B, T, _ = a.shape
    return a.reshape(B, T, N_HEADS_G, -1).transpose(0, 2, 1, 3)


def _merge(a):
    B, H, T, d = a.shape
    return a.transpose(0, 2, 1, 3).reshape(B, T, H * d)


def _head_norm(h, w):
    H, d = h.shape[1], h.shape[-1]
    return _rms(h, w.reshape(H, 1, d))


def _to_chunks(a):
    B, H, T = a.shape[:3]
    a = a.reshape((B, H, T // CHUNK, CHUNK) + a.shape[3:])
    return jnp.moveaxis(a, 2, 0)


def _from_chunks(a):
    a = jnp.moveaxis(a, 0, 2)
    return a.reshape(a.shape[:2] + (a.shape[2] * a.shape[3],) + a.shape[4:])


def _flip_t(a):
    return jnp.flip(a, axis=2)


def _identity(a):
    return a


def _bidir(run, ctx_dirs, lat_dirs, state0):
    y_ctx, y_lat = [], []
    for d in range(2):
        fl = _flip_t if d == 1 else _identity
        yc, st = run(*[fl(a) for a in ctx_dirs[d]], state0)
        yl, _ = run(*[fl(a) for a in lat_dirs[d]], st)
        y_ctx.append(fl(yc))
        y_lat.append(fl(yl))
    return y_ctx[0] + y_ctx[1], y_lat[0] + y_lat[1]


def _axial_rope_tables(rows):
    row = jnp.repeat(jnp.arange(rows), GRID_W).astype(F32)
    col = jnp.tile(jnp.arange(GRID_W), rows).astype(F32)
    half = DA_QK // 2
    inv = jnp.power(ROPE_BASE, -jnp.arange(0, half, 2, dtype=F32) / half)
    def tab(p):
        ang = p[:, None] * inv
        ang = jnp.concatenate([ang, ang], axis=-1)
        return jnp.cos(ang), jnp.sin(ang)
    cr, sr = tab(row)
    cc, sc = tab(col)
    return jnp.concatenate([cr, cc], -1), jnp.concatenate([sr, sc], -1)


def _rot_half(x):
    x1, x2 = jnp.split(x, 2, axis=-1)
    return jnp.concatenate([-x2, x1], axis=-1)


def _rope2d(x, cos, sin):
    half = x.shape[-1] // 2
    rot = jnp.concatenate([_rot_half(x[..., :half]), _rot_half(x[..., half:])], axis=-1)
    return (x * cos + rot * sin).astype(x.dtype)


def _dwconv(x, w):
    K, C = w.shape
    return lax.conv_general_dilated(x, w[:, None, :].astype(x.dtype), window_strides=(1,),
                                    padding=[(K // 2, K // 2)],
                                    dimension_numbers=('NWC', 'WIO', 'NWC'),
                                    feature_group_count=C)


def _mlstm_run(q, k, v, log_i, log_f, state):
    tri = jnp.tril(jnp.ones((CHUNK, CHUNK), bool))
    def step(carry, inp):
        C, n, m = carry
        qc, kc, vc, ic, fc = inp
        b = jnp.cumsum(fc, axis=-1)
        d_ts = jnp.where(tri, b[..., :, None] - b[..., None, :] + ic[..., None, :], -jnp.inf)
        inter = b + m[..., None]
        m_t = jnp.maximum(inter, jnp.max(d_ts, axis=-1))
        s = jnp.einsum('bhtd,bhsd->bhts', qc, kc) * jnp.exp(d_ts - m_t[..., None])
        w_inter = jnp.exp(inter - m_t)
        num = (w_inter[..., None] * jnp.einsum('bhtd,bhde->bhte', qc, C)
               + jnp.einsum('bhts,bhse->bhte', s, vc))
        nq = w_inter * jnp.einsum('bhtd,bhd->bht', qc, n) + jnp.sum(s, axis=-1)
        h = num / jnp.maximum(jnp.abs(nq), jnp.exp(-m_t))[..., None]
        b_last = b[..., -1]
        d_s = b_last[..., None] - b + ic
        m_new = jnp.maximum(b_last + m, jnp.max(d_s, axis=-1))
        w_s = jnp.exp(d_s - m_new[..., None])
        w_c = jnp.exp(b_last + m - m_new)
        C = w_c[..., None, None] * C + jnp.einsum('bhs,bhsd,bhse->bhde', w_s, kc, vc)
        n = w_c[..., None] * n + jnp.einsum('bhs,bhsd->bhd', w_s, kc)
        return (C, n, m_new), h
    state, h = lax.scan(step, state, tuple(_to_chunks(a) for a in (q, k, v, log_i, log_f)))
    return _from_chunks(h), state


def _gla_run(q, k, v, log_a, S):
    tri = jnp.tril(jnp.ones((CHUNK, CHUNK), bool))
    def step(S, inp):
        qc, kc, vc, ac = inp
        b = jnp.cumsum(ac, axis=2)
        rel = jnp.exp(jnp.where(tri[:, :, None], b[:, :, :, None, :] - b[:, :, None, :, :], -jnp.inf))
        A = jnp.einsum('bhtc,bhsc,bhtsc->bhts', qc, kc, rel)
        o = (jnp.einsum('bhtc,bhce->bhte', qc * jnp.exp(b), S)
             + jnp.einsum('bhts,bhse->bhte', A, vc))
        b_last = b[:, :, -1:, :]
        S = (jnp.exp(b_last[:, :, 0])[..., None] * S
             + jnp.einsum('bhsc,bhse->bhce', kc * jnp.exp(b_last - b), vc))
        return S, o
    S, o = lax.scan(step, S, tuple(_to_chunks(a) for a in (q, k, v, log_a)))
    return _from_chunks(o), S


def _gdn_run(q, k, v, g, beta, S):
    qc, kc, vc, gc, bc = (_to_chunks(a) for a in (q, k, v, g, beta))
    gc = jnp.cumsum(gc, axis=-1)
    tri = jnp.tril(jnp.ones((CHUNK, CHUNK), bool))
    strict = jnp.tril(jnp.ones((CHUNK, CHUNK), bool), k=-1)
    decay = jnp.exp(jnp.where(tri, gc[..., :, None] - gc[..., None, :], -jnp.inf))
    kb = kc * bc[..., None]
    lower = jnp.where(strict, jnp.einsum('nbhtd,nbhsd->nbhts', kb, kc) * decay, 0.0)
    eye = jnp.eye(CHUNK, dtype=lower.dtype)
    tmat = lax.linalg.triangular_solve(eye + lower, jnp.broadcast_to(eye, lower.shape),
                                       left_side=True, lower=True, unit_diagonal=True)
    u = tmat @ (vc * bc[..., None])
    w = tmat @ (kb * jnp.exp(gc)[..., None])
    attn = jnp.einsum('nbhtd,nbhsd->nbhts', qc, kc) * decay
    def step(S, inp):
        qi, ki, ui, wi, ai, gi = inp
        v_new = ui - jnp.einsum('bhtc,bhce->bhte', wi, S)
        o = (jnp.einsum('bhtc,bhce->bhte', qi * jnp.exp(gi)[..., None], S)
             + jnp.einsum('bhts,bhse->bhte', ai, v_new))
        g_last = gi[..., -1:]
        S = (jnp.exp(g_last)[..., None] * S
             + jnp.einsum('bhsc,bhse->bhce', ki * jnp.exp(g_last - gi)[..., None], v_new))
        return S, o
    S, o = lax.scan(step, S, (qc, kc, u, w, attn, gc))
    return _from_chunks(o), S


def _diff_attend(q, k, v, lam):
    B, H, _, T, dq = q.shape
    nb = T // Q_BLOCK
    qb = q.reshape(B, H, 2, nb, Q_BLOCK, dq).transpose(3, 0, 1, 2, 4, 5)
    scale = dq ** -0.5
    def one(qblk):
        s = jnp.einsum('bhjqd,bhjkd->bhjqk', qblk, k).astype(F32) * scale
        p = jax.nn.softmax(s, axis=-1)
        wts = p[:, :, 0] - lam * p[:, :, 1]
        return jnp.einsum('bhqk,bhkd->bhqd', wts.astype(v.dtype), v)
    o = lax.map(one, qb)
    return o.transpose(1, 2, 0, 3, 4).reshape(B, H, T, -1)


def _mlstm_mixer(uc, ux, b_i, b_f, norm_w, need_ctx):
    def prep(u):
        B, T, _ = u.shape
        q, k, v, o, z, ig, fg = _split(u, [GROUP_W] * 5 + [2 * N_HEADS_G] * 2)
        q = _heads(q.astype(F32))
        k = _heads(k.astype(F32)) * HEAD_DIM ** -0.5
        v = _heads(v.astype(F32))
        gshape = (B, T, 2, N_HEADS_G)
        log_i = (ig.astype(F32).reshape(gshape) + b_i).transpose(2, 0, 3, 1)
        log_f = jax.nn.log_sigmoid(fg.astype(F32).reshape(gshape) + b_f).transpose(2, 0, 3, 1)
        return [(q, k, v, log_i[d], log_f[d]) for d in range(2)], (o, z)
    ctx_dirs, gates_c = prep(uc)
    lat_dirs, gates_x = prep(ux)
    B = ux.shape[0]
    state0 = (jnp.zeros((B, N_HEADS_G, HEAD_DIM, HEAD_DIM), F32),
              jnp.zeros((B, N_HEADS_G, HEAD_DIM), F32),
              jnp.zeros((B, N_HEADS_G), F32))
    hc, hx = _bidir(_mlstm_run, ctx_dirs, lat_dirs, state0)
    def finish(h, gates, dtype):
        o, z = gates
        y = (_merge(_head_norm(h, norm_w)) * jax.nn.sigmoid(o.astype(F32))
             * jax.nn.silu(z.astype(F32)))
        return y.astype(dtype)
    yc = finish(hc, gates_c, uc.dtype) if need_ctx else None
    return yc, finish(hx, gates_x, ux.dtype)


def _diff_mixer(uc, ux, cos, sin, q_norm, k_norm, lam, norm_w, lam_init, need_ctx):
    def prep(u):
        B, T, _ = u.shape
        q, k, v, z = _split(u, [GROUP_W] * 4)
        def parts(a):
            return a.reshape(B, T, N_HEADS_G, 2, DA_QK).transpose(0, 2, 3, 1, 4)
        return _rms(parts(q), q_norm), _rms(parts(k), k_norm), _heads(v), z
    qc, kc, vc, zc = prep(uc)
    qx, kx, vx, zx = prep(ux)
    qx = _rope2d(qx, cos, sin)
    kx = _rope2d(kx, cos, sin)
    lf = lam.astype(F32)
    lam_full = jnp.exp(jnp.sum(lf[0] * lf[1])) - jnp.exp(jnp.sum(lf[2] * lf[3])) + lam_init
    def finish(o, z, dtype):
        y = _merge(_head_norm(o, norm_w)).astype(F32) * (1.0 - lam_init) * jax.nn.silu(z.astype(F32))
        return y.astype(dtype)
    ox = _diff_attend(qx, jnp.concatenate([kx, kc], axis=3), jnp.concatenate([vx, vc], axis=2), lam_full)
    yx = finish(ox, zx, ux.dtype)
    yc = finish(_diff_attend(qc, kc, vc, lam_full), zc, uc.dtype) if need_ctx else None
    return yc, yx


def _gla_mixer(uc, ux, w_up, b_gk, norm_w, need_ctx):
    def prep(u):
        B, T, _ = u.shape
        q, k, v, z, r = _split(u, [N_HEADS_G * GLA_DK] * 2 + [GROUP_W] * 2 + [2 * GLA_RANK])
        q = _heads(q.astype(F32)) * GLA_DK ** -0.5
        k = _heads(k.astype(F32))
        v = _heads(v.astype(F32))
        r = r.astype(F32).reshape(B, T, 2, GLA_RANK)
        gk = jnp.einsum('btdr,drk->dbtk', r, w_up.astype(F32)) + b_gk.astype(F32)[:, None, None, :]
        log_a = jax.nn.log_sigmoid(gk) / GLA_TAU
        log_a = log_a.reshape(2, B, T, N_HEADS_G, GLA_DK).transpose(0, 1, 3, 2, 4)
        return [(q, k, v, log_a[d]) for d in range(2)], z
    ctx_dirs, zc = prep(uc)
    lat_dirs, zx = prep(ux)
    B = ux.shape[0]
    state0 = jnp.zeros((B, N_HEADS_G, GLA_DK, HEAD_DIM), F32)
    oc, ox = _bidir(_gla_run, ctx_dirs, lat_dirs, state0)
    def finish(o, z, dtype):
        return (_merge(_head_norm(o, norm_w)) * jax.nn.silu(z.astype(F32))).astype(dtype)
    yc = finish(oc, zc, uc.dtype) if need_ctx else None
    return yc, finish(ox, zx, ux.dtype)


def _gdn_mixer(uc, ux, conv_w, a_log, dt_bias, norm_w, need_ctx):
    def prep(u):
        B, T, _ = u.shape
        qkv, z, bt, a = _split(u, [3 * GROUP_W, GROUP_W, 2 * N_HEADS_G, 2 * N_HEADS_G])
        qkv = jax.nn.silu(_dwconv(qkv, conv_w))
        q, k, v = jnp.split(qkv, 3, axis=-1)
        q = _l2n(_heads(q)) * HEAD_DIM ** -0.5
        k = _l2n(_heads(k))
        v = _heads(v).astype(F32)
        gshape = (B, T, 2, N_HEADS_G)
        beta = jax.nn.sigmoid(bt.astype(F32).reshape(gshape)).transpose(2, 0, 3, 1)
        dt = jax.nn.softplus(a.astype(F32).reshape(gshape) + dt_bias).transpose(2, 0, 3, 1)
        g = -jnp.exp(a_log.astype(F32))[:, None, :, None] * dt
        return [(q, k, v, g[d], beta[d]) for d in range(2)], z
    ctx_dirs, zc = prep(uc)
    lat_dirs, zx = prep(ux)
    B = ux.shape[0]
    state0 = jnp.zeros((B, N_HEADS_G, HEAD_DIM, HEAD_DIM), F32)
    oc, ox = _bidir(_gdn_run, ctx_dirs, lat_dirs, state0)
    def finish(o, z, dtype):
        return (_merge(_head_norm(o, norm_w)) * jax.nn.silu(z.astype(F32))).astype(dtype)
    yc = finish(oc, zc, uc.dtype) if need_ctx else None
    return yc, finish(ox, zx, ux.dtype)


def setup_inputs(seed: int = 0) -> dict:
    key = jax.random.key(seed)
    ks = jax.random.split(key, 24)
    H = N_HEADS_G
    def nrm(k, shape, s=1.0):
        return jax.random.normal(k, shape, F32) * s
    dt = jnp.exp(jax.random.uniform(ks[20], (DEPTH, 2, H), F32)
                 * (math.log(0.1) - math.log(0.001)) + math.log(0.001))
    return {
        'x': nrm(ks[0], (BATCH, SEQ, D_MODEL)),
        'c': nrm(ks[1], (BATCH, D_MODEL)),
        'ctx': nrm(ks[2], (BATCH, CTX_LEN, D_MODEL)),
        'c_ctx': nrm(ks[3], (D_MODEL,)),
        'norm_w': 1.0 + nrm(ks[4], (DEPTH, D_MODEL), 0.02),
        'w_mod': nrm(ks[5], (DEPTH, D_MODEL, 3 * D_MODEL), W_MOD_SCALE * D_MODEL ** -0.5),
        'b_mod': nrm(ks[6], (DEPTH, 3 * D_MODEL), 0.02),
        'w_in': nrm(ks[7], (DEPTH, D_MODEL, P_IN), D_MODEL ** -0.5),
        'w_out': nrm(ks[8], (DEPTH, D_MIX, D_MODEL), D_MIX ** -0.5),
        'mlstm_b_i': nrm(ks[9], (DEPTH, 2, H), 0.1),
        'mlstm_b_f': jnp.linspace(3.0, 6.0, H, dtype=F32) + nrm(ks[10], (DEPTH, 2, H), 0.1),
        'mlstm_norm': 1.0 + nrm(ks[11], (DEPTH, GROUP_W), 0.02),
        'diff_q_norm': 1.0 + nrm(ks[12], (DEPTH, DA_QK), 0.02),
        'diff_k_norm': 1.0 + nrm(ks[13], (DEPTH, DA_QK), 0.02),
        'diff_lambda': nrm(ks[14], (DEPTH, 4, DA_QK), 0.1),
        'diff_norm': 1.0 + nrm(ks[15], (DEPTH, GROUP_W), 0.02),
        'gla_w_up': nrm(ks[16], (DEPTH, 2, GLA_RANK, H * GLA_DK), GLA_RANK ** -0.5),
        'gla_b': nrm(ks[17], (DEPTH, 2, H * GLA_DK), 0.1),
        'gla_norm': 1.0 + nrm(ks[18], (DEPTH, GROUP_W), 0.02),
        'gdn_conv': nrm(ks[19], (DEPTH, GDN_CONV, 3 * GROUP_W), GDN_CONV ** -0.5),
        'gdn_a_log': jnp.log(jax.random.uniform(ks[21], (DEPTH, 2, H), F32, 1.0, 16.0)),
        'gdn_dt_bias': dt + jnp.log(-jnp.expm1(-dt)),
        'gdn_norm': 1.0 + nrm(ks[22], (DEPTH, GROUP_W), 0.02),
    }


def reference(x, c, ctx, c_ctx, norm_w, w_mod, b_mod, w_in, w_out, mlstm_b_i, mlstm_b_f, mlstm_norm,
              diff_q_norm, diff_k_norm, diff_lambda, diff_norm, gla_w_up, gla_b, gla_norm,
              gdn_conv, gdn_a_log, gdn_dt_bias, gdn_norm):
    n_lat = x.shape[1]
    rows = n_lat // GRID_W
    cos, sin = _axial_rope_tables(rows)
    for l in range(DEPTH):
        need_ctx = l < DEPTH - 1
        mod = jax.nn.silu(c) @ w_mod[l] + b_mod[l]
        mod_c = jax.nn.silu(c_ctx) @ w_mod[l] + b_mod[l]
        sh, sc, gt = jnp.split(mod, 3, axis=-1)
        sh_c, sc_c, gt_c = jnp.split(mod_c, 3, axis=-1)
        hx = _rms(x, norm_w[l]) * (1.0 + sc[:, None, :]) + sh[:, None, :]
        hc = _rms(ctx, norm_w[l]) * (1.0 + sc_c) + sh_c
        ux = _split(hx @ w_in[l], [M_COLS, A_COLS, G_COLS, D_COLS])
        uc = _split(hc @ w_in[l], [M_COLS, A_COLS, G_COLS, D_COLS])
        lam_init = 0.8 - 0.6 * math.exp(-0.3 * l)
        outs = [
            _mlstm_mixer(uc[0], ux[0], mlstm_b_i[l], mlstm_b_f[l], mlstm_norm[l], need_ctx),
            _diff_mixer(uc[1], ux[1], cos, sin, diff_q_norm[l], diff_k_norm[l], diff_lambda[l],
                        diff_norm[l], lam_init, need_ctx),
            _gla_mixer(uc[2], ux[2], gla_w_up[l], gla_b[l], gla_norm[l], need_ctx),
            _gdn_mixer(uc[3], ux[3], gdn_conv[l], gdn_a_log[l], gdn_dt_bias[l], gdn_norm[l], need_ctx),
        ]
        yx = jnp.concatenate([o[1] for o in outs], axis=-1)
        x = x + gt[:, None, :] * (yx @ w_out[l])
        if need_ctx:
            yc = jnp.concatenate([o[0] for o in outs], axis=-1)
            ctx = ctx + gt_c * (yc @ w_out[l])
    return x
```

```python
import functools
import math

import jax
import jax.numpy as jnp
import numpy as np
from jax import lax
from jax.experimental import pallas as pl
from jax.experimental.pallas import tpu as pltpu

F32 = jnp.float32
MXU_DT = jnp.bfloat16
EPS = 1e-6
N_HEADS = 4
HEAD_DIM = 64
GROUP_W = N_HEADS * HEAD_DIM
CHUNK = 64
STEP = 2 * CHUNK
DA_QK = 32
GLA_DK = 32
GLA_RANK = 16
GLA_TAU = 16.0
GDN_CONV = 5
GRID_W = 64
ROPE_BASE = 10000.0
TOKEN_TILE = 256
ATTN_TQ = 256
ATTN_TK = 768
VMEM_LIMIT = 48 * 1024 * 1024
NEG_BIG = -1e30


def _mm(a, b):
    return jnp.dot(a.astype(MXU_DT), b.astype(MXU_DT), preferred_element_type=F32)


def _mm_nt(a, b):
    return lax.dot_general(a.astype(MXU_DT), b.astype(MXU_DT), (((1,), (1,)), ((), ())),
                           preferred_element_type=F32)


def _mm_tn(a, b):
    return lax.dot_general(a.astype(MXU_DT), b.astype(MXU_DT), (((0,), (0,)), ((), ())),
                           preferred_element_type=F32)


def _split2(x):
    hi = x.astype(MXU_DT)
    lo = (x - hi.astype(F32)).astype(MXU_DT)
    return hi, lo


def _mm_sel(sel, x):
    hi, lo = _split2(x)
    s = sel.astype(MXU_DT)
    return (jnp.dot(s, hi, preferred_element_type=F32) + jnp.dot(s, lo, preferred_element_type=F32))


def _mm_sel_r(x, sel):
    hi, lo = _split2(x)
    s = sel.astype(MXU_DT)
    return (jnp.dot(hi, s, preferred_element_type=F32) + jnp.dot(lo, s, preferred_element_type=F32))


def _mm_sel_nt(x, sel):
    hi, lo = _split2(x)
    s = sel.astype(MXU_DT)
    dn = (((1,), (1,)), ((), ()))
    return (lax.dot_general(hi, s, dn, preferred_element_type=F32)
            + lax.dot_general(lo, s, dn, preferred_element_type=F32))


def _group_ones(width, group):
    g = np.arange(width) // group
    return jnp.asarray((g[:, None] == g[None, :]).astype(np.float32))


def _params(*sem):
    return pltpu.CompilerParams(dimension_semantics=sem, vmem_limit_bytes=VMEM_LIMIT)


def _mod_body(cc_ref, w_ref, b_ref, o_ref):
    o_ref[...] = _mm(jax.nn.silu(cc_ref[...]), w_ref[...]) + b_ref[...]


def _modulation(cc, w_mod, b_mod):
    depth, d, d3 = w_mod.shape
    tn = 1024
    return pl.pallas_call(
        _mod_body,
        out_shape=jax.ShapeDtypeStruct((depth, cc.shape[0], d3), F32),
        grid=(depth, d3 // tn),
        in_specs=[pl.BlockSpec(cc.shape, lambda l, j: (0, 0)),
                  pl.BlockSpec((None, d, tn), lambda l, j: (l, 0, j)),
                  pl.BlockSpec((None, 1, tn), lambda l, j: (l, 0, j))],
        out_specs=pl.BlockSpec((None, cc.shape[0], tn), lambda l, j: (l, 0, j)),
        compiler_params=_params("parallel", "parallel"),
        name="modulation",
    )(cc, w_mod, b_mod.reshape(depth, 1, d3))


_SEG = dict(m_qkv=(0, 768), m_oz=(768, 1280), m_g=(1280, 1408), a_qk=(1408, 1920), a_vz=(1920, 2432),
            g_qk=(2432, 2688), g_vz=(2688, 3200), g_r=(3200, 3328), d_qkv=(3328, 4096), d_z=(4096, 4352),
            d_g=(4352, 4480))
P_PAD = 4480


def _permute_w_in(w):
    d = w.shape[0]
    z = lambda n: jnp.zeros((d, n), w.dtype)
    m0, a0, g0, d0 = 0, 1296, 2320, 3120
    cols = [w[:, m0:m0 + 1280], w[:, m0 + 1280:m0 + 1296], z(112),
            w[:, a0:a0 + 512], w[:, a0 + 512:a0 + 1024],
            w[:, g0:g0 + 256], w[:, g0 + 256:g0 + 768], w[:, g0 + 768:g0 + 800], z(96),
            w[:, d0:d0 + 768], w[:, d0 + 768:d0 + 1024], w[:, d0 + 1024:d0 + 1040], z(112)]
    wp = jnp.concatenate(cols, axis=1)
    wg = jnp.concatenate([wp[:, 1280:1408], wp[:, 4352:4480]], axis=1).T
    return wp.astype(MXU_DT), wg.astype(MXU_DT)


def _gate_fns(xm, xd, pm, pd, lane):
    gm = jnp.where(lane < 8, xm + pm[0], jax.nn.log_sigmoid(xm + pm[0]))
    gd = jnp.where(lane < 8, jax.nn.sigmoid(xd), -jnp.exp(pd[0]) * jax.nn.softplus(xd + pd[1]))
    return gm, gd


def _inproj_body(x_ref, nw_ref, mod_ref, wp_ref, wg_ref, pm_ref, pd_ref, pmT_ref, pdT_ref, wup_ref, bgk_ref,
                 m_qkv, m_oz, m_g, m_gT, a_qk, a_vz, g_qk, g_vz, g_la, d_qkv, d_z, d_g, d_gT):
    x = x_ref[...]
    h = x * lax.rsqrt(jnp.mean(x * x, axis=-1, keepdims=True) + EPS) * nw_ref[...]
    h = (h * (1.0 + mod_ref[1:2, :]) + mod_ref[0:1, :]).astype(MXU_DT)
    u = jnp.dot(h, wp_ref[...], preferred_element_type=F32)
    seg = lambda name: u[:, _SEG[name][0]:_SEG[name][1]]
    m_qkv[...] = seg("m_qkv")
    m_oz[...] = seg("m_oz")
    a_qk[...] = seg("a_qk")
    a_vz[...] = seg("a_vz")
    g_qk[...] = seg("g_qk")
    g_vz[...] = seg("g_vz")
    d_qkv[...] = seg("d_qkv")
    d_z[...] = seg("d_z")
    lane = lax.broadcasted_iota(jnp.int32, (1, 128), 1)
    gm, gd = _gate_fns(seg("m_g"), seg("d_g"), (pm_ref[0:1, :],), (pd_ref[0:1, :], pd_ref[1:2, :]), lane)
    m_g[...] = gm
    d_g[...] = gd
    ut = lax.dot_general(wg_ref[...], h, (((1,), (1,)), ((), ())), preferred_element_type=F32)
    row = lax.broadcasted_iota(jnp.int32, (128, 1), 0)
    gmt, gdt = _gate_fns(ut[0:128], ut[128:256], (pmT_ref[:, 0:1],), (pdT_ref[:, 0:1], pdT_ref[:, 1:2]), row)
    m_gT[...] = gmt
    d_gT[...] = gdt
    gk = _mm(seg("g_r"), wup_ref[...]) + bgk_ref[...]
    g_la[...] = jax.nn.log_sigmoid(gk) / GLA_TAU


def _inproj(xs, nw, modsel, wp, wg, pm, pd, wup, bgk, n_ctx_tiles):
    b, t, d = xs.shape
    tm = TOKEN_TILE
    tok = lambda w: pl.BlockSpec((None, tm, w), lambda bi, i: (bi, i, 0))
    tokT = pl.BlockSpec((None, 128, tm), lambda bi, i: (bi, 0, i))
    full = lambda a: pl.BlockSpec(a.shape, lambda bi, i: (0,) * a.ndim)
    widths = [768, 512, 128, None, 512, 512, 256, 512, 256, 768, 256, 128, None]
    out_shape, out_specs = [], []
    for w in widths:
        if w is None:
            out_shape.append(jax.ShapeDtypeStruct((b, 128, t), F32))
            out_specs.append(tokT)
        else:
            out_shape.append(jax.ShapeDtypeStruct((b, t, w), F32))
            out_specs.append(tok(w))
    pmT, pdT = pm.T, pd.T
    return pl.pallas_call(
        _inproj_body,
        out_shape=out_shape,
        grid=(b, t // tm),
        in_specs=[tok(d), full(nw),
                  pl.BlockSpec((None, None, 2, d), lambda bi, i: (bi, (i >= n_ctx_tiles).astype(jnp.int32), 0, 0)),
                  full(wp), full(wg), full(pm), full(pd), full(pmT), full(pdT), full(wup), full(bgk)],
        out_specs=out_specs,
        compiler_params=_params("parallel", "parallel"),
        name="inproj",
    )(xs, nw, modsel, wp, wg, pm, pd, pmT, pdT, wup, bgk)


def _cum_mats():
    i = np.arange(CHUNK)
    fwd = (i[None, :] <= i[:, None]).astype(np.float32)
    return jnp.asarray(np.stack([fwd, fwd.T]))


def _step_block(d, n, n_ctx_steps, n_steps):
    bwd = jnp.where(n < n_ctx_steps, n_ctx_steps - 1 - n, n_steps - 1 - (n - n_ctx_steps))
    return jnp.where(d == 0, n, bwd)


def _sub_chunk(d, j):
    return jnp.where(d == 0, j, 1 - j)


def _lane_half(x, c):
    return jnp.where(c == 0, x[:, :CHUNK], x[:, CHUNK:])


def _mlstm_body(qkv_ref, g_ref, gT_ref, cm_ref, o_ref, c_ref, m_ref):
    d = pl.program_id(1)

    @pl.when(pl.program_id(2) == 0)
    def _():
        c_ref[...] = jnp.zeros_like(c_ref)
        m_ref[...] = jnp.zeros_like(m_ref)

    cm = cm_ref[...]
    gT = gT_ref[...]
    ones = jnp.ones((CHUNK, HEAD_DIM), F32)
    for j in range(2):
        c = _sub_chunk(d, j)
        off = pl.multiple_of(c * CHUNK, CHUNK)
        g = g_ref[pl.ds(off, CHUNK), :]
        gt = _lane_half(gT, c)
        bc = _mm_sel(cm, g)
        br = _mm_sel_nt(gt, cm)
        tot = jnp.sum(g, axis=0, keepdims=True)
        qkv = qkv_ref[pl.ds(off, CHUNK), :]
        for h in range(N_HEADS):
            sel = lambda a: jnp.where(d == 0, a[:, h:h + 1], a[:, 4 + h:5 + h])
            selr = lambda a: jnp.where(d == 0, a[h:h + 1, :], a[4 + h:5 + h, :])
            i_col, b_col = sel(g[:, 0:8]), sel(bc[:, 8:16])
            i_row, b_row = selr(gt[0:8, :]), selr(br[8:16, :])
            b_last = sel(tot[:, 8:16])
            q = qkv[:, h * HEAD_DIM:(h + 1) * HEAD_DIM]
            k = qkv[:, GROUP_W + h * HEAD_DIM:GROUP_W + (h + 1) * HEAD_DIM] * HEAD_DIM ** -0.5
            v1 = jnp.concatenate([qkv[:, 2 * GROUP_W + h * HEAD_DIM:2 * GROUP_W + (h + 1) * HEAD_DIM], ones], axis=1)
            m_prev = m_ref[h][:, 0:1]
            cst = c_ref[h]
            dmat = jnp.where(cm > 0, b_col - b_row + i_row, -jnp.inf)
            inter = b_col + m_prev
            m_t = jnp.maximum(inter, jnp.max(dmat, axis=1, keepdims=True))
            s = _mm_nt(q, k) * jnp.exp(dmat - m_t)
            w_inter = jnp.exp(inter - m_t)
            acc = w_inter * _mm(q, cst) + _mm(s, v1)
            den = jnp.maximum(jnp.abs(acc[:, HEAD_DIM:HEAD_DIM + 1]), jnp.exp(-m_t))
            o_ref[pl.ds(off, CHUNK), h * HEAD_DIM:(h + 1) * HEAD_DIM] = acc[:, :HEAD_DIM] / den
            d_s = b_last - b_col + i_col
            m_new = jnp.maximum(b_last + m_prev, jnp.max(d_s, axis=0, keepdims=True))
            w_s = jnp.exp(d_s - m_new)
            w_c = jnp.exp(b_last + m_prev - m_new)
            c_ref[h] = w_c * cst + _mm_tn(k * w_s, v1)
            m_ref[h] = jnp.broadcast_to(m_new, (1, 128))


def _rec_specs(width, n_ctx_steps, n_steps):
    blk = lambda d, n: _step_block(d, n, n_ctx_steps, n_steps)
    tok = pl.BlockSpec((None, STEP, width), lambda b, d, n: (b, blk(d, n), 0))
    return tok, blk


def _mlstm(m_qkv, m_g, m_gT, cms, n_ctx_steps):
    b, t, _ = m_qkv.shape
    n_steps = t // STEP
    tok, blk = _rec_specs(3 * GROUP_W, n_ctx_steps, n_steps)
    return pl.pallas_call(
        _mlstm_body,
        out_shape=jax.ShapeDtypeStruct((b, 2, t, GROUP_W), F32),
        grid=(b, 2, n_steps),
        in_specs=[tok,
                  pl.BlockSpec((None, STEP, 128), lambda bi, d, n: (bi, blk(d, n), 0)),
                  pl.BlockSpec((None, 128, STEP), lambda bi, d, n: (bi, 0, blk(d, n))),
                  pl.BlockSpec((None, CHUNK, CHUNK), lambda bi, d, n: (d, 0, 0))],
        out_specs=pl.BlockSpec((None, None, STEP, GROUP_W), lambda bi, d, n: (bi, d, blk(d, n), 0)),
        scratch_shapes=[pltpu.VMEM((N_HEADS, HEAD_DIM, 128), F32), pltpu.VMEM((N_HEADS, 1, 128), F32)],
        compiler_params=_params("parallel", "parallel", "arbitrary"),
        name="mlstm",
    )(m_qkv, m_g, m_gT, cms)


def _gla_level_mats():
    L = CHUNK
    t = np.arange(L)
    sels, valids, pairs = [], [], []
    for rev in (False, True):
        pos_all = t if not rev else L - 1 - t
        eqs, eks, vq, vk, pm = [], [], [], [], []
        for j in range(6):
            w = 32 >> j
            par = pos_all // (2 * w)
            pos = pos_all % (2 * w)
            same = par[:, None] == par[None, :]
            late = pos >= w
            eq = same & late[:, None] & late[None, :] & (pos[None, :] <= pos[:, None])
            ek = same & (~late)[:, None] & (~late)[None, :] & (pos[None, :] > pos[:, None])
            eqs.append(eq)
            eks.append(ek)
            vq.append(late)
            vk.append(~late)
            pm.append(same & late[:, None] & (~late)[None, :])
        pm.append(np.eye(L, dtype=bool))
        sels.append(np.concatenate(eqs + eks, axis=0))
        valids.append(np.concatenate(vq + vk)[:, None])
        pairs.append(np.concatenate(pm, axis=0))
    f = lambda a: jnp.asarray(np.stack(a).astype(np.float32))
    return f(sels), f(valids), f(pairs)


def _gla_body(qk_ref, v_ref, la_ref, cm_ref, sel_ref, val_ref, pair_ref, o_ref, s_ref):
    d = pl.program_id(1)

    @pl.when(pl.program_id(2) == 0)
    def _():
        s_ref[...] = jnp.zeros_like(s_ref)

    cm = cm_ref[...]
    sel = sel_ref[...]
    val = val_ref[...]
    pair = pair_ref[...]
    nk = N_HEADS * GLA_DK
    lane = lax.broadcasted_iota(jnp.int32, (1, nk), 1)
    for j in range(2):
        c = _sub_chunk(d, j)
        off = pl.multiple_of(c * CHUNK, CHUNK)
        la2 = la_ref[pl.ds(off, CHUNK), :]
        la = jnp.where(d == 0, la2[:, :nk], la2[:, nk:])
        qk = qk_ref[pl.ds(off, CHUNK), :]
        q = qk[:, :nk] * GLA_DK ** -0.5
        k = qk[:, nk:]
        v = v_ref[pl.ds(off, CHUNK), :]
        bcum = _mm_sel(cm, la)
        b_last = jnp.sum(la, axis=0, keepdims=True)
        ex = jnp.exp(_mm_sel(sel, la)) * val
        q_in = q * jnp.exp(bcum)
        k_out = k * jnp.exp(b_last - bcum)
        for h in range(N_HEADS):
            hm = (lane >= h * GLA_DK) & (lane < (h + 1) * GLA_DK)
            a = pair[6 * CHUNK:7 * CHUNK] * _mm_nt(jnp.where(hm, q, 0.0), k)
            for lv in range(6):
                ql = jnp.where(hm, q * ex[lv * CHUNK:(lv + 1) * CHUNK], 0.0)
                kl = k * ex[(6 + lv) * CHUNK:(7 + lv) * CHUNK]
                a = a + pair[lv * CHUNK:(lv + 1) * CHUNK] * _mm_nt(ql, kl)
            vh = v[:, h * HEAD_DIM:(h + 1) * HEAD_DIM]
            sh = s_ref[h]
            o = _mm(jnp.where(hm, q_in, 0.0), sh) + _mm(a, vh)
            o_ref[pl.ds(off, CHUNK), h * HEAD_DIM:(h + 1) * HEAD_DIM] = o
            decay = jnp.exp(b_last)
            upd = _mm_tn(jnp.where(hm, k_out, 0.0), vh)
            s_ref[h] = _row_scale(decay, sh) + upd


def _row_scale(row, mat):
    n = mat.shape[0]
    eye = (lax.broadcasted_iota(jnp.int32, (n, n), 0) == lax.broadcasted_iota(jnp.int32, (n, n), 1))
    col = jnp.sum(jnp.where(eye, row, 0.0), axis=1, keepdims=True)
    return col * mat


def _gla(g_qk, g_vz, g_la, cms, consts, n_ctx_steps):
    b, t, _ = g_qk.shape
    n_steps = t // STEP
    blk = lambda d, n: _step_block(d, n, n_ctx_steps, n_steps)
    sel, val, pair = consts
    dsel = lambda a: pl.BlockSpec((None,) + a.shape[1:], lambda bi, d, n: (d, 0, 0))
    return pl.pallas_call(
        _gla_body,
        out_shape=jax.ShapeDtypeStruct((b, 2, t, GROUP_W), F32),
        grid=(b, 2, n_steps),
        in_specs=[pl.BlockSpec((None, STEP, 256), lambda bi, d, n: (bi, blk(d, n), 0)),
                  pl.BlockSpec((None, STEP, 256), lambda bi, d, n: (bi, blk(d, n), 0)),
                  pl.BlockSpec((None, STEP, 256), lambda bi, d, n: (bi, blk(d, n), 0)),
                  dsel(cms), dsel(sel), dsel(val), dsel(pair)],
        out_specs=pl.BlockSpec((None, None, STEP, GROUP_W), lambda bi, d, n: (bi, d, blk(d, n), 0)),
        scratch_shapes=[pltpu.VMEM((N_HEADS, N_HEADS * GLA_DK, HEAD_DIM), F32)],
        compiler_params=_params("parallel", "parallel", "arbitrary"),
        name="gla",
    )(g_qk, g_vz, g_la, cms, sel, val, pair)


def _gdn_prep_body(prev_ref, cur_ref, next_ref, w_ref, ind_ref, o_ref, ext_ref, *, n_ctx_tiles, n_tiles):
    i = pl.program_id(1)
    tm = cur_ref.shape[0]
    first = (i == 0) | (i == n_ctx_tiles)
    last = (i == n_ctx_tiles - 1) | (i == n_tiles - 1)
    ext_ref[0:8, :] = jnp.where(first, 0.0, prev_ref[...])
    ext_ref[8:8 + tm, :] = cur_ref[...]
    ext_ref[8 + tm:16 + tm, :] = jnp.where(last, 0.0, next_ref[...])
    half = GDN_CONV // 2
    y = jnp.zeros(cur_ref.shape, F32)
    for j in range(GDN_CONV):
        y = y + ext_ref[8 - half + j:8 - half + j + tm, :] * w_ref[j:j + 1, :]
    y = jax.nn.silu(y)
    ind = ind_ref[...]
    q, k = y[:, :GROUP_W], y[:, GROUP_W:2 * GROUP_W]
    qn = q * lax.rsqrt(_mm_sel_r(q * q, ind) + EPS) * HEAD_DIM ** -0.5
    kn = k * lax.rsqrt(_mm_sel_r(k * k, ind) + EPS)
    o_ref[:, :GROUP_W] = qn
    o_ref[:, GROUP_W:2 * GROUP_W] = kn
    o_ref[:, 2 * GROUP_W:] = y[:, 2 * GROUP_W:]


def _gdn_prep(d_qkv, conv_w, ind64, n_ctx_tiles):
    b, t, w = d_qkv.shape
    tm = TOKEN_TILE
    n_tiles = t // tm
    r = tm // 8
    n8 = t // 8
    return pl.pallas_call(
        functools.partial(_gdn_prep_body, n_ctx_tiles=n_ctx_tiles, n_tiles=n_tiles),
        out_shape=jax.ShapeDtypeStruct((b, t, w), F32),
        grid=(b, n_tiles),
        in_specs=[pl.BlockSpec((None, 8, w), lambda bi, i: (bi, jnp.maximum(i * r - 1, 0), 0)),
                  pl.BlockSpec((None, tm, w), lambda bi, i: (bi, i, 0)),
                  pl.BlockSpec((None, 8, w), lambda bi, i: (bi, jnp.minimum((i + 1) * r, n8 - 1), 0)),
                  pl.BlockSpec(conv_w.shape, lambda bi, i: (0, 0)),
                  pl.BlockSpec(ind64.shape, lambda bi, i: (0, 0))],
        out_specs=pl.BlockSpec((None, tm, w), lambda bi, i: (bi, i, 0)),
        scratch_shapes=[pltpu.VMEM((tm + 16, w), F32)],
        compiler_params=_params("parallel", "parallel"),
        name="gdn_prep",
    )(d_qkv, d_qkv, d_qkv, conv_w, ind64)


def _gdn_body(qkv_ref, g_ref, gT_ref, cm_ref, pair_ref, o_ref, s_ref):
    d = pl.program_id(1)

    @pl.when(pl.program_id(2) == 0)
    def _():
        s_ref[...] = jnp.zeros_like(s_ref)

    cm = cm_ref[...]
    pair = pair_ref[...]
    gT = gT_ref[...]
    eye = (lax.broadcasted_iota(jnp.int32, (CHUNK, CHUNK), 0)
           == lax.broadcasted_iota(jnp.int32, (CHUNK, CHUNK), 1))
    strict = jnp.where(eye, 0.0, cm)
    for j in range(2):
        c = _sub_chunk(d, j)
        off = pl.multiple_of(c * CHUNK, CHUNK)
        g = g_ref[pl.ds(off, CHUNK), :]
        gt = _lane_half(gT, c)
        gc = _mm_sel(cm, g)
        gr = _mm_sel_nt(gt, cm)
        tot = jnp.sum(g, axis=0, keepdims=True)
        qkv = qkv_ref[pl.ds(off, CHUNK), :]
        for h in range(N_HEADS):
            sel = lambda a: jnp.where(d == 0, a[:, h:h + 1], a[:, 4 + h:5 + h])
            selr = lambda a: jnp.where(d == 0, a[h:h + 1, :], a[4 + h:5 + h, :])
            beta, g_col = sel(g[:, 0:8]), sel(gc[:, 8:16])
            g_row = selr(gr[8:16, :])
            g_last = sel(tot[:, 8:16])
            q = qkv[:, h * HEAD_DIM:(h + 1) * HEAD_DIM]
            k = qkv[:, GROUP_W + h * HEAD_DIM:GROUP_W + (h + 1) * HEAD_DIM]
            v = qkv[:, 2 * GROUP_W + h * HEAD_DIM:2 * GROUP_W + (h + 1) * HEAD_DIM]
            decay = jnp.exp(jnp.where(cm > 0, g_col - g_row, -jnp.inf))
            kb = k * beta
            lower = strict * _mm_nt(kb, k) * decay
            tm_ = jnp.where(eye, 1.0, 0.0) - lower * pair[5 * CHUNK:6 * CHUNK]
            for lv in range(4, -1, -1):
                tm_ = tm_ - _mm(tm_, _mm(lower * pair[lv * CHUNK:(lv + 1) * CHUNK], tm_))
            u = _mm(tm_, v * beta)
            w = _mm(tm_, kb * jnp.exp(g_col))
            attn = _mm_nt(q, k) * decay
            st = s_ref[h]
            v_new = u - _mm(w, st)
            o = _mm(q * jnp.exp(g_col), st) + _mm(attn, v_new)
            o_ref[pl.ds(off, CHUNK), h * HEAD_DIM:(h + 1) * HEAD_DIM] = o
            s_ref[h] = jnp.exp(g_last) * st + _mm_tn(k * jnp.exp(g_last - g_col), v_new)


def _gdn(d_qkv, d_g, d_gT, cms, pair, n_ctx_steps):
    b, t, _ = d_qkv.shape
    n_steps = t // STEP
    tok, blk = _rec_specs(3 * GROUP_W, n_ctx_steps, n_steps)
    return pl.pallas_call(
        _gdn_body,
        out_shape=jax.ShapeDtypeStruct((b, 2, t, GROUP_W), F32),
        grid=(b, 2, n_steps),
        in_specs=[tok,
                  pl.BlockSpec((None, STEP, 128), lambda bi, d, n: (bi, blk(d, n), 0)),
                  pl.BlockSpec((None, 128, STEP), lambda bi, d, n: (bi, 0, blk(d, n))),
                  pl.BlockSpec((None, CHUNK, CHUNK), lambda bi, d, n: (d, 0, 0)),
                  pl.BlockSpec((None,) + pair.shape[1:], lambda bi, d, n: (d, 0, 0))],
        out_specs=pl.BlockSpec((None, None, STEP, GROUP_W), lambda bi, d, n: (bi, d, blk(d, n), 0)),
        scratch_shapes=[pltpu.VMEM((N_HEADS, HEAD_DIM, HEAD_DIM), F32)],
        compiler_params=_params("parallel", "parallel", "arbitrary"),
        name="gdn",
    )(d_qkv, d_g, d_gT, cms, pair)


def _rope_tables(n_ctx, n_lat):
    rows = n_lat // GRID_W
    row = np.repeat(np.arange(rows), GRID_W).astype(np.float32)
    col = np.tile(np.arange(GRID_W), rows).astype(np.float32)
    half = DA_QK // 2
    inv = np.power(ROPE_BASE, -np.arange(0, half, 2, dtype=np.float32) / half).astype(np.float32)

    def tab(p):
        ang = p[:, None] * inv
        ang = np.concatenate([ang, ang], axis=-1)
        return np.cos(ang), np.sin(ang)

    cr, sr = tab(row)
    cc, sc = tab(col)
    cos = np.concatenate([cr, cc], -1)
    sin = np.concatenate([sr, sc], -1)
    cos = np.concatenate([np.ones((n_ctx, DA_QK), np.float32), cos], 0)
    sin = np.concatenate([np.zeros((n_ctx, DA_QK), np.float32), sin], 0)
    reps = GROUP_W // DA_QK
    cos, sin = np.tile(cos, (1, reps)), np.tile(sin, (1, reps))
    first = (np.arange(GROUP_W) % 16) < 8
    sin_a = np.where(first, -sin, 0.0)
    sin_b = np.where(first, 0.0, sin)
    return jnp.asarray(cos, F32), jnp.asarray(sin_a, F32), jnp.asarray(sin_b, F32)


def _diff_prep_body(qk_ref, v_ref, cos_ref, sa_ref, sb_ref, nw_ref, ind_ref, q_out, kt_out, v_out):
    ind = ind_ref[...]
    cos, sa, sb = cos_ref[...], sa_ref[...], sb_ref[...]

    def norm_rope(x, w):
        xn = x * lax.rsqrt(_mm_sel_r(x * x, ind) * (1.0 / DA_QK) + EPS) * w
        return xn * cos + pltpu.roll(xn, GROUP_W - 8, 1) * sa + pltpu.roll(xn, 8, 1) * sb

    qk = qk_ref[...]
    q = norm_rope(qk[:, :GROUP_W], nw_ref[0:1, :]) * DA_QK ** -0.5
    k = norm_rope(qk[:, GROUP_W:], nw_ref[1:2, :])
    for g in range(GROUP_W // DA_QK):
        q_out[g] = q[:, g * DA_QK:(g + 1) * DA_QK].astype(q_out.dtype)
    kt_out[...] = k.T.astype(kt_out.dtype)
    v = v_ref[...]
    ones = jnp.ones((v.shape[0], HEAD_DIM), F32)
    for h in range(N_HEADS):
        v_out[h] = jnp.concatenate([v[:, h * HEAD_DIM:(h + 1) * HEAD_DIM], ones], axis=1).astype(v_out.dtype)


def _diff_prep(a_qk, a_vz, tables, nw, ind32):
    b, t, _ = a_qk.shape
    tm = TOKEN_TILE
    cos, sa, sb = tables
    tab = pl.BlockSpec((tm, GROUP_W), lambda bi, i: (i, 0))
    ng = GROUP_W // DA_QK
    return pl.pallas_call(
        _diff_prep_body,
        out_shape=[jax.ShapeDtypeStruct((b, ng, t, DA_QK), MXU_DT),
                   jax.ShapeDtypeStruct((b, GROUP_W, t), MXU_DT),
                   jax.ShapeDtypeStruct((b, N_HEADS, t, 128), MXU_DT)],
        grid=(b, t // tm),
        in_specs=[pl.BlockSpec((None, tm, 2 * GROUP_W), lambda bi, i: (bi, i, 0)),
                  pl.BlockSpec((None, tm, GROUP_W), lambda bi, i: (bi, i, 0)),
                  tab, tab, tab,
                  pl.BlockSpec(nw.shape, lambda bi, i: (0, 0)),
                  pl.BlockSpec(ind32.shape, lambda bi, i: (0, 0))],
        out_specs=[pl.BlockSpec((None, ng, tm, DA_QK), lambda bi, i: (bi, 0, i, 0)),
                   pl.BlockSpec((None, GROUP_W, tm), lambda bi, i: (bi, 0, i)),
                   pl.BlockSpec((None, N_HEADS, tm, 128), lambda bi, i: (bi, 0, i, 0))],
        compiler_params=_params("parallel", "parallel"),
        name="diff_prep",
    )(a_qk, a_vz, cos, sa, sb, nw, ind32)


def _diff_attn_body(lam_ref, q_ref, kt_ref, v_ref, o_ref, acc_ref, m_ref, *, lam_init):
    j = pl.program_id(3)

    @pl.when(j == 0)
    def _():
        acc_ref[...] = jnp.zeros_like(acc_ref)
        m_ref[...] = jnp.full_like(m_ref, -jnp.inf)

    v = v_ref[...]
    for p in range(2):
        s = jnp.dot(q_ref[p], kt_ref[p * DA_QK:(p + 1) * DA_QK, :], preferred_element_type=F32)
        m_old = m_ref[p][:, 0:1]
        m_new = jnp.maximum(m_old, jnp.max(s, axis=1, keepdims=True))
        e = jnp.exp(s - m_new)
        acc_ref[p] = jnp.exp(m_old - m_new) * acc_ref[p] + _mm(e, v)
        m_ref[p] = jnp.broadcast_to(m_new, m_ref.shape[1:])

    @pl.when(j == pl.num_programs(3) - 1)
    def _():
        lam = lam_ref[...]
        lam_full = (jnp.exp(jnp.sum(lam[0:1] * lam[1:2], keepdims=True))
                    - jnp.exp(jnp.sum(lam[2:3] * lam[3:4], keepdims=True)) + lam_init)
        a0, a1 = acc_ref[0], acc_ref[1]
        o_ref[...] = (a0[:, :HEAD_DIM] / a0[:, HEAD_DIM:HEAD_DIM + 1]
                      - lam_full * (a1[:, :HEAD_DIM] / a1[:, HEAD_DIM:HEAD_DIM + 1]))


def _diff_attn(lam, q, kt, v1, lam_init, q_start, n_q, n_keys, tq, tk):
    b = q.shape[0]
    q0 = q_start // tq
    return pl.pallas_call(
        functools.partial(_diff_attn_body, lam_init=lam_init),
        out_shape=jax.ShapeDtypeStruct((b, N_HEADS, n_q, HEAD_DIM), F32),
        grid=(b, N_HEADS, n_q // tq, n_keys // tk),
        in_specs=[pl.BlockSpec(lam.shape, lambda bi, h, i, j: (0, 0)),
                  pl.BlockSpec((None, 2, tq, DA_QK), lambda bi, h, i, j: (bi, h, i + q0, 0)),
                  pl.BlockSpec((None, 2 * DA_QK, tk), lambda bi, h, i, j: (bi, h, j)),
                  pl.BlockSpec((None, None, tk, 128), lambda bi, h, i, j: (bi, h, j, 0))],
        out_specs=pl.BlockSpec((None, None, tq, HEAD_DIM), lambda bi, h, i, j: (bi, h, i, 0)),
        scratch_shapes=[pltpu.VMEM((2, tq, 128), F32), pltpu.VMEM((2, tq, 128), F32)],
        compiler_params=_params("parallel", "parallel", "parallel", "arbitrary"),
        name="diff_attn",
    )(lam, q, kt, v1)


def _finish_body(x_ref, mh_ref, moz_ref, ao_ref, az_ref, gh_ref, gz_ref, dh_ref, dz_ref, nw_ref, ind_ref,
                 wout_ref, gt_ref, o_ref, *, lam_init):
    ind = ind_ref[...]

    def head_norm(hh, w):
        return hh * lax.rsqrt(_mm_sel_r(hh * hh, ind) * (1.0 / HEAD_DIM) + EPS) * w

    moz = moz_ref[...]
    ym = (head_norm(mh_ref[0] + mh_ref[1], nw_ref[0:1, :]) * jax.nn.sigmoid(moz[:, :GROUP_W])
          * jax.nn.silu(moz[:, GROUP_W:]))
    ao = jnp.concatenate([ao_ref[h] for h in range(N_HEADS)], axis=1)
    ya = head_norm(ao, nw_ref[1:2, :]) * (1.0 - lam_init) * jax.nn.silu(az_ref[...])
    yg = head_norm(gh_ref[0] + gh_ref[1], nw_ref[2:3, :]) * jax.nn.silu(gz_ref[...])
    yd = head_norm(dh_ref[0] + dh_ref[1], nw_ref[3:4, :]) * jax.nn.silu(dz_ref[...])
    y = jnp.concatenate([ym, ya, yg, yd], axis=1).astype(MXU_DT)
    o_ref[...] = x_ref[...] + gt_ref[...] * jnp.dot(y, wout_ref[...], preferred_element_type=F32)


def _finish(xs, mh, m_oz, ao, a_vz, gh, g_vz, dh, d_z, nw, ind64, wout, gtsel, lam_init, n_ctx_tiles):
    b, t, d = xs.shape
    tm = TOKEN_TILE
    tok = lambda w, c=0: pl.BlockSpec((None, tm, w), lambda bi, i: (bi, i, c))
    two = pl.BlockSpec((None, 2, tm, GROUP_W), lambda bi, i: (bi, 0, i, 0))
    full = lambda a: pl.BlockSpec(a.shape, lambda bi, i: (0,) * a.ndim)
    return pl.pallas_call(
        functools.partial(_finish_body, lam_init=lam_init),
        out_shape=jax.ShapeDtypeStruct((b, t, d), F32),
        grid=(b, t // tm),
        in_specs=[tok(d), two, tok(2 * GROUP_W),
                  pl.BlockSpec((None, N_HEADS, tm, HEAD_DIM), lambda bi, i: (bi, 0, i, 0)),
                  tok(GROUP_W, 1), two, tok(GROUP_W, 1), two, tok(GROUP_W),
                  full(nw), full(ind64), full(wout),
                  pl.BlockSpec((None, None, 1, d), lambda bi, i: (bi, (i >= n_ctx_tiles).astype(jnp.int32), 0, 0))],
        out_specs=tok(d),
        compiler_params=_params("parallel", "parallel"),
        name="finish",
    )(xs, mh, m_oz, ao, a_vz, gh, g_vz, dh, d_z, nw, ind64, wout, gtsel)


def kernel(x, c, ctx, c_ctx, norm_w, w_mod, b_mod, w_in, w_out, mlstm_b_i, mlstm_b_f, mlstm_norm, diff_q_norm,
           diff_k_norm, diff_lambda, diff_norm, gla_w_up, gla_b, gla_norm, gdn_conv, gdn_a_log, gdn_dt_bias,
           gdn_norm):
    bsz, n_lat, d = x.shape
    n_ctx = ctx.shape[1]
    depth = w_in.shape[0]
    t = n_ctx + n_lat
    assert n_ctx % TOKEN_TILE == 0 and n_lat % TOKEN_TILE == 0 and n_lat % GRID_W == 0
    assert n_ctx % ATTN_TQ == 0 and n_lat % ATTN_TQ == 0 and t % ATTN_TK == 0
    n_ctx_tiles = n_ctx // TOKEN_TILE
    n_ctx_steps = n_ctx // STEP

    cc = jnp.concatenate([c, c_ctx[None, :], jnp.zeros((8 - (bsz + 1) % 8, d), F32)], axis=0)
    mod = _modulation(cc, w_mod, b_mod)

    cms = _cum_mats()
    gla_consts = _gla_level_mats()
    ind64 = _group_ones(GROUP_W, HEAD_DIM)
    ind32 = _group_ones(GROUP_W, DA_QK)
    tables = _rope_tables(n_ctx, n_lat)
    pad = lambda a, n: jnp.concatenate([a, jnp.zeros(a.shape[:-1] + (n - a.shape[-1],), a.dtype)], axis=-1)

    xs = jnp.concatenate([ctx, x], axis=1)
    for l in range(depth):
        lam_init = 0.8 - 0.6 * math.exp(-0.3 * l)
        sh, sc, gt = jnp.split(mod[l], 3, axis=-1)
        lat = jnp.stack([sh[:bsz], sc[:bsz]], axis=1)
        cx = jnp.broadcast_to(jnp.stack([sh[bsz], sc[bsz]], axis=0)[None], (bsz, 2, d))
        modsel = jnp.stack([cx, lat], axis=1)
        gtsel = jnp.stack([jnp.broadcast_to(gt[bsz][None], (bsz, d)), gt[:bsz]], axis=1)[:, :, None, :]
        wp, wg = _permute_w_in(w_in[l])
        pm = pad(jnp.concatenate([mlstm_b_i[l].reshape(1, 8), mlstm_b_f[l].reshape(1, 8)], axis=1), 128)
        pm = jnp.concatenate([pm, jnp.zeros((7, 128), F32)], axis=0)
        z8 = jnp.zeros((1, 8), F32)
        pd = jnp.concatenate([pad(jnp.concatenate([z8, gdn_a_log[l].reshape(1, 8)], axis=1), 128),
                              pad(jnp.concatenate([z8, gdn_dt_bias[l].reshape(1, 8)], axis=1), 128),
                              jnp.zeros((6, 128), F32)], axis=0)
        wup = jnp.zeros((128, 256), F32)
        wup = wup.at[0:GLA_RANK, 0:128].set(gla_w_up[l, 0]).at[GLA_RANK:2 * GLA_RANK, 128:256].set(gla_w_up[l, 1])
        bgk = gla_b[l].reshape(1, 256)

        (m_qkv, m_oz, m_g, m_gT, a_qk, a_vz, g_qk, g_vz, g_la, d_qkv, d_z, d_g, d_gT) = _inproj(
            xs, norm_w[l][None, :], modsel, wp, wg, pm, pd, wup.astype(MXU_DT), bgk, n_ctx_tiles)

        mh = _mlstm(m_qkv, m_g, m_gT, cms, n_ctx_steps)

        qkn = jnp.stack([jnp.tile(diff_q_norm[l], GROUP_W // DA_QK), jnp.tile(diff_k_norm[l], GROUP_W // DA_QK)])
        aq, akt, av1 = _diff_prep(a_qk, a_vz, tables, qkn, ind32)
        lam = pad(diff_lambda[l], 128)
        ao_ctx = _diff_attn(lam, aq, akt, av1, lam_init, 0, n_ctx, n_ctx, ATTN_TQ, n_ctx)
        ao_lat = _diff_attn(lam, aq, akt, av1, lam_init, n_ctx, n_lat, t, ATTN_TQ, ATTN_TK)
        ao = jnp.concatenate([ao_ctx, ao_lat], axis=2)

        gh = _gla(g_qk, g_vz, g_la, cms, gla_consts, n_ctx_steps)

        dq = _gdn_prep(d_qkv, gdn_conv[l], ind64, n_ctx_tiles)
        dh = _gdn(dq, d_g, d_gT, cms, gla_consts[2], n_ctx_steps)

        nws = jnp.stack([mlstm_norm[l], diff_norm[l], gla_norm[l], gdn_norm[l]])
        xs = _finish(xs, mh, m_oz, ao, a_vz, gh, g_vz, dh, d_z, nws, ind64, w_out[l].astype(MXU_DT), gtsel,
                     lam_init, n_ctx_tiles)
    return xs[:, n_ctx:]
```

```python
import functools
import math

import jax
import jax.numpy as jnp
import numpy as np
from jax import lax
from jax.experimental import pallas as pl
from jax.experimental.pallas import tpu as pltpu

F32 = jnp.float32
MXU_DT = jnp.bfloat16
EPS = 1e-6
N_HEADS = 4
HEAD_DIM = 64
GROUP_W = N_HEADS * HEAD_DIM
CHUNK = 64
STEP = 2 * CHUNK
DA_QK = 32
GLA_DK = 32
GLA_RANK = 16
GLA_TAU = 16.0
GDN_CONV = 5
GRID_W = 64
ROPE_BASE = 10000.0
TOKEN_TILE = 256
ATTN_TQ = 512
ATTN_TK = 768
VMEM_LIMIT = 48 * 1024 * 1024
NEG_BIG = -1e30


def _mm(a, b):
    return jnp.dot(a.astype(MXU_DT), b.astype(MXU_DT), preferred_element_type=F32)


def _mm_nt(a, b):
    return lax.dot_general(a.astype(MXU_DT), b.astype(MXU_DT), (((1,), (1,)), ((), ())),
                           preferred_element_type=F32)


def _mm_tn(a, b):
    return lax.dot_general(a.astype(MXU_DT), b.astype(MXU_DT), (((0,), (0,)), ((), ())),
                           preferred_element_type=F32)


def _split2(x):
    hi = x.astype(MXU_DT)
    lo = (x - hi.astype(F32)).astype(MXU_DT)
    return hi, lo


def _mm_sel(sel, x):
    hi, lo = _split2(x)
    s = sel.astype(MXU_DT)
    return (jnp.dot(s, hi, preferred_element_type=F32) + jnp.dot(s, lo, preferred_element_type=F32))


def _mm_sel_r(x, sel):
    hi, lo = _split2(x)
    s = sel.astype(MXU_DT)
    return (jnp.dot(hi, s, preferred_element_type=F32) + jnp.dot(lo, s, preferred_element_type=F32))


def _mm_sel_nt(x, sel):
    hi, lo = _split2(x)
    s = sel.astype(MXU_DT)
    dn = (((1,), (1,)), ((), ()))
    return (lax.dot_general(hi, s, dn, preferred_element_type=F32)
            + lax.dot_general(lo, s, dn, preferred_element_type=F32))


def _group_ones(width, group):
    g = np.arange(width) // group
    return jnp.asarray((g[:, None] == g[None, :]).astype(np.float32))


def _params(*sem):
    return pltpu.CompilerParams(dimension_semantics=sem, vmem_limit_bytes=VMEM_LIMIT)


def _mod_body(cc_ref, w_ref, b_ref, o_ref):
    o_ref[...] = _mm(jax.nn.silu(cc_ref[...]), w_ref[...]) + b_ref[...]


def _modulation(cc, w_mod, b_mod):
    depth, d, d3 = w_mod.shape
    tn = 1024
    return pl.pallas_call(
        _mod_body,
        out_shape=jax.ShapeDtypeStruct((depth, cc.shape[0], d3), F32),
        grid=(depth, d3 // tn),
        in_specs=[pl.BlockSpec(cc.shape, lambda l, j: (0, 0)),
                  pl.BlockSpec((None, d, tn), lambda l, j: (l, 0, j)),
                  pl.BlockSpec((None, 1, tn), lambda l, j: (l, 0, j))],
        out_specs=pl.BlockSpec((None, cc.shape[0], tn), lambda l, j: (l, 0, j)),
        compiler_params=_params("parallel", "parallel"),
        name="modulation",
    )(cc, w_mod, b_mod.reshape(depth, 1, d3))


_SEG = dict(m_qkv=(0, 768), m_oz=(768, 1280), m_g=(1280, 1408), a_qk=(1408, 1920), a_vz=(1920, 2432),
            g_qk=(2432, 2688), g_vz=(2688, 3200), g_r=(3200, 3328), d_qkv=(3328, 4096), d_z=(4096, 4352),
            d_g=(4352, 4480))
P_PAD = 4480


def _permute_w_in(w):
    d = w.shape[0]
    z = lambda n: jnp.zeros((d, n), w.dtype)
    m0, a0, g0, d0 = 0, 1296, 2320, 3120
    cols = [w[:, m0:m0 + 1280], w[:, m0 + 1280:m0 + 1296], z(112),
            w[:, a0:a0 + 512], w[:, a0 + 512:a0 + 1024],
            w[:, g0:g0 + 256], w[:, g0 + 256:g0 + 768], w[:, g0 + 768:g0 + 800], z(96),
            w[:, d0:d0 + 768], w[:, d0 + 768:d0 + 1024], w[:, d0 + 1024:d0 + 1040], z(112)]
    wp = jnp.concatenate(cols, axis=1)
    wg = jnp.concatenate([wp[:, 1280:1408], wp[:, 4352:4480]], axis=1).T
    return wp.astype(MXU_DT), wg.astype(MXU_DT)


def _gate_fns(xm, xd, pm, pd, lane):
    gm = jnp.where(lane < 8, xm + pm[0], jax.nn.log_sigmoid(xm + pm[0]))
    gd = jnp.where(lane < 8, jax.nn.sigmoid(xd), -jnp.exp(pd[0]) * jax.nn.softplus(xd + pd[1]))
    return gm, gd


def _inproj_body(x_ref, nw_ref, mod_ref, wp_ref, wg_ref, pm_ref, pd_ref, pmT_ref, pdT_ref, wup_ref, bgk_ref,
                 m_qkv, m_oz, m_g, m_gT, a_qk, a_vz, g_qk, g_vz, g_la, d_qkv, d_z, d_g, d_gT):
    x = x_ref[...]
    h = x * lax.rsqrt(jnp.mean(x * x, axis=-1, keepdims=True) + EPS) * nw_ref[...]
    h = (h * (1.0 + mod_ref[1:2, :]) + mod_ref[0:1, :]).astype(MXU_DT)
    u = jnp.dot(h, wp_ref[...], preferred_element_type=F32)
    seg = lambda name: u[:, _SEG[name][0]:_SEG[name][1]]
    m_qkv[...] = seg("m_qkv")
    m_oz[...] = seg("m_oz")
    a_qk[...] = seg("a_qk")
    a_vz[...] = seg("a_vz")
    g_qk[...] = seg("g_qk")
    g_vz[...] = seg("g_vz")
    d_qkv[...] = seg("d_qkv")
    d_z[...] = seg("d_z")
    lane = lax.broadcasted_iota(jnp.int32, (1, 128), 1)
    gm, gd = _gate_fns(seg("m_g"), seg("d_g"), (pm_ref[0:1, :],), (pd_ref[0:1, :], pd_ref[1:2, :]), lane)
    m_g[...] = gm
    d_g[...] = gd
    ut = lax.dot_general(wg_ref[...], h, (((1,), (1,)), ((), ())), preferred_element_type=F32)
    row = lax.broadcasted_iota(jnp.int32, (128, 1), 0)
    gmt, gdt = _gate_fns(ut[0:128], ut[128:256], (pmT_ref[:, 0:1],), (pdT_ref[:, 0:1], pdT_ref[:, 1:2]), row)
    m_gT[...] = gmt
    d_gT[...] = gdt
    gk = _mm(seg("g_r"), wup_ref[...]) + bgk_ref[...]
    g_la[...] = jax.nn.log_sigmoid(gk) / GLA_TAU


def _inproj(xs, nw, modsel, wp, wg, pm, pd, wup, bgk, n_ctx_tiles):
    b, t, d = xs.shape
    tm = TOKEN_TILE
    tok = lambda w: pl.BlockSpec((None, tm, w), lambda bi, i: (bi, i, 0))
    tokT = pl.BlockSpec((None, 128, tm), lambda bi, i: (bi, 0, i))
    full = lambda a: pl.BlockSpec(a.shape, lambda bi, i: (0,) * a.ndim)
    widths = [768, 512, 128, None, 512, 512, 256, 512, 256, 768, 256, 128, None]
    out_shape, out_specs = [], []
    for w in widths:
        if w is None:
            out_shape.append(jax.ShapeDtypeStruct((b, 128, t), F32))
            out_specs.append(tokT)
        else:
            out_shape.append(jax.ShapeDtypeStruct((b, t, w), F32))
            out_specs.append(tok(w))
    pmT, pdT = pm.T, pd.T
    return pl.pallas_call(
        _inproj_body,
        out_shape=out_shape,
        grid=(b, t // tm),
        in_specs=[tok(d), full(nw),
                  pl.BlockSpec((None, None, 2, d), lambda bi, i: (bi, (i >= n_ctx_tiles).astype(jnp.int32), 0, 0)),
                  full(wp), full(wg), full(pm), full(pd), full(pmT), full(pdT), full(wup), full(bgk)],
        out_specs=out_specs,
        compiler_params=_params("parallel", "parallel"),
        name="inproj",
    )(xs, nw, modsel, wp, wg, pm, pd, pmT, pdT, wup, bgk)


def _cum_mats_np():
    i = np.arange(CHUNK)
    fwd = (i[None, :] <= i[:, None]).astype(np.float32)
    return np.stack([fwd, fwd.T])


def _cum_mats():
    return jnp.asarray(_cum_mats_np())


def _step_block(d, n, n_ctx_steps, n_steps):
    bwd = jnp.where(n < n_ctx_steps, n_ctx_steps - 1 - n, n_steps - 1 - (n - n_ctx_steps))
    return jnp.where(d == 0, n, bwd)


def _sub_chunk(d, j):
    return jnp.where(d == 0, j, 1 - j)


def _lane_half(x, c):
    return jnp.where(c == 0, x[:, :CHUNK], x[:, CHUNK:])


def _mlstm_body(qkv_ref, g_ref, gT_ref, cm_ref, o_ref, c_ref, m_ref):
    d = pl.program_id(1)

    @pl.when(pl.program_id(2) == 0)
    def _():
        c_ref[...] = jnp.zeros_like(c_ref)
        m_ref[...] = jnp.zeros_like(m_ref)

    cm = cm_ref[...]
    gT = gT_ref[...]
    ones = jnp.ones((CHUNK, HEAD_DIM), F32)
    for j in range(2):
        c = _sub_chunk(d, j)
        off = pl.multiple_of(c * CHUNK, CHUNK)
        g = g_ref[pl.ds(off, CHUNK), :]
        gt = _lane_half(gT, c)
        bc = _mm_sel(cm, g)
        br = _mm_sel_nt(gt, cm)
        tot = jnp.sum(g, axis=0, keepdims=True)
        qkv = qkv_ref[pl.ds(off, CHUNK), :]
        for h in range(N_HEADS):
            sel = lambda a: jnp.where(d == 0, a[:, h:h + 1], a[:, 4 + h:5 + h])
            selr = lambda a: jnp.where(d == 0, a[h:h + 1, :], a[4 + h:5 + h, :])
            i_col, b_col = sel(g[:, 0:8]), sel(bc[:, 8:16])
            i_row, b_row = selr(gt[0:8, :]), selr(br[8:16, :])
            b_last = sel(tot[:, 8:16])
            q = qkv[:, h * HEAD_DIM:(h + 1) * HEAD_DIM]
            k = qkv[:, GROUP_W + h * HEAD_DIM:GROUP_W + (h + 1) * HEAD_DIM] * HEAD_DIM ** -0.5
            v1 = jnp.concatenate([qkv[:, 2 * GROUP_W + h * HEAD_DIM:2 * GROUP_W + (h + 1) * HEAD_DIM], ones], axis=1)
            m_prev = m_ref[h][:, 0:1]
            cst = c_ref[h]
            dmat = jnp.where(cm > 0, b_col - b_row + i_row, -jnp.inf)
            inter = b_col + m_prev
            m_t = jnp.maximum(inter, jnp.max(dmat, axis=1, keepdims=True))
            s = _mm_nt(q, k) * jnp.exp(dmat - m_t)
            w_inter = jnp.exp(inter - m_t)
            acc = w_inter * _mm(q, cst) + _mm(s, v1)
            den = jnp.maximum(jnp.abs(acc[:, HEAD_DIM:HEAD_DIM + 1]), jnp.exp(-m_t))
            o_ref[pl.ds(off, CHUNK), h * HEAD_DIM:(h + 1) * HEAD_DIM] = acc[:, :HEAD_DIM] / den
            d_s = b_last - b_col + i_col
            m_new = jnp.maximum(b_last + m_prev, jnp.max(d_s, axis=0, keepdims=True))
            w_s = jnp.exp(d_s - m_new)
            w_c = jnp.exp(b_last + m_prev - m_new)
            c_ref[h] = w_c * cst + _mm_tn(k * w_s, v1)
            m_ref[h] = jnp.broadcast_to(m_new, (1, 128))


def _rec_specs(width, n_ctx_steps, n_steps):
    blk = lambda d, n: _step_block(d, n, n_ctx_steps, n_steps)
    tok = pl.BlockSpec((None, STEP, width), lambda b, d, n: (b, blk(d, n), 0))
    return tok, blk


def _mlstm(m_qkv, m_g, m_gT, cms, n_ctx_steps):
    b, t, _ = m_qkv.shape
    n_steps = t // STEP
    tok, blk = _rec_specs(3 * GROUP_W, n_ctx_steps, n_steps)
    return pl.pallas_call(
        _mlstm_body,
        out_shape=jax.ShapeDtypeStruct((b, 2, t, GROUP_W), F32),
        grid=(b, 2, n_steps),
        in_specs=[tok,
                  pl.BlockSpec((None, STEP, 128), lambda bi, d, n: (bi, blk(d, n), 0)),
                  pl.BlockSpec((None, 128, STEP), lambda bi, d, n: (bi, 0, blk(d, n))),
                  pl.BlockSpec((None, CHUNK, CHUNK), lambda bi, d, n: (d, 0, 0))],
        out_specs=pl.BlockSpec((None, None, STEP, GROUP_W), lambda bi, d, n: (bi, d, blk(d, n), 0)),
        scratch_shapes=[pltpu.VMEM((N_HEADS, HEAD_DIM, 128), F32), pltpu.VMEM((N_HEADS, 1, 128), F32)],
        compiler_params=_params("parallel", "parallel", "arbitrary"),
        name="mlstm",
    )(m_qkv, m_g, m_gT, cms)


def _gla_level_mats_np():
    L = CHUNK
    t = np.arange(L)
    sels, valids, pairs = [], [], []
    for rev in (False, True):
        pos_all = t if not rev else L - 1 - t
        eqs, eks, vq, vk, pm = [], [], [], [], []
        for j in range(6):
            w = 32 >> j
            par = pos_all // (2 * w)
            pos = pos_all % (2 * w)
            same = par[:, None] == par[None, :]
            late = pos >= w
            eq = same & late[:, None] & late[None, :] & (pos[None, :] <= pos[:, None])
            ek = same & (~late)[:, None] & (~late)[None, :] & (pos[None, :] > pos[:, None])
            eqs.append(eq)
            eks.append(ek)
            vq.append(late)
            vk.append(~late)
            pm.append(same & late[:, None] & (~late)[None, :])
        pm.append(np.eye(L, dtype=bool))
        sels.append(np.concatenate(eqs + eks, axis=0))
        valids.append(np.concatenate(vq + vk)[:, None])
        pairs.append(np.concatenate(pm, axis=0))
    f = lambda a: np.stack(a).astype(np.float32)
    return f(sels), f(valids), f(pairs)


def _gla_level_mats():
    return tuple(jnp.asarray(a) for a in _gla_level_mats_np())


def _gla_body(qk_ref, v_ref, la_ref, cm_ref, sel_ref, val_ref, pair_ref, o_ref, s_ref):
    d = pl.program_id(1)

    @pl.when(pl.program_id(2) == 0)
    def _():
        s_ref[...] = jnp.zeros_like(s_ref)

    cm = cm_ref[...]
    sel = sel_ref[...]
    val = val_ref[...]
    pair = pair_ref[...]
    nk = N_HEADS * GLA_DK
    lane = lax.broadcasted_iota(jnp.int32, (1, nk), 1)
    for j in range(2):
        c = _sub_chunk(d, j)
        off = pl.multiple_of(c * CHUNK, CHUNK)
        la2 = la_ref[pl.ds(off, CHUNK), :]
        la = jnp.where(d == 0, la2[:, :nk], la2[:, nk:])
        qk = qk_ref[pl.ds(off, CHUNK), :]
        q = qk[:, :nk] * GLA_DK ** -0.5
        k = qk[:, nk:]
        v = v_ref[pl.ds(off, CHUNK), :]
        bcum = _mm_sel(cm, la)
        b_last = jnp.sum(la, axis=0, keepdims=True)
        ex = jnp.exp(_mm_sel(sel, la)) * val
        q_in = q * jnp.exp(bcum)
        k_out = k * jnp.exp(b_last - bcum)
        for h in range(N_HEADS):
            hm = (lane >= h * GLA_DK) & (lane < (h + 1) * GLA_DK)
            a = pair[6 * CHUNK:7 * CHUNK] * _mm_nt(jnp.where(hm, q, 0.0), k)
            for lv in range(6):
                ql = jnp.where(hm, q * ex[lv * CHUNK:(lv + 1) * CHUNK], 0.0)
                kl = k * ex[(6 + lv) * CHUNK:(7 + lv) * CHUNK]
                a = a + pair[lv * CHUNK:(lv + 1) * CHUNK] * _mm_nt(ql, kl)
            vh = v[:, h * HEAD_DIM:(h + 1) * HEAD_DIM]
            sh = s_ref[h]
            o = _mm(jnp.where(hm, q_in, 0.0), sh) + _mm(a, vh)
            o_ref[pl.ds(off, CHUNK), h * HEAD_DIM:(h + 1) * HEAD_DIM] = o
            decay = jnp.exp(b_last)
            upd = _mm_tn(jnp.where(hm, k_out, 0.0), vh)
            s_ref[h] = _row_scale(decay, sh) + upd


def _row_scale(row, mat):
    n = mat.shape[0]
    eye = (lax.broadcasted_iota(jnp.int32, (n, n), 0) == lax.broadcasted_iota(jnp.int32, (n, n), 1))
    col = jnp.sum(jnp.where(eye, row, 0.0), axis=1, keepdims=True)
    return col * mat


def _gla(g_qk, g_vz, g_la, cms, consts, n_ctx_steps):
    b, t, _ = g_qk.shape
    n_steps = t // STEP
    blk = lambda d, n: _step_block(d, n, n_ctx_steps, n_steps)
    sel, val, pair = consts
    dsel = lambda a: pl.BlockSpec((None,) + a.shape[1:], lambda bi, d, n: (d, 0, 0))
    return pl.pallas_call(
        _gla_body,
        out_shape=jax.ShapeDtypeStruct((b, 2, t, GROUP_W), F32),
        grid=(b, 2, n_steps),
        in_specs=[pl.BlockSpec((None, STEP, 256), lambda bi, d, n: (bi, blk(d, n), 0)),
                  pl.BlockSpec((None, STEP, 256), lambda bi, d, n: (bi, blk(d, n), 0)),
                  pl.BlockSpec((None, STEP, 256), lambda bi, d, n: (bi, blk(d, n), 0)),
                  dsel(cms), dsel(sel), dsel(val), dsel(pair)],
        out_specs=pl.BlockSpec((None, None, STEP, GROUP_W), lambda bi, d, n: (bi, d, blk(d, n), 0)),
        scratch_shapes=[pltpu.VMEM((N_HEADS, N_HEADS * GLA_DK, HEAD_DIM), F32)],
        compiler_params=_params("parallel", "parallel", "arbitrary"),
        name="gla",
    )(g_qk, g_vz, g_la, cms, sel, val, pair)


def _gdn_prep_body(prev_ref, cur_ref, next_ref, w_ref, ind_ref, o_ref, ext_ref, *, n_ctx_tiles, n_tiles):
    i = pl.program_id(1)
    tm = cur_ref.shape[0]
    first = (i == 0) | (i == n_ctx_tiles)
    last = (i == n_ctx_tiles - 1) | (i == n_tiles - 1)
    ext_ref[0:8, :] = jnp.where(first, 0.0, prev_ref[...])
    ext_ref[8:8 + tm, :] = cur_ref[...]
    ext_ref[8 + tm:16 + tm, :] = jnp.where(last, 0.0, next_ref[...])
    half = GDN_CONV // 2
    y = jnp.zeros(cur_ref.shape, F32)
    for j in range(GDN_CONV):
        y = y + ext_ref[8 - half + j:8 - half + j + tm, :] * w_ref[j:j + 1, :]
    y = jax.nn.silu(y)
    ind = ind_ref[...]
    q, k = y[:, :GROUP_W], y[:, GROUP_W:2 * GROUP_W]
    qn = q * lax.rsqrt(_mm_sel_r(q * q, ind) + EPS) * HEAD_DIM ** -0.5
    kn = k * lax.rsqrt(_mm_sel_r(k * k, ind) + EPS)
    o_ref[:, :GROUP_W] = qn
    o_ref[:, GROUP_W:2 * GROUP_W] = kn
    o_ref[:, 2 * GROUP_W:] = y[:, 2 * GROUP_W:]


def _gdn_prep(d_qkv, conv_w, ind64, n_ctx_tiles):
    b, t, w = d_qkv.shape
    tm = TOKEN_TILE
    n_tiles = t // tm
    r = tm // 8
    n8 = t // 8
    return pl.pallas_call(
        functools.partial(_gdn_prep_body, n_ctx_tiles=n_ctx_tiles, n_tiles=n_tiles),
        out_shape=jax.ShapeDtypeStruct((b, t, w), F32),
        grid=(b, n_tiles),
        in_specs=[pl.BlockSpec((None, 8, w), lambda bi, i: (bi, jnp.maximum(i * r - 1, 0), 0)),
                  pl.BlockSpec((None, tm, w), lambda bi, i: (bi, i, 0)),
                  pl.BlockSpec((None, 8, w), lambda bi, i: (bi, jnp.minimum((i + 1) * r, n8 - 1), 0)),
                  pl.BlockSpec(conv_w.shape, lambda bi, i: (0, 0)),
                  pl.BlockSpec(ind64.shape, lambda bi, i: (0, 0))],
        out_specs=pl.BlockSpec((None, tm, w), lambda bi, i: (bi, i, 0)),
        scratch_shapes=[pltpu.VMEM((tm + 16, w), F32)],
        compiler_params=_params("parallel", "parallel"),
        name="gdn_prep",
    )(d_qkv, d_qkv, d_qkv, conv_w, ind64)


def _wide_consts():
    cm = _cum_mats_np()
    eye = np.eye(CHUNK, dtype=np.float32)
    pair = _gla_level_mats_np()[2].reshape(2, 7, CHUNK, CHUNK)[:, :6]
    tile = lambda a: np.concatenate([a] * N_HEADS, axis=-1)
    hb = np.arange(GROUP_W) // HEAD_DIM
    bdm = (hb[:, None] == hb[None, :]).astype(np.float32)
    e_lo = np.zeros((2, 128, GROUP_W), np.float32)
    e_hi = np.zeros((2, 128, GROUP_W), np.float32)
    for d in range(2):
        for h in range(N_HEADS):
            e_lo[d, d * 4 + h, h * HEAD_DIM:(h + 1) * HEAD_DIM] = 1.0
            e_hi[d, 8 + d * 4 + h, h * HEAD_DIM:(h + 1) * HEAD_DIM] = 1.0
    return dict(cm=jnp.asarray(cm), cmw=jnp.asarray(tile(cm)), strictw=jnp.asarray(tile(cm - eye)),
                eyew=jnp.asarray(tile(eye)), pairw=jnp.asarray(tile(pair).reshape(2, 6 * CHUNK, GROUP_W)),
                bdm=jnp.asarray(bdm, MXU_DT), e_lo=jnp.asarray(e_lo), e_hi=jnp.asarray(e_hi))


def _bd_rows(x, bdm):
    xb = x.astype(MXU_DT)
    return jnp.concatenate([xb] * N_HEADS, axis=0) * bdm


def _each(f, *lists):
    return [f(*a) for a in zip(*lists)]


def _gdn_body(qkv_f, qkv_b, g_f, g_b, gT_f, gT_b, cm_ref, cmw_ref, strictw_ref, eyew_ref, pairw_ref, bdm_ref,
              elo_ref, ehi_ref, o_f, o_b, s_ref):
    @pl.when(pl.program_id(0) == 0)
    def _():
        s_ref[...] = jnp.zeros_like(s_ref)

    nb = qkv_f.shape[0]
    streams = [(b, d) for b in range(nb) for d in range(2)]
    dirs = [d for _, d in streams]
    bdm = bdm_ref[...]
    bdm2 = jnp.concatenate([bdm, bdm], axis=1)
    eyew = eyew_ref[...]
    cm = [cm_ref[d] for d in dirs]
    cmw = [cmw_ref[d] for d in dirs]
    strictw = [strictw_ref[d] for d in dirs]
    pairw = [[pairw_ref[d, lv * CHUNK:(lv + 1) * CHUNK, :] for d in dirs] for lv in range(6)]
    elo = [elo_ref[d] for d in dirs]
    ehi = [ehi_ref[d] for d in dirs]
    for j in range(2):
        offs = [(j if d == 0 else 1 - j) * CHUNK for d in dirs]
        src = [(qkv_f, g_f, gT_f) if d == 0 else (qkv_b, g_b, gT_b) for d in dirs]
        qkv = [r[0][b, o:o + CHUNK, :] for (b, _), r, o in zip(streams, src, offs)]
        g = [r[1][b, o:o + CHUNK, :] for (b, _), r, o in zip(streams, src, offs)]
        gt = [r[2][b, :, o:o + CHUNK] for (b, _), r, o in zip(streams, src, offs)]
        gc = _each(_mm_sel, cm, g)
        gr = _each(_mm_sel_nt, gt, cm)
        beta = _each(_mm_sel_r, g, elo)
        gw = _each(_mm_sel_r, gc, ehi)
        grow = [jnp.concatenate([x[8 + 4 * d + h:9 + 4 * d + h, :] for h in range(N_HEADS)], axis=1)
                for x, d in zip(gr, dirs)]
        decay = [jnp.exp(jnp.where(m > 0, a - r, -jnp.inf)) for m, a, r in zip(cmw, gw, grow)]
        q = [x[:, :GROUP_W] for x in qkv]
        k = [x[:, GROUP_W:2 * GROUP_W] for x in qkv]
        v = [x[:, 2 * GROUP_W:] for x in qkv]
        kb = _each(lambda a, b_: a * b_, k, beta)
        bdk = [jnp.concatenate([x.T.astype(MXU_DT)] * N_HEADS, axis=1) * bdm for x in k]
        gq = _each(lambda a, b_, w: _mm(jnp.concatenate([a, b_], axis=0), w), kb, q, bdk)
        lower = [s * x[:CHUNK] * dc for s, x, dc in zip(strictw, gq, decay)]
        attn = [x[CHUNK:] * dc for x, dc in zip(gq, decay)]
        tw = [eyew - lo * p for lo, p in zip(lower, pairw[5])]
        for lv in range(4, -1, -1):
            y = [_mm(lo * p, _bd_rows(t, bdm)) for lo, p, t in zip(lower, pairw[lv], tw)]
            tw = [t - _mm(t, _bd_rows(yy, bdm)) for t, yy in zip(tw, y)]
        rhs = [jnp.concatenate([vv * b_, kk * jnp.exp(a)], axis=1).astype(MXU_DT)
               for vv, b_, kk, a in zip(v, beta, kb, gw)]
        uw = [_mm(t, jnp.concatenate([r] * N_HEADS, axis=0) * bdm2) for t, r in zip(tw, rhs)]
        st = [s_ref[i] for i in range(len(streams))]
        v_new = [x[:, :GROUP_W] - _mm(x[:, GROUP_W:], s) for x, s in zip(uw, st)]
        o = [_mm(jnp.concatenate([qq * jnp.exp(a), at], axis=1),
                 jnp.concatenate([s.astype(MXU_DT), _bd_rows(vn, bdm)], axis=0))
             for qq, a, at, s, vn in zip(q, gw, attn, st, v_new)]
        last = [CHUNK - 1 if d == 0 else 0 for d in dirs]
        g_last = [a[r:r + 1, :] for a, r in zip(gw, last)]
        for i, ((b, d), off) in enumerate(zip(streams, offs)):
            (o_f if d == 0 else o_b)[b, off:off + CHUNK, :] = o[i]
            upd = _mm_tn(k[i] * jnp.exp(g_last[i] - gw[i]), v_new[i])
            s_ref[i] = jnp.exp(g_last[i]) * st[i] + bdm.astype(F32) * upd


def _gdn(d_qkv, d_g, d_gT, wc, n_ctx_steps):
    b, t, _ = d_qkv.shape
    n_steps = t // STEP
    fwd = lambda n: n
    bwd = lambda n: _step_block(1, n, n_ctx_steps, n_steps)
    tok = lambda w, f: pl.BlockSpec((b, STEP, w), lambda n: (0, f(n), 0))
    tokT = lambda f: pl.BlockSpec((b, 128, STEP), lambda n: (0, 0, f(n)))
    consts = [wc[k] for k in ("cm", "cmw", "strictw", "eyew", "pairw", "bdm", "e_lo", "e_hi")]
    full = lambda a: pl.BlockSpec(a.shape, lambda n: (0,) * a.ndim)
    out = jax.ShapeDtypeStruct((b, t, GROUP_W), F32)
    return pl.pallas_call(
        _gdn_body,
        out_shape=[out, out],
        grid=(n_steps,),
        in_specs=[tok(3 * GROUP_W, fwd), tok(3 * GROUP_W, bwd), tok(128, fwd), tok(128, bwd), tokT(fwd), tokT(bwd)]
                 + [full(a) for a in consts],
        out_specs=[tok(GROUP_W, fwd), tok(GROUP_W, bwd)],
        scratch_shapes=[pltpu.VMEM((2 * b, GROUP_W, GROUP_W), F32)],
        compiler_params=_params("arbitrary"),
        name="gdn",
    )(d_qkv, d_qkv, d_g, d_g, d_gT, d_gT, *consts)


def _rope_tables(n_ctx, n_lat):
    rows = n_lat // GRID_W
    row = np.repeat(np.arange(rows), GRID_W).astype(np.float32)
    col = np.tile(np.arange(GRID_W), rows).astype(np.float32)
    half = DA_QK // 2
    inv = np.power(ROPE_BASE, -np.arange(0, half, 2, dtype=np.float32) / half).astype(np.float32)

    def tab(p):
        ang = p[:, None] * inv
        ang = np.concatenate([ang, ang], axis=-1)
        return np.cos(ang), np.sin(ang)

    cr, sr = tab(row)
    cc, sc = tab(col)
    cos = np.concatenate([cr, cc], -1)
    sin = np.concatenate([sr, sc], -1)
    cos = np.concatenate([np.ones((n_ctx, DA_QK), np.float32), cos], 0)
    sin = np.concatenate([np.zeros((n_ctx, DA_QK), np.float32), sin], 0)
    reps = GROUP_W // DA_QK
    cos, sin = np.tile(cos, (1, reps)), np.tile(sin, (1, reps))
    first = (np.arange(GROUP_W) % 16) < 8
    sin_a = np.where(first, -sin, 0.0)
    sin_b = np.where(first, 0.0, sin)
    return jnp.asarray(cos, F32), jnp.asarray(sin_a, F32), jnp.asarray(sin_b, F32)


def _diff_prep_body(qk_ref, v_ref, cos_ref, sa_ref, sb_ref, nw_ref, ind_ref, q_out, kt_out, v_out):
    ind = ind_ref[...]
    cos, sa, sb = cos_ref[...], sa_ref[...], sb_ref[...]

    def norm_rope(x, w):
        xn = x * lax.rsqrt(_mm_sel_r(x * x, ind) * (1.0 / DA_QK) + EPS) * w
        return xn * cos + pltpu.roll(xn, GROUP_W - 8, 1) * sa + pltpu.roll(xn, 8, 1) * sb

    qk = qk_ref[...]
    q = norm_rope(qk[:, :GROUP_W], nw_ref[0:1, :]) * (DA_QK ** -0.5 * math.log2(math.e))
    k = norm_rope(qk[:, GROUP_W:], nw_ref[1:2, :])
    for g in range(GROUP_W // DA_QK):
        q_out[g] = q[:, g * DA_QK:(g + 1) * DA_QK].astype(q_out.dtype)
    kt_out[...] = k.T.astype(kt_out.dtype)
    v = v_ref[...]
    ones = jnp.ones((v.shape[0], HEAD_DIM), F32)
    for h in range(N_HEADS):
        v_out[h] = jnp.concatenate([v[:, h * HEAD_DIM:(h + 1) * HEAD_DIM], ones], axis=1).astype(v_out.dtype)


def _diff_prep(a_qk, a_vz, tables, nw, ind32):
    b, t, _ = a_qk.shape
    tm = TOKEN_TILE
    cos, sa, sb = tables
    tab = pl.BlockSpec((tm, GROUP_W), lambda bi, i: (i, 0))
    ng = GROUP_W // DA_QK
    return pl.pallas_call(
        _diff_prep_body,
        out_shape=[jax.ShapeDtypeStruct((b, ng, t, DA_QK), MXU_DT),
                   jax.ShapeDtypeStruct((b, GROUP_W, t), MXU_DT),
                   jax.ShapeDtypeStruct((b, N_HEADS, t, 128), MXU_DT)],
        grid=(b, t // tm),
        in_specs=[pl.BlockSpec((None, tm, 2 * GROUP_W), lambda bi, i: (bi, i, 0)),
                  pl.BlockSpec((None, tm, GROUP_W), lambda bi, i: (bi, i, 0)),
                  tab, tab, tab,
                  pl.BlockSpec(nw.shape, lambda bi, i: (0, 0)),
                  pl.BlockSpec(ind32.shape, lambda bi, i: (0, 0))],
        out_specs=[pl.BlockSpec((None, ng, tm, DA_QK), lambda bi, i: (bi, 0, i, 0)),
                   pl.BlockSpec((None, GROUP_W, tm), lambda bi, i: (bi, 0, i)),
                   pl.BlockSpec((None, N_HEADS, tm, 128), lambda bi, i: (bi, 0, i, 0))],
        compiler_params=_params("parallel", "parallel"),
        name="diff_prep",
    )(a_qk, a_vz, cos, sa, sb, nw, ind32)


def _diff_attn_body(lam_ref, q_ref, kt_ref, vp_ref, vc_ref, o_ref, acc_ref, m_ref, e_ref, *, lam_init):
    j = pl.program_id(3)

    @pl.when(j == 0)
    def _():
        acc_ref[...] = jnp.zeros_like(acc_ref)
        m_ref[...] = jnp.full_like(m_ref, -jnp.inf)
        e_ref[...] = jnp.zeros_like(e_ref)

    vp = vp_ref[...]
    for p in range(2):
        s = jnp.dot(q_ref[p], kt_ref[p * DA_QK:(p + 1) * DA_QK, :], preferred_element_type=F32)
        pv = jnp.dot(e_ref[p], vp, preferred_element_type=F32)
        m_old = m_ref[p][:, 0:1]
        m_new = jnp.maximum(m_old, jnp.max(s, axis=1, keepdims=True))
        e_ref[p] = jnp.exp2(s - m_new).astype(e_ref.dtype)
        acc_ref[p] = jnp.exp2(m_old - m_new) * (acc_ref[p] + pv)
        m_ref[p] = jnp.broadcast_to(m_new, m_ref.shape[1:])

    @pl.when(j == pl.num_programs(3) - 1)
    def _():
        lam = lam_ref[...]
        lam_full = (jnp.exp(jnp.sum(lam[0:1] * lam[1:2], keepdims=True))
                    - jnp.exp(jnp.sum(lam[2:3] * lam[3:4], keepdims=True)) + lam_init)
        vc = vc_ref[...]
        a0 = acc_ref[0] + jnp.dot(e_ref[0], vc, preferred_element_type=F32)
        a1 = acc_ref[1] + jnp.dot(e_ref[1], vc, preferred_element_type=F32)
        o_ref[...] = (a0[:, :HEAD_DIM] / a0[:, HEAD_DIM:HEAD_DIM + 1]
                      - lam_full * (a1[:, :HEAD_DIM] / a1[:, HEAD_DIM:HEAD_DIM + 1]))


def _diff_attn(lam, q, kt, v1, lam_init, n_keys, tq, tk):
    b, _, n_q, _ = q.shape
    return pl.pallas_call(
        functools.partial(_diff_attn_body, lam_init=lam_init),
        out_shape=jax.ShapeDtypeStruct((b, N_HEADS, n_q, HEAD_DIM), F32),
        grid=(b, N_HEADS, n_q // tq, n_keys // tk),
        in_specs=[pl.BlockSpec(lam.shape, lambda bi, h, i, j: (0, 0)),
                  pl.BlockSpec((None, 2, tq, DA_QK), lambda bi, h, i, j: (bi, h, i, 0)),
                  pl.BlockSpec((None, 2 * DA_QK, tk), lambda bi, h, i, j: (bi, h, j)),
                  pl.BlockSpec((None, None, tk, 128), lambda bi, h, i, j: (bi, h, jnp.maximum(j - 1, 0), 0)),
                  pl.BlockSpec((None, None, tk, 128), lambda bi, h, i, j: (bi, h, j, 0))],
        out_specs=pl.BlockSpec((None, None, tq, HEAD_DIM), lambda bi, h, i, j: (bi, h, i, 0)),
        scratch_shapes=[pltpu.VMEM((2, tq, 128), F32), pltpu.VMEM((2, tq, 128), F32),
                        pltpu.VMEM((2, tq, tk), MXU_DT)],
        compiler_params=_params("parallel", "parallel", "parallel", "arbitrary"),
        name="diff_attn",
    )(lam, q, kt, v1, v1)


def _finish_body(x_ref, mh_ref, moz_ref, ao_ref, az_ref, gh_ref, gz_ref, dhf_ref, dhb_ref, dz_ref, nw_ref, ind_ref,
                 wout_ref, gt_ref, o_ref, *, lam_init):
    ind = ind_ref[...]

    def head_norm(hh, w):
        return hh * lax.rsqrt(_mm_sel_r(hh * hh, ind) * (1.0 / HEAD_DIM) + EPS) * w

    moz = moz_ref[...]
    ym = (head_norm(mh_ref[0] + mh_ref[1], nw_ref[0:1, :]) * jax.nn.sigmoid(moz[:, :GROUP_W])
          * jax.nn.silu(moz[:, GROUP_W:]))
    ao = jnp.concatenate([ao_ref[h] for h in range(N_HEADS)], axis=1)
    ya = head_norm(ao, nw_ref[1:2, :]) * (1.0 - lam_init) * jax.nn.silu(az_ref[...])
    yg = head_norm(gh_ref[0] + gh_ref[1], nw_ref[2:3, :]) * jax.nn.silu(gz_ref[...])
    yd = head_norm(dhf_ref[...] + dhb_ref[...], nw_ref[3:4, :]) * jax.nn.silu(dz_ref[...])
    y = jnp.concatenate([ym, ya, yg, yd], axis=1).astype(MXU_DT)
    o_ref[...] = x_ref[...] + gt_ref[...] * jnp.dot(y, wout_ref[...], preferred_element_type=F32)


def _finish(xs, mh, m_oz, ao, a_vz, gh, g_vz, dh_f, dh_b, d_z, nw, ind64, wout, gtsel, lam_init, n_ctx_tiles):
    b, t, d = xs.shape
    tm = TOKEN_TILE
    tok = lambda w, c=0: pl.BlockSpec((None, tm, w), lambda bi, i: (bi, i, c))
    two = pl.BlockSpec((None, 2, tm, GROUP_W), lambda bi, i: (bi, 0, i, 0))
    full = lambda a: pl.BlockSpec(a.shape, lambda bi, i: (0,) * a.ndim)
    return pl.pallas_call(
        functools.partial(_finish_body, lam_init=lam_init),
        out_shape=jax.ShapeDtypeStruct((b, t, d), F32),
        grid=(b, t // tm),
        in_specs=[tok(d), two, tok(2 * GROUP_W),
                  pl.BlockSpec((None, N_HEADS, tm, HEAD_DIM), lambda bi, i: (bi, 0, i, 0)),
                  tok(GROUP_W, 1), two, tok(GROUP_W, 1), tok(GROUP_W), tok(GROUP_W), tok(GROUP_W),
                  full(nw), full(ind64), full(wout),
                  pl.BlockSpec((None, None, 1, d), lambda bi, i: (bi, (i >= n_ctx_tiles).astype(jnp.int32), 0, 0))],
        out_specs=tok(d),
        compiler_params=_params("parallel", "parallel"),
        name="finish",
    )(xs, mh, m_oz, ao, a_vz, gh, g_vz, dh_f, dh_b, d_z, nw, ind64, wout, gtsel)


def kernel(x, c, ctx, c_ctx, norm_w, w_mod, b_mod, w_in, w_out, mlstm_b_i, mlstm_b_f, mlstm_norm, diff_q_norm,
           diff_k_norm, diff_lambda, diff_norm, gla_w_up, gla_b, gla_norm, gdn_conv, gdn_a_log, gdn_dt_bias,
           gdn_norm):
    bsz, n_lat, d = x.shape
    n_ctx = ctx.shape[1]
    depth = w_in.shape[0]
    t = n_ctx + n_lat
    assert n_ctx % TOKEN_TILE == 0 and n_lat % TOKEN_TILE == 0 and n_lat % GRID_W == 0
    assert n_lat % ATTN_TQ == 0 and t % ATTN_TK == 0 and n_ctx % STEP == 0
    n_ctx_tiles = n_ctx // TOKEN_TILE
    n_ctx_steps = n_ctx // STEP

    cc = jnp.concatenate([c, c_ctx[None, :], jnp.zeros((8 - (bsz + 1) % 8, d), F32)], axis=0)
    mod = _modulation(cc, w_mod, b_mod)

    cms = _cum_mats()
    gla_consts = _gla_level_mats()
    wide = _wide_consts()
    ind64 = _group_ones(GROUP_W, HEAD_DIM)
    ind32 = _group_ones(GROUP_W, DA_QK)
    tables = _rope_tables(n_ctx, n_lat)
    pad = lambda a, n: jnp.concatenate([a, jnp.zeros(a.shape[:-1] + (n - a.shape[-1],), a.dtype)], axis=-1)

    xs = jnp.concatenate([ctx, x], axis=1)
    for l in range(depth):
        lam_init = 0.8 - 0.6 * math.exp(-0.3 * l)
        sh, sc, gt = jnp.split(mod[l], 3, axis=-1)
        lat = jnp.stack([sh[:bsz], sc[:bsz]], axis=1)
        cx = jnp.broadcast_to(jnp.stack([sh[bsz], sc[bsz]], axis=0)[None], (bsz, 2, d))
        modsel = jnp.stack([cx, lat], axis=1)
        gtsel = jnp.stack([jnp.broadcast_to(gt[bsz][None], (bsz, d)), gt[:bsz]], axis=1)[:, :, None, :]
        wp, wg = _permute_w_in(w_in[l])
        pm = pad(jnp.concatenate([mlstm_b_i[l].reshape(1, 8), mlstm_b_f[l].reshape(1, 8)], axis=1), 128)
        pm = jnp.concatenate([pm, jnp.zeros((7, 128), F32)], axis=0)
        z8 = jnp.zeros((1, 8), F32)
        pd = jnp.concatenate([pad(jnp.concatenate([z8, gdn_a_log[l].reshape(1, 8)], axis=1), 128),
                              pad(jnp.concatenate([z8, gdn_dt_bias[l].reshape(1, 8)], axis=1), 128),
                              jnp.zeros((6, 128), F32)], axis=0)
        wup = jnp.zeros((128, 256), F32)
        wup = wup.at[0:GLA_RANK, 0:128].set(gla_w_up[l, 0]).at[GLA_RANK:2 * GLA_RANK, 128:256].set(gla_w_up[l, 1])
        bgk = gla_b[l].reshape(1, 256)

        (m_qkv, m_oz, m_g, m_gT, a_qk, a_vz, g_qk, g_vz, g_la, d_qkv, d_z, d_g, d_gT) = _inproj(
            xs, norm_w[l][None, :], modsel, wp, wg, pm, pd, wup.astype(MXU_DT), bgk, n_ctx_tiles)

        mh = _mlstm(m_qkv, m_g, m_gT, cms, n_ctx_steps)

        qkn = jnp.stack([jnp.tile(diff_q_norm[l], GROUP_W // DA_QK), jnp.tile(diff_k_norm[l], GROUP_W // DA_QK)])
        aq, akt, av1 = _diff_prep(a_qk, a_vz, tables, qkn, ind32)
        lam = pad(diff_lambda[l], 128)
        ao_ctx = _diff_attn(lam, aq[:, :, :n_ctx], akt, av1, lam_init, n_ctx, n_ctx, n_ctx)
        ao_lat = _diff_attn(lam, aq[:, :, n_ctx:], akt, av1, lam_init, t, ATTN_TQ, ATTN_TK)
        ao = jnp.concatenate([ao_ctx, ao_lat], axis=2)

        gh = _gla(g_qk, g_vz, g_la, cms, gla_consts, n_ctx_steps)

        dq = _gdn_prep(d_qkv, gdn_conv[l], ind64, n_ctx_tiles)
        dh_f, dh_b = _gdn(dq, d_g, d_gT, wide, n_ctx_steps)

        nws = jnp.stack([mlstm_norm[l], diff_norm[l], gla_norm[l], gdn_norm[l]])
        xs = _finish(xs, mh, m_oz, ao, a_vz, gh, g_vz, dh_f, dh_b, d_z, nws, ind64, w_out[l].astype(MXU_DT),
                     gtsel, lam_init, n_ctx_tiles)
    return xs[:, n_ctx:]
```

```python
import functools
import math

import jax
import jax.numpy as jnp
import numpy as np
from jax import lax
from jax.experimental import pallas as pl
from jax.experimental.pallas import tpu as pltpu

F32 = jnp.float32
MXU_DT = jnp.bfloat16
EPS = 1e-6
N_HEADS = 4
HEAD_DIM = 64
GROUP_W = N_HEADS * HEAD_DIM
CHUNK = 64
STEP = 2 * CHUNK
DA_QK = 32
GLA_DK = 32
GLA_RANK = 16
GLA_TAU = 16.0
GDN_CONV = 5
GRID_W = 64
ROPE_BASE = 10000.0
TOKEN_TILE = 256
ATTN_TQ = 512
ATTN_TK = 768
VMEM_LIMIT = 48 * 1024 * 1024
NEG_BIG = -1e30


def _mm(a, b):
    return jnp.dot(a.astype(MXU_DT), b.astype(MXU_DT), preferred_element_type=F32)


def _mm_nt(a, b):
    return lax.dot_general(a.astype(MXU_DT), b.astype(MXU_DT), (((1,), (1,)), ((), ())),
                           preferred_element_type=F32)


def _mm_tn(a, b):
    return lax.dot_general(a.astype(MXU_DT), b.astype(MXU_DT), (((0,), (0,)), ((), ())),
                           preferred_element_type=F32)


def _split2(x):
    hi = x.astype(MXU_DT)
    lo = (x - hi.astype(F32)).astype(MXU_DT)
    return hi, lo


def _mm_sel(sel, x):
    hi, lo = _split2(x)
    s = sel.astype(MXU_DT)
    return (jnp.dot(s, hi, preferred_element_type=F32) + jnp.dot(s, lo, preferred_element_type=F32))


def _mm_sel_r(x, sel):
    hi, lo = _split2(x)
    s = sel.astype(MXU_DT)
    return (jnp.dot(hi, s, preferred_element_type=F32) + jnp.dot(lo, s, preferred_element_type=F32))


def _mm_sel_nt(x, sel):
    hi, lo = _split2(x)
    s = sel.astype(MXU_DT)
    dn = (((1,), (1,)), ((), ()))
    return (lax.dot_general(hi, s, dn, preferred_element_type=F32)
            + lax.dot_general(lo, s, dn, preferred_element_type=F32))


def _group_ones(width, group):
    g = np.arange(width) // group
    return jnp.asarray((g[:, None] == g[None, :]).astype(np.float32))


def _params(*sem):
    return pltpu.CompilerParams(dimension_semantics=sem, vmem_limit_bytes=VMEM_LIMIT)


def _mod_body(cc_ref, w_ref, b_ref, o_ref):
    o_ref[...] = _mm(jax.nn.silu(cc_ref[...]), w_ref[...]) + b_ref[...]


def _modulation(cc, w_mod, b_mod):
    depth, d, d3 = w_mod.shape
    tn = 1024
    return pl.pallas_call(
        _mod_body,
        out_shape=jax.ShapeDtypeStruct((depth, cc.shape[0], d3), F32),
        grid=(depth, d3 // tn),
        in_specs=[pl.BlockSpec(cc.shape, lambda l, j: (0, 0)),
                  pl.BlockSpec((None, d, tn), lambda l, j: (l, 0, j)),
                  pl.BlockSpec((None, 1, tn), lambda l, j: (l, 0, j))],
        out_specs=pl.BlockSpec((None, cc.shape[0], tn), lambda l, j: (l, 0, j)),
        compiler_params=_params("parallel", "parallel"),
        name="modulation",
    )(cc, w_mod, b_mod.reshape(depth, 1, d3))


_SEG = dict(m_qkv=(0, 768), m_oz=(768, 1280), m_g=(1280, 1408), a_qk=(1408, 1920), a_vz=(1920, 2432),
            g_qk=(2432, 2688), g_vz=(2688, 3200), g_r=(3200, 3328), d_qkv=(3328, 4096), d_z=(4096, 4352),
            d_g=(4352, 4480))
P_PAD = 4480


def _permute_w_in(w):
    d = w.shape[0]
    z = lambda n: jnp.zeros((d, n), w.dtype)
    m0, a0, g0, d0 = 0, 1296, 2320, 3120
    cols = [w[:, m0:m0 + 1280], w[:, m0 + 1280:m0 + 1296], z(112),
            w[:, a0:a0 + 512], w[:, a0 + 512:a0 + 1024],
            w[:, g0:g0 + 256], w[:, g0 + 256:g0 + 768], w[:, g0 + 768:g0 + 800], z(96),
            w[:, d0:d0 + 768], w[:, d0 + 768:d0 + 1024], w[:, d0 + 1024:d0 + 1040], z(112)]
    wp = jnp.concatenate(cols, axis=1)
    wg = jnp.concatenate([wp[:, 1280:1408], wp[:, 4352:4480]], axis=1).T
    return wp.astype(MXU_DT), wg.astype(MXU_DT)


def _gate_fns(xm, xd, pm, pd, lane):
    gm = jnp.where(lane < 8, xm + pm[0], jax.nn.log_sigmoid(xm + pm[0]))
    gd = jnp.where(lane < 8, jax.nn.sigmoid(xd), -jnp.exp(pd[0]) * jax.nn.softplus(xd + pd[1]))
    return gm, gd


def _inproj_body(x_ref, nw_ref, mod_ref, wp_ref, wg_ref, pm_ref, pd_ref, pmT_ref, pdT_ref, wup_ref, bgk_ref,
                 m_qkv, m_oz, m_g, m_gT, a_qk, a_vz, g_qk, g_vz, g_la, d_qkv, d_z, d_g, d_gT):
    x = x_ref[...]
    h = x * lax.rsqrt(jnp.mean(x * x, axis=-1, keepdims=True) + EPS) * nw_ref[...]
    h = (h * (1.0 + mod_ref[1:2, :]) + mod_ref[0:1, :]).astype(MXU_DT)
    u = jnp.dot(h, wp_ref[...], preferred_element_type=F32)
    seg = lambda name: u[:, _SEG[name][0]:_SEG[name][1]]
    m_qkv[...] = seg("m_qkv")
    m_oz[...] = seg("m_oz")
    a_qk[...] = seg("a_qk")
    a_vz[...] = seg("a_vz")
    g_qk[...] = seg("g_qk")
    g_vz[...] = seg("g_vz")
    d_qkv[...] = seg("d_qkv")
    d_z[...] = seg("d_z")
    lane = lax.broadcasted_iota(jnp.int32, (1, 128), 1)
    gm, gd = _gate_fns(seg("m_g"), seg("d_g"), (pm_ref[0:1, :],), (pd_ref[0:1, :], pd_ref[1:2, :]), lane)
    m_g[...] = gm
    d_g[...] = gd
    ut = lax.dot_general(wg_ref[...], h, (((1,), (1,)), ((), ())), preferred_element_type=F32)
    row = lax.broadcasted_iota(jnp.int32, (128, 1), 0)
    gmt, gdt = _gate_fns(ut[0:128], ut[128:256], (pmT_ref[:, 0:1],), (pdT_ref[:, 0:1], pdT_ref[:, 1:2]), row)
    m_gT[...] = gmt
    d_gT[...] = gdt
    gk = _mm(seg("g_r"), wup_ref[...]) + bgk_ref[...]
    g_la[...] = jax.nn.log_sigmoid(gk) / GLA_TAU


def _inproj(xs, nw, modsel, wp, wg, pm, pd, wup, bgk, n_ctx_tiles):
    b, t, d = xs.shape
    tm = TOKEN_TILE
    tok = lambda w: pl.BlockSpec((None, tm, w), lambda bi, i: (bi, i, 0))
    tokT = pl.BlockSpec((None, 128, tm), lambda bi, i: (bi, 0, i))
    full = lambda a: pl.BlockSpec(a.shape, lambda bi, i: (0,) * a.ndim)
    widths = [768, 512, 128, None, 512, 512, 256, 512, 256, 768, 256, 128, None]
    out_shape, out_specs = [], []
    for w in widths:
        if w is None:
            out_shape.append(jax.ShapeDtypeStruct((b, 128, t), F32))
            out_specs.append(tokT)
        else:
            out_shape.append(jax.ShapeDtypeStruct((b, t, w), F32))
            out_specs.append(tok(w))
    pmT, pdT = pm.T, pd.T
    return pl.pallas_call(
        _inproj_body,
        out_shape=out_shape,
        grid=(b, t // tm),
        in_specs=[tok(d), full(nw),
                  pl.BlockSpec((None, None, 2, d), lambda bi, i: (bi, (i >= n_ctx_tiles).astype(jnp.int32), 0, 0)),
                  full(wp), full(wg), full(pm), full(pd), full(pmT), full(pdT), full(wup), full(bgk)],
        out_specs=out_specs,
        compiler_params=_params("parallel", "parallel"),
        name="inproj",
    )(xs, nw, modsel, wp, wg, pm, pd, pmT, pdT, wup, bgk)


def _cum_mats_np():
    i = np.arange(CHUNK)
    fwd = (i[None, :] <= i[:, None]).astype(np.float32)
    return np.stack([fwd, fwd.T])


def _step_block(d, n, n_ctx_steps, n_steps):
    bwd = jnp.where(n < n_ctx_steps, n_ctx_steps - 1 - n, n_steps - 1 - (n - n_ctx_steps))
    return jnp.where(d == 0, n, bwd)


def _head_rowmax(x):
    return jnp.concatenate(
        [jnp.broadcast_to(jnp.max(x[:, h * HEAD_DIM:(h + 1) * HEAD_DIM], axis=1, keepdims=True), (CHUNK, HEAD_DIM))
         for h in range(N_HEADS)], axis=1)


def _mlstm_body(qkv_f, qkv_b, g_f, g_b, gT_f, gT_b, cm_ref, cmw_ref, bdm_ref, elo_ref, ehi_ref, ind_ref,
                o_f, o_b, cn_ref, m_ref):
    @pl.when(pl.program_id(0) == 0)
    def _():
        cn_ref[...] = jnp.zeros_like(cn_ref)
        m_ref[...] = jnp.zeros_like(m_ref)

    nb = qkv_f.shape[0]
    streams = [(b, d) for b in range(nb) for d in range(2)]
    dirs = [d for _, d in streams]
    bdm = bdm_ref[...]
    bdm2 = jnp.concatenate([bdm, bdm], axis=1).astype(F32)
    ind = ind_ref[...].astype(MXU_DT)
    ones = jnp.ones((CHUNK, GROUP_W), F32)
    dup = lambda a: jnp.concatenate([a, a], axis=1)
    cm = [cm_ref[d] for d in dirs]
    cmw = [cmw_ref[d] for d in dirs]
    elo = [elo_ref[d] for d in dirs]
    ehi = [ehi_ref[d] for d in dirs]
    for j in range(2):
        offs = [(j if d == 0 else 1 - j) * CHUNK for d in dirs]
        src = [(qkv_f, g_f, gT_f) if d == 0 else (qkv_b, g_b, gT_b) for d in dirs]
        qkv = [r[0][b, o:o + CHUNK, :] for (b, _), r, o in zip(streams, src, offs)]
        g = [r[1][b, o:o + CHUNK, :] for (b, _), r, o in zip(streams, src, offs)]
        gt = [r[2][b, :, o:o + CHUNK] for (b, _), r, o in zip(streams, src, offs)]
        bc = _each(_mm_sel, cm, g)
        br = _each(_mm_sel_nt, gt, cm)
        bw = _each(_mm_sel_r, bc, ehi)
        iw = _each(_mm_sel_r, g, elo)
        rw = [jnp.concatenate([a[4 * d + h:4 * d + h + 1, :] - c[8 + 4 * d + h:9 + 4 * d + h, :]
                               for h in range(N_HEADS)], axis=1) for a, c, d in zip(gt, br, dirs)]
        dmat = [jnp.where(m > 0, a + r, -jnp.inf) for m, a, r in zip(cmw, bw, rw)]
        m_prev = [m_ref[i] for i in range(len(streams))]
        inter = _each(lambda a, m: a + m, bw, m_prev)
        m_t = [jnp.maximum(a, _head_rowmax(x)) for a, x in zip(inter, dmat)]
        q = [x[:, :GROUP_W] for x in qkv]
        k = [x[:, GROUP_W:2 * GROUP_W] * HEAD_DIM ** -0.5 for x in qkv]
        v = [x[:, 2 * GROUP_W:] for x in qkv]
        bdk = [jnp.concatenate([x.T.astype(MXU_DT)] * N_HEADS, axis=1) * bdm for x in k]
        s = [_mm(qq, w) * jnp.exp(x - m) for qq, w, x, m in zip(q, bdk, dmat, m_t)]
        w_inter = [jnp.exp(a - m) for a, m in zip(inter, m_t)]
        cn = [cn_ref[i] for i in range(len(streams))]
        acc = [dup(wi) * _mm(qq, c) + _mm(ss, jnp.concatenate([_bd_rows(vv, bdm), ind], axis=1))
               for wi, qq, c, ss, vv in zip(w_inter, q, cn, s, v)]
        hout = [a[:, :GROUP_W] / jnp.maximum(jnp.abs(a[:, GROUP_W:]), jnp.exp(-m)) for a, m in zip(acc, m_t)]
        last = [CHUNK - 1 if d == 0 else 0 for d in dirs]
        b_last = [a[r:r + 1, :] for a, r in zip(bw, last)]
        ds = [bl - a + i_ for bl, a, i_ in zip(b_last, bw, iw)]
        m_new = [jnp.maximum(bl + m, jnp.max(x, axis=0, keepdims=True)) for bl, m, x in zip(b_last, m_prev, ds)]
        for i, ((b, d), off) in enumerate(zip(streams, offs)):
            (o_f if d == 0 else o_b)[b, off:off + CHUNK, :] = hout[i]
            w_s = jnp.exp(ds[i] - m_new[i])
            w_c = jnp.exp(b_last[i] + m_prev[i] - m_new[i])
            upd = _mm_tn(k[i] * w_s, jnp.concatenate([v[i], ones], axis=1))
            cn_ref[i] = dup(w_c) * cn[i] + bdm2 * upd
            m_ref[i] = m_new[i]


def _scan_specs(b, n_ctx_steps, n_steps):
    fwd = lambda n: n
    bwd = lambda n: _step_block(1, n, n_ctx_steps, n_steps)
    tok = lambda w, f, c=0: pl.BlockSpec((b, STEP, w), lambda n: (0, f(n), c))
    tokT = lambda f: pl.BlockSpec((b, 128, STEP), lambda n: (0, 0, f(n)))
    full = lambda a: pl.BlockSpec(a.shape, lambda n: (0,) * a.ndim)
    return fwd, bwd, tok, tokT, full


def _mlstm(m_qkv, m_g, m_gT, wc, ind64, n_ctx_steps):
    b, t, _ = m_qkv.shape
    n_steps = t // STEP
    fwd, bwd, tok, tokT, full = _scan_specs(b, n_ctx_steps, n_steps)
    consts = [wc[k] for k in ("cm", "cmw", "bdm", "e_lo", "e_hi")] + [ind64]
    out = jax.ShapeDtypeStruct((b, t, GROUP_W), F32)
    return pl.pallas_call(
        _mlstm_body,
        out_shape=[out, out],
        grid=(n_steps,),
        in_specs=[tok(3 * GROUP_W, fwd), tok(3 * GROUP_W, bwd), tok(128, fwd), tok(128, bwd), tokT(fwd), tokT(bwd)]
                 + [full(a) for a in consts],
        out_specs=[tok(GROUP_W, fwd), tok(GROUP_W, bwd)],
        scratch_shapes=[pltpu.VMEM((2 * b, GROUP_W, 2 * GROUP_W), F32), pltpu.VMEM((2 * b, 1, GROUP_W), F32)],
        compiler_params=_params("arbitrary"),
        name="mlstm",
    )(m_qkv, m_qkv, m_g, m_g, m_gT, m_gT, *consts)


def _gla_level_mats_np():
    L = CHUNK
    t = np.arange(L)
    sels, valids, pairs = [], [], []
    for rev in (False, True):
        pos_all = t if not rev else L - 1 - t
        eqs, eks, vq, vk, pm = [], [], [], [], []
        for j in range(6):
            w = 32 >> j
            par = pos_all // (2 * w)
            pos = pos_all % (2 * w)
            same = par[:, None] == par[None, :]
            late = pos >= w
            eq = same & late[:, None] & late[None, :] & (pos[None, :] <= pos[:, None])
            ek = same & (~late)[:, None] & (~late)[None, :] & (pos[None, :] > pos[:, None])
            eqs.append(eq)
            eks.append(ek)
            vq.append(late)
            vk.append(~late)
            pm.append(same & late[:, None] & (~late)[None, :])
        pm.append(np.eye(L, dtype=bool))
        sels.append(np.concatenate(eqs + eks, axis=0))
        valids.append(np.concatenate(vq + vk)[:, None])
        pairs.append(np.concatenate(pm, axis=0))
    f = lambda a: np.stack(a).astype(np.float32)
    return f(sels), f(valids), f(pairs)


def _gla_consts():
    sel, val, pair = _gla_level_mats_np()
    nq = 6 * CHUNK
    tile = lambda a: np.concatenate([a] * N_HEADS, axis=-1)
    selq, selk = sel[:, :nq], sel[:, nq:]
    selkt = np.stack([np.concatenate([selk[d, lv * CHUNK:(lv + 1) * CHUNK].T for lv in range(6)], axis=1)
                      for d in range(2)])
    valkt = np.stack([val[d, nq:, 0][None, :] for d in range(2)])
    rh = np.arange(N_HEADS * GLA_DK) // GLA_DK
    ch = np.arange(GROUP_W) // HEAD_DIM
    bdg = (rh[:, None] == ch[None, :]).astype(np.float32)
    return dict(selq=jnp.asarray(selq), valq=jnp.asarray(val[:, :nq]), selkt=jnp.asarray(selkt),
                valkt=jnp.asarray(valkt), pairw=jnp.asarray(tile(pair)), bdg=jnp.asarray(bdg, MXU_DT))


def _gla_body(qk_f, qk_b, v_f, v_b, la_f, la_b, cm_ref, selq_ref, valq_ref, selkt_ref, valkt_ref, pairw_ref,
              bdg_ref, bdm_ref, o_f, o_b, s_ref):
    @pl.when(pl.program_id(0) == 0)
    def _():
        s_ref[...] = jnp.zeros_like(s_ref)

    nb = qk_f.shape[0]
    streams = [(b, d) for b in range(nb) for d in range(2)]
    dirs = [d for _, d in streams]
    nk = N_HEADS * GLA_DK
    bdg = bdg_ref[...]
    bdm = bdm_ref[...]
    bdk = lambda xt: jnp.concatenate([xt.astype(MXU_DT)] * N_HEADS, axis=1) * bdg
    cm = [cm_ref[d] for d in dirs]
    selq = [selq_ref[d] for d in dirs]
    valq = [valq_ref[d] for d in dirs]
    selkt = [selkt_ref[d] for d in dirs]
    valkt = [valkt_ref[d] for d in dirs]
    pairw = [[pairw_ref[d, lv * CHUNK:(lv + 1) * CHUNK, :] for d in dirs] for lv in range(7)]
    for j in range(2):
        offs = [(j if d == 0 else 1 - j) * CHUNK for d in dirs]
        src = [(qk_f, v_f, la_f) if d == 0 else (qk_b, v_b, la_b) for d in dirs]
        qk = [r[0][b, o:o + CHUNK, :] for (b, _), r, o in zip(streams, src, offs)]
        v = [r[1][b, o:o + CHUNK, :] for (b, _), r, o in zip(streams, src, offs)]
        la = [r[2][b, o:o + CHUNK, d * nk:(d + 1) * nk] for (b, d), r, o in zip(streams, src, offs)]
        q = [x[:, :nk] * GLA_DK ** -0.5 for x in qk]
        k = [x[:, nk:] for x in qk]
        lat = [x.T for x in la]
        kt = [x.T for x in k]
        bcum = _each(_mm_sel, cm, la)
        exq = [jnp.exp(_mm_sel(s, x)) * vq for s, x, vq in zip(selq, la, valq)]
        exk = [jnp.exp(_mm_sel_r(x, s)) * vk for s, x, vk in zip(selkt, lat, valkt)]
        a = [p * _mm(qq, bdk(x)) for p, qq, x in zip(pairw[6], q, kt)]
        for lv in range(6):
            sl = slice(lv * CHUNK, (lv + 1) * CHUNK)
            a = [aa + p * _mm(qq * eq[sl], bdk(x * ek[:, sl]))
                 for aa, p, qq, eq, x, ek in zip(a, pairw[lv], q, exq, kt, exk)]
        st = [s_ref[i] for i in range(len(streams))]
        o = [_mm(jnp.concatenate([qq * jnp.exp(bc), aa], axis=1),
                 jnp.concatenate([s.astype(MXU_DT), _bd_rows(vv, bdm)], axis=0))
             for qq, bc, aa, s, vv in zip(q, bcum, a, st, v)]
        last = [CHUNK - 1 if d == 0 else 0 for d in dirs]
        b_last = [x[r:r + 1, :] for x, r in zip(bcum, last)]
        b_last_t = [jnp.sum(x, axis=1, keepdims=True) for x in lat]
        for i, ((b, d), off) in enumerate(zip(streams, offs)):
            (o_f if d == 0 else o_b)[b, off:off + CHUNK, :] = o[i]
            upd = _mm_tn(k[i] * jnp.exp(b_last[i] - bcum[i]), v[i])
            s_ref[i] = jnp.exp(b_last_t[i]) * st[i] + bdg.astype(F32) * upd


def _gla(g_qk, g_vz, g_la, wc, gc, n_ctx_steps):
    b, t, _ = g_qk.shape
    n_steps = t // STEP
    fwd, bwd, tok, tokT, full = _scan_specs(b, n_ctx_steps, n_steps)
    consts = [wc["cm"]] + [gc[k] for k in ("selq", "valq", "selkt", "valkt", "pairw", "bdg")] + [wc["bdm"]]
    out = jax.ShapeDtypeStruct((b, t, GROUP_W), F32)
    w = 2 * N_HEADS * GLA_DK
    return pl.pallas_call(
        _gla_body,
        out_shape=[out, out],
        grid=(n_steps,),
        in_specs=[tok(w, fwd), tok(w, bwd), tok(GROUP_W, fwd), tok(GROUP_W, bwd),
                  tok(w, fwd), tok(w, bwd)] + [full(a) for a in consts],
        out_specs=[tok(GROUP_W, fwd), tok(GROUP_W, bwd)],
        scratch_shapes=[pltpu.VMEM((2 * b, N_HEADS * GLA_DK, GROUP_W), F32)],
        compiler_params=_params("arbitrary"),
        name="gla",
    )(g_qk, g_qk, g_vz, g_vz, g_la, g_la, *consts)


def _gdn_prep_body(prev_ref, cur_ref, next_ref, w_ref, ind_ref, o_ref, ext_ref, *, n_ctx_tiles, n_tiles):
    i = pl.program_id(1)
    tm = cur_ref.shape[0]
    first = (i == 0) | (i == n_ctx_tiles)
    last = (i == n_ctx_tiles - 1) | (i == n_tiles - 1)
    ext_ref[0:8, :] = jnp.where(first, 0.0, prev_ref[...])
    ext_ref[8:8 + tm, :] = cur_ref[...]
    ext_ref[8 + tm:16 + tm, :] = jnp.where(last, 0.0, next_ref[...])
    half = GDN_CONV // 2
    y = jnp.zeros(cur_ref.shape, F32)
    for j in range(GDN_CONV):
        y = y + ext_ref[8 - half + j:8 - half + j + tm, :] * w_ref[j:j + 1, :]
    y = jax.nn.silu(y)
    ind = ind_ref[...]
    q, k = y[:, :GROUP_W], y[:, GROUP_W:2 * GROUP_W]
    qn = q * lax.rsqrt(_mm_sel_r(q * q, ind) + EPS) * HEAD_DIM ** -0.5
    kn = k * lax.rsqrt(_mm_sel_r(k * k, ind) + EPS)
    o_ref[:, :GROUP_W] = qn
    o_ref[:, GROUP_W:2 * GROUP_W] = kn
    o_ref[:, 2 * GROUP_W:] = y[:, 2 * GROUP_W:]


def _gdn_prep(d_qkv, conv_w, ind64, n_ctx_tiles):
    b, t, w = d_qkv.shape
    tm = TOKEN_TILE
    n_tiles = t // tm
    r = tm // 8
    n8 = t // 8
    return pl.pallas_call(
        functools.partial(_gdn_prep_body, n_ctx_tiles=n_ctx_tiles, n_tiles=n_tiles),
        out_shape=jax.ShapeDtypeStruct((b, t, w), F32),
        grid=(b, n_tiles),
        in_specs=[pl.BlockSpec((None, 8, w), lambda bi, i: (bi, jnp.maximum(i * r - 1, 0), 0)),
                  pl.BlockSpec((None, tm, w), lambda bi, i: (bi, i, 0)),
                  pl.BlockSpec((None, 8, w), lambda bi, i: (bi, jnp.minimum((i + 1) * r, n8 - 1), 0)),
                  pl.BlockSpec(conv_w.shape, lambda bi, i: (0, 0)),
                  pl.BlockSpec(ind64.shape, lambda bi, i: (0, 0))],
        out_specs=pl.BlockSpec((None, tm, w), lambda bi, i: (bi, i, 0)),
        scratch_shapes=[pltpu.VMEM((tm + 16, w), F32)],
        compiler_params=_params("parallel", "parallel"),
        name="gdn_prep",
    )(d_qkv, d_qkv, d_qkv, conv_w, ind64)


def _wide_consts():
    cm = _cum_mats_np()
    eye = np.eye(CHUNK, dtype=np.float32)
    pair = _gla_level_mats_np()[2].reshape(2, 7, CHUNK, CHUNK)[:, :6]
    tile = lambda a: np.concatenate([a] * N_HEADS, axis=-1)
    hb = np.arange(GROUP_W) // HEAD_DIM
    bdm = (hb[:, None] == hb[None, :]).astype(np.float32)
    e_lo = np.zeros((2, 128, GROUP_W), np.float32)
    e_hi = np.zeros((2, 128, GROUP_W), np.float32)
    for d in range(2):
        for h in range(N_HEADS):
            e_lo[d, d * 4 + h, h * HEAD_DIM:(h + 1) * HEAD_DIM] = 1.0
            e_hi[d, 8 + d * 4 + h, h * HEAD_DIM:(h + 1) * HEAD_DIM] = 1.0
    return dict(cm=jnp.asarray(cm), cmw=jnp.asarray(tile(cm)), strictw=jnp.asarray(tile(cm - eye)),
                eyew=jnp.asarray(tile(eye)), pairw=jnp.asarray(tile(pair).reshape(2, 6 * CHUNK, GROUP_W)),
                bdm=jnp.asarray(bdm, MXU_DT), e_lo=jnp.asarray(e_lo), e_hi=jnp.asarray(e_hi))


def _bd_rows(x, bdm):
    xb = x.astype(MXU_DT)
    return jnp.concatenate([xb] * N_HEADS, axis=0) * bdm


def _each(f, *lists):
    return [f(*a) for a in zip(*lists)]


def _gdn_body(qkv_f, qkv_b, g_f, g_b, gT_f, gT_b, cm_ref, cmw_ref, strictw_ref, eyew_ref, pairw_ref, bdm_ref,
              elo_ref, ehi_ref, o_f, o_b, s_ref):
    @pl.when(pl.program_id(0) == 0)
    def _():
        s_ref[...] = jnp.zeros_like(s_ref)

    nb = qkv_f.shape[0]
    streams = [(b, d) for b in range(nb) for d in range(2)]
    dirs = [d for _, d in streams]
    bdm = bdm_ref[...]
    bdm2 = jnp.concatenate([bdm, bdm], axis=1)
    eyew = eyew_ref[...]
    cm = [cm_ref[d] for d in dirs]
    cmw = [cmw_ref[d] for d in dirs]
    strictw = [strictw_ref[d] for d in dirs]
    pairw = [[pairw_ref[d, lv * CHUNK:(lv + 1) * CHUNK, :] for d in dirs] for lv in range(6)]
    elo = [elo_ref[d] for d in dirs]
    ehi = [ehi_ref[d] for d in dirs]
    for j in range(2):
        offs = [(j if d == 0 else 1 - j) * CHUNK for d in dirs]
        src = [(qkv_f, g_f, gT_f) if d == 0 else (qkv_b, g_b, gT_b) for d in dirs]
        qkv = [r[0][b, o:o + CHUNK, :] for (b, _), r, o in zip(streams, src, offs)]
        g = [r[1][b, o:o + CHUNK, :] for (b, _), r, o in zip(streams, src, offs)]
        gt = [r[2][b, :, o:o + CHUNK] for (b, _), r, o in zip(streams, src, offs)]
        gc = _each(_mm_sel, cm, g)
        gr = _each(_mm_sel_nt, gt, cm)
        beta = _each(_mm_sel_r, g, elo)
        gw = _each(_mm_sel_r, gc, ehi)
        grow = [jnp.concatenate([x[8 + 4 * d + h:9 + 4 * d + h, :] for h in range(N_HEADS)], axis=1)
                for x, d in zip(gr, dirs)]
        decay = [jnp.exp(jnp.where(m > 0, a - r, -jnp.inf)) for m, a, r in zip(cmw, gw, grow)]
        q = [x[:, :GROUP_W] for x in qkv]
        k = [x[:, GROUP_W:2 * GROUP_W] for x in qkv]
        v = [x[:, 2 * GROUP_W:] for x in qkv]
        kb = _each(lambda a, b_: a * b_, k, beta)
        bdk = [jnp.concatenate([x.T.astype(MXU_DT)] * N_HEADS, axis=1) * bdm for x in k]
        gq = _each(lambda a, b_, w: _mm(jnp.concatenate([a, b_], axis=0), w), kb, q, bdk)
        lower = [s * x[:CHUNK] * dc for s, x, dc in zip(strictw, gq, decay)]
        attn = [x[CHUNK:] * dc for x, dc in zip(gq, decay)]
        tw = [eyew - lo * p for lo, p in zip(lower, pairw[5])]
        for lv in range(4, -1, -1):
            y = [_mm(lo * p, _bd_rows(t, bdm)) for lo, p, t in zip(lower, pairw[lv], tw)]
            tw = [t - _mm(t, _bd_rows(yy, bdm)) for t, yy in zip(tw, y)]
        rhs = [jnp.concatenate([vv * b_, kk * jnp.exp(a)], axis=1).astype(MXU_DT)
               for vv, b_, kk, a in zip(v, beta, kb, gw)]
        uw = [_mm(t, jnp.concatenate([r] * N_HEADS, axis=0) * bdm2) for t, r in zip(tw, rhs)]
        st = [s_ref[i] for i in range(len(streams))]
        v_new = [x[:, :GROUP_W] - _mm(x[:, GROUP_W:], s) for x, s in zip(uw, st)]
        o = [_mm(jnp.concatenate([qq * jnp.exp(a), at], axis=1),
                 jnp.concatenate([s.astype(MXU_DT), _bd_rows(vn, bdm)], axis=0))
             for qq, a, at, s, vn in zip(q, gw, attn, st, v_new)]
        last = [CHUNK - 1 if d == 0 else 0 for d in dirs]
        g_last = [a[r:r + 1, :] for a, r in zip(gw, last)]
        for i, ((b, d), off) in enumerate(zip(streams, offs)):
            (o_f if d == 0 else o_b)[b, off:off + CHUNK, :] = o[i]
            upd = _mm_tn(k[i] * jnp.exp(g_last[i] - gw[i]), v_new[i])
            s_ref[i] = jnp.exp(g_last[i]) * st[i] + bdm.astype(F32) * upd


def _gdn(d_qkv, d_g, d_gT, wc, n_ctx_steps):
    b, t, _ = d_qkv.shape
    n_steps = t // STEP
    fwd, bwd, tok, tokT, full = _scan_specs(b, n_ctx_steps, n_steps)
    consts = [wc[k] for k in ("cm", "cmw", "strictw", "eyew", "pairw", "bdm", "e_lo", "e_hi")]
    out = jax.ShapeDtypeStruct((b, t, GROUP_W), F32)
    return pl.pallas_call(
        _gdn_body,
        out_shape=[out, out],
        grid=(n_steps,),
        in_specs=[tok(3 * GROUP_W, fwd), tok(3 * GROUP_W, bwd), tok(128, fwd), tok(128, bwd), tokT(fwd), tokT(bwd)]
                 + [full(a) for a in consts],
        out_specs=[tok(GROUP_W, fwd), tok(GROUP_W, bwd)],
        scratch_shapes=[pltpu.VMEM((2 * b, GROUP_W, GROUP_W), F32)],
        compiler_params=_params("arbitrary"),
        name="gdn",
    )(d_qkv, d_qkv, d_g, d_g, d_gT, d_gT, *consts)


def _rope_tables(n_ctx, n_lat):
    rows = n_lat // GRID_W
    row = np.repeat(np.arange(rows), GRID_W).astype(np.float32)
    col = np.tile(np.arange(GRID_W), rows).astype(np.float32)
    half = DA_QK // 2
    inv = np.power(ROPE_BASE, -np.arange(0, half, 2, dtype=np.float32) / half).astype(np.float32)

    def tab(p):
        ang = p[:, None] * inv
        ang = np.concatenate([ang, ang], axis=-1)
        return np.cos(ang), np.sin(ang)

    cr, sr = tab(row)
    cc, sc = tab(col)
    cos = np.concatenate([cr, cc], -1)
    sin = np.concatenate([sr, sc], -1)
    cos = np.concatenate([np.ones((n_ctx, DA_QK), np.float32), cos], 0)
    sin = np.concatenate([np.zeros((n_ctx, DA_QK), np.float32), sin], 0)
    reps = GROUP_W // DA_QK
    cos, sin = np.tile(cos, (1, reps)), np.tile(sin, (1, reps))
    first = (np.arange(GROUP_W) % 16) < 8
    sin_a = np.where(first, -sin, 0.0)
    sin_b = np.where(first, 0.0, sin)
    return jnp.asarray(cos, F32), jnp.asarray(sin_a, F32), jnp.asarray(sin_b, F32)


def _diff_prep_body(qk_ref, v_ref, cos_ref, sa_ref, sb_ref, nw_ref, ind_ref, q_out, kt_out, v_out):
    ind = ind_ref[...]
    cos, sa, sb = cos_ref[...], sa_ref[...], sb_ref[...]

    def norm_rope(x, w):
        xn = x * lax.rsqrt(_mm_sel_r(x * x, ind) * (1.0 / DA_QK) + EPS) * w
        return xn * cos + pltpu.roll(xn, GROUP_W - 8, 1) * sa + pltpu.roll(xn, 8, 1) * sb

    qk = qk_ref[...]
    q = norm_rope(qk[:, :GROUP_W], nw_ref[0:1, :]) * (DA_QK ** -0.5 * math.log2(math.e))
    k = norm_rope(qk[:, GROUP_W:], nw_ref[1:2, :])
    for g in range(GROUP_W // DA_QK):
        q_out[g] = q[:, g * DA_QK:(g + 1) * DA_QK].astype(q_out.dtype)
    kt_out[...] = k.T.astype(kt_out.dtype)
    v = v_ref[...]
    ones = jnp.ones((v.shape[0], HEAD_DIM), F32)
    for h in range(N_HEADS):
        v_out[h] = jnp.concatenate([v[:, h * HEAD_DIM:(h + 1) * HEAD_DIM], ones], axis=1).astype(v_out.dtype)


def _diff_prep(a_qk, a_vz, tables, nw, ind32):
    b, t, _ = a_qk.shape
    tm = TOKEN_TILE
    cos, sa, sb = tables
    tab = pl.BlockSpec((tm, GROUP_W), lambda bi, i: (i, 0))
    ng = GROUP_W // DA_QK
    return pl.pallas_call(
        _diff_prep_body,
        out_shape=[jax.ShapeDtypeStruct((b, ng, t, DA_QK), MXU_DT),
                   jax.ShapeDtypeStruct((b, GROUP_W, t), MXU_DT),
                   jax.ShapeDtypeStruct((b, N_HEADS, t, 128), MXU_DT)],
        grid=(b, t // tm),
        in_specs=[pl.BlockSpec((None, tm, 2 * GROUP_W), lambda bi, i: (bi, i, 0)),
                  pl.BlockSpec((None, tm, GROUP_W), lambda bi, i: (bi, i, 0)),
                  tab, tab, tab,
                  pl.BlockSpec(nw.shape, lambda bi, i: (0, 0)),
                  pl.BlockSpec(ind32.shape, lambda bi, i: (0, 0))],
        out_specs=[pl.BlockSpec((None, ng, tm, DA_QK), lambda bi, i: (bi, 0, i, 0)),
                   pl.BlockSpec((None, GROUP_W, tm), lambda bi, i: (bi, 0, i)),
                   pl.BlockSpec((None, N_HEADS, tm, 128), lambda bi, i: (bi, 0, i, 0))],
        compiler_params=_params("parallel", "parallel"),
        name="diff_prep",
    )(a_qk, a_vz, cos, sa, sb, nw, ind32)


def _diff_attn_body(lam_ref, q_ref, kt_ref, vp_ref, vc_ref, o_ref, acc_ref, m_ref, e_ref, *, lam_init):
    j = pl.program_id(3)

    @pl.when(j == 0)
    def _():
        acc_ref[...] = jnp.zeros_like(acc_ref)
        m_ref[...] = jnp.full_like(m_ref, -jnp.inf)
        e_ref[...] = jnp.zeros_like(e_ref)

    vp = vp_ref[...]
    for p in range(2):
        s = jnp.dot(q_ref[p], kt_ref[p * DA_QK:(p + 1) * DA_QK, :], preferred_element_type=F32)
        pv = jnp.dot(e_ref[p], vp, preferred_element_type=F32)
        m_old = m_ref[p][:, 0:1]
        m_new = jnp.maximum(m_old, jnp.max(s, axis=1, keepdims=True))
        e_ref[p] = jnp.exp2(s - m_new).astype(e_ref.dtype)
        acc_ref[p] = jnp.exp2(m_old - m_new) * (acc_ref[p] + pv)
        m_ref[p] = jnp.broadcast_to(m_new, m_ref.shape[1:])

    @pl.when(j == pl.num_programs(3) - 1)
    def _():
        lam = lam_ref[...]
        lam_full = (jnp.exp(jnp.sum(lam[0:1] * lam[1:2], keepdims=True))
                    - jnp.exp(jnp.sum(lam[2:3] * lam[3:4], keepdims=True)) + lam_init)
        vc = vc_ref[...]
        a0 = acc_ref[0] + jnp.dot(e_ref[0], vc, preferred_element_type=F32)
        a1 = acc_ref[1] + jnp.dot(e_ref[1], vc, preferred_element_type=F32)
        o_ref[...] = (a0[:, :HEAD_DIM] / a0[:, HEAD_DIM:HEAD_DIM + 1]
                      - lam_full * (a1[:, :HEAD_DIM] / a1[:, HEAD_DIM:HEAD_DIM + 1]))


def _diff_attn(lam, q, kt, v1, lam_init, n_keys, tq, tk):
    b, _, n_q, _ = q.shape
    return pl.pallas_call(
        functools.partial(_diff_attn_body, lam_init=lam_init),
        out_shape=jax.ShapeDtypeStruct((b, N_HEADS, n_q, HEAD_DIM), F32),
        grid=(b, N_HEADS, n_q // tq, n_keys // tk),
        in_specs=[pl.BlockSpec(lam.shape, lambda bi, h, i, j: (0, 0)),
                  pl.BlockSpec((None, 2, tq, DA_QK), lambda bi, h, i, j: (bi, h, i, 0)),
                  pl.BlockSpec((None, 2 * DA_QK, tk), lambda bi, h, i, j: (bi, h, j)),
                  pl.BlockSpec((None, None, tk, 128), lambda bi, h, i, j: (bi, h, jnp.maximum(j - 1, 0), 0)),
                  pl.BlockSpec((None, None, tk, 128), lambda bi, h, i, j: (bi, h, j, 0))],
        out_specs=pl.BlockSpec((None, None, tq, HEAD_DIM), lambda bi, h, i, j: (bi, h, i, 0)),
        scratch_shapes=[pltpu.VMEM((2, tq, 128), F32), pltpu.VMEM((2, tq, 128), F32),
                        pltpu.VMEM((2, tq, tk), MXU_DT)],
        compiler_params=_params("parallel", "parallel", "parallel", "arbitrary"),
        name="diff_attn",
    )(lam, q, kt, v1, v1)


def _finish_body(x_ref, mhf_ref, mhb_ref, moz_ref, ao_ref, az_ref, ghf_ref, ghb_ref, gz_ref, dhf_ref, dhb_ref, dz_ref,
                 nw_ref, ind_ref, wout_ref, gt_ref, o_ref, *, lam_init):
    ind = ind_ref[...]

    def head_norm(hh, w):
        return hh * lax.rsqrt(_mm_sel_r(hh * hh, ind) * (1.0 / HEAD_DIM) + EPS) * w

    moz = moz_ref[...]
    ym = (head_norm(mhf_ref[...] + mhb_ref[...], nw_ref[0:1, :]) * jax.nn.sigmoid(moz[:, :GROUP_W])
          * jax.nn.silu(moz[:, GROUP_W:]))
    ao = jnp.concatenate([ao_ref[h] for h in range(N_HEADS)], axis=1)
    ya = head_norm(ao, nw_ref[1:2, :]) * (1.0 - lam_init) * jax.nn.silu(az_ref[...])
    yg = head_norm(ghf_ref[...] + ghb_ref[...], nw_ref[2:3, :]) * jax.nn.silu(gz_ref[...])
    yd = head_norm(dhf_ref[...] + dhb_ref[...], nw_ref[3:4, :]) * jax.nn.silu(dz_ref[...])
    y = jnp.concatenate([ym, ya, yg, yd], axis=1).astype(MXU_DT)
    o_ref[...] = x_ref[...] + gt_ref[...] * jnp.dot(y, wout_ref[...], preferred_element_type=F32)


def _finish(xs, mh, m_oz, ao, a_vz, gh, g_vz, dh, d_z, nw, ind64, wout, gtsel, lam_init, n_ctx_tiles):
    b, t, d = xs.shape
    tm = TOKEN_TILE
    tok = lambda w, c=0: pl.BlockSpec((None, tm, w), lambda bi, i: (bi, i, c))
    g = tok(GROUP_W)
    full = lambda a: pl.BlockSpec(a.shape, lambda bi, i: (0,) * a.ndim)
    return pl.pallas_call(
        functools.partial(_finish_body, lam_init=lam_init),
        out_shape=jax.ShapeDtypeStruct((b, t, d), F32),
        grid=(b, t // tm),
        in_specs=[tok(d), g, g, tok(2 * GROUP_W),
                  pl.BlockSpec((None, N_HEADS, tm, HEAD_DIM), lambda bi, i: (bi, 0, i, 0)),
                  tok(GROUP_W, 1), g, g, tok(GROUP_W, 1), g, g, g,
                  full(nw), full(ind64), full(wout),
                  pl.BlockSpec((None, None, 1, d), lambda bi, i: (bi, (i >= n_ctx_tiles).astype(jnp.int32), 0, 0))],
        out_specs=tok(d),
        compiler_params=_params("parallel", "parallel"),
        name="finish",
    )(xs, mh[0], mh[1], m_oz, ao, a_vz, gh[0], gh[1], g_vz, dh[0], dh[1], d_z, nw, ind64, wout, gtsel)


def kernel(x, c, ctx, c_ctx, norm_w, w_mod, b_mod, w_in, w_out, mlstm_b_i, mlstm_b_f, mlstm_norm, diff_q_norm,
           diff_k_norm, diff_lambda, diff_norm, gla_w_up, gla_b, gla_norm, gdn_conv, gdn_a_log, gdn_dt_bias,
           gdn_norm):
    bsz, n_lat, d = x.shape
    n_ctx = ctx.shape[1]
    depth = w_in.shape[0]
    t = n_ctx + n_lat
    assert n_ctx % TOKEN_TILE == 0 and n_lat % TOKEN_TILE == 0 and n_lat % GRID_W == 0
    assert n_lat % ATTN_TQ == 0 and t % ATTN_TK == 0 and n_ctx % STEP == 0
    n_ctx_tiles = n_ctx // TOKEN_TILE
    n_ctx_steps = n_ctx // STEP

    cc = jnp.concatenate([c, c_ctx[None, :], jnp.zeros((8 - (bsz + 1) % 8, d), F32)], axis=0)
    mod = _modulation(cc, w_mod, b_mod)

    wide = _wide_consts()
    gla_consts = _gla_consts()
    ind64 = _group_ones(GROUP_W, HEAD_DIM)
    ind32 = _group_ones(GROUP_W, DA_QK)
    tables = _rope_tables(n_ctx, n_lat)
    pad = lambda a, n: jnp.concatenate([a, jnp.zeros(a.shape[:-1] + (n - a.shape[-1],), a.dtype)], axis=-1)

    xs = jnp.concatenate([ctx, x], axis=1)
    for l in range(depth):
        lam_init = 0.8 - 0.6 * math.exp(-0.3 * l)
        sh, sc, gt = jnp.split(mod[l], 3, axis=-1)
        lat = jnp.stack([sh[:bsz], sc[:bsz]], axis=1)
        cx = jnp.broadcast_to(jnp.stack([sh[bsz], sc[bsz]], axis=0)[None], (bsz, 2, d))
        modsel = jnp.stack([cx, lat], axis=1)
        gtsel = jnp.stack([jnp.broadcast_to(gt[bsz][None], (bsz, d)), gt[:bsz]], axis=1)[:, :, None, :]
        wp, wg = _permute_w_in(w_in[l])
        pm = pad(jnp.concatenate([mlstm_b_i[l].reshape(1, 8), mlstm_b_f[l].reshape(1, 8)], axis=1), 128)
        pm = jnp.concatenate([pm, jnp.zeros((7, 128), F32)], axis=0)
        z8 = jnp.zeros((1, 8), F32)
        pd = jnp.concatenate([pad(jnp.concatenate([z8, gdn_a_log[l].reshape(1, 8)], axis=1), 128),
                              pad(jnp.concatenate([z8, gdn_dt_bias[l].reshape(1, 8)], axis=1), 128),
                              jnp.zeros((6, 128), F32)], axis=0)
        wup = jnp.zeros((128, 256), F32)
        wup = wup.at[0:GLA_RANK, 0:128].set(gla_w_up[l, 0]).at[GLA_RANK:2 * GLA_RANK, 128:256].set(gla_w_up[l, 1])
        bgk = gla_b[l].reshape(1, 256)

        (m_qkv, m_oz, m_g, m_gT, a_qk, a_vz, g_qk, g_vz, g_la, d_qkv, d_z, d_g, d_gT) = _inproj(
            xs, norm_w[l][None, :], modsel, wp, wg, pm, pd, wup.astype(MXU_DT), bgk, n_ctx_tiles)

        mh = _mlstm(m_qkv, m_g, m_gT, wide, ind64, n_ctx_steps)

        qkn = jnp.stack([jnp.tile(diff_q_norm[l], GROUP_W // DA_QK), jnp.tile(diff_k_norm[l], GROUP_W // DA_QK)])
        aq, akt, av1 = _diff_prep(a_qk, a_vz, tables, qkn, ind32)
        lam = pad(diff_lambda[l], 128)
        ao_ctx = _diff_attn(lam, aq[:, :, :n_ctx], akt, av1, lam_init, n_ctx, n_ctx, n_ctx)
        ao_lat = _diff_attn(lam, aq[:, :, n_ctx:], akt, av1, lam_init, t, ATTN_TQ, ATTN_TK)
        ao = jnp.concatenate([ao_ctx, ao_lat], axis=2)

        gh = _gla(g_qk, g_vz, g_la, wide, gla_consts, n_ctx_steps)

        dq = _gdn_prep(d_qkv, gdn_conv[l], ind64, n_ctx_tiles)
        dh = _gdn(dq, d_g, d_gT, wide, n_ctx_steps)

        nws = jnp.stack([mlstm_norm[l], diff_norm[l], gla_norm[l], gdn_norm[l]])
        xs = _finish(xs, mh, m_oz, ao, a_vz, gh, g_vz, dh, d_z, nws, ind64, w_out[l].astype(MXU_DT),
                     gtsel, lam_init, n_ctx_tiles)
    return xs[:, n_ctx:]
```

```python
import functools
import math

import jax
import jax.numpy as jnp
import numpy as np
from jax import lax
from jax.experimental import pallas as pl
from jax.experimental.pallas import tpu as pltpu

F32 = jnp.float32
MXU_DT = jnp.bfloat16
EPS = 1e-6
N_HEADS = 4
HEAD_DIM = 64
GROUP_W = N_HEADS * HEAD_DIM
CHUNK = 64
STEP = 2 * CHUNK
DA_QK = 32
GLA_DK = 32
GLA_RANK = 16
GLA_TAU = 16.0
GDN_CONV = 5
GRID_W = 64
ROPE_BASE = 10000.0
TOKEN_TILE = 256
ATTN_TQ = 512
ATTN_TK = 768
V_ROWS = 80
VMEM_LIMIT = 48 * 1024 * 1024
NEG_BIG = -1e30


def _mm(a, b):
    return jnp.dot(a.astype(MXU_DT), b.astype(MXU_DT), preferred_element_type=F32)


def _mm_nt(a, b):
    return lax.dot_general(a.astype(MXU_DT), b.astype(MXU_DT), (((1,), (1,)), ((), ())),
                           preferred_element_type=F32)


def _mm_tn(a, b):
    return lax.dot_general(a.astype(MXU_DT), b.astype(MXU_DT), (((0,), (0,)), ((), ())),
                           preferred_element_type=F32)


def _split2(x):
    hi = x.astype(MXU_DT)
    lo = (x - hi.astype(F32)).astype(MXU_DT)
    return hi, lo


def _mm_sel(sel, x):
    hi, lo = _split2(x)
    s = sel.astype(MXU_DT)
    return (jnp.dot(s, hi, preferred_element_type=F32) + jnp.dot(s, lo, preferred_element_type=F32))


def _mm_sel_r(x, sel):
    hi, lo = _split2(x)
    s = sel.astype(MXU_DT)
    return (jnp.dot(hi, s, preferred_element_type=F32) + jnp.dot(lo, s, preferred_element_type=F32))


def _mm_sel_nt(x, sel):
    hi, lo = _split2(x)
    s = sel.astype(MXU_DT)
    dn = (((1,), (1,)), ((), ()))
    return (lax.dot_general(hi, s, dn, preferred_element_type=F32)
            + lax.dot_general(lo, s, dn, preferred_element_type=F32))


def _group_ones(width, group):
    g = np.arange(width) // group
    return jnp.asarray((g[:, None] == g[None, :]).astype(np.float32))


def _params(*sem):
    return pltpu.CompilerParams(dimension_semantics=sem, vmem_limit_bytes=VMEM_LIMIT)


def _mod_body(cc_ref, w_ref, b_ref, o_ref):
    o_ref[...] = _mm(jax.nn.silu(cc_ref[...]), w_ref[...]) + b_ref[...]


def _modulation(cc, w_mod, b_mod):
    depth, d, d3 = w_mod.shape
    tn = 1024
    return pl.pallas_call(
        _mod_body,
        out_shape=jax.ShapeDtypeStruct((depth, cc.shape[0], d3), F32),
        grid=(depth, d3 // tn),
        in_specs=[pl.BlockSpec(cc.shape, lambda l, j: (0, 0)),
                  pl.BlockSpec((None, d, tn), lambda l, j: (l, 0, j)),
                  pl.BlockSpec((None, 1, tn), lambda l, j: (l, 0, j))],
        out_specs=pl.BlockSpec((None, cc.shape[0], tn), lambda l, j: (l, 0, j)),
        compiler_params=_params("parallel", "parallel"),
        name="modulation",
    )(cc, w_mod, b_mod.reshape(depth, 1, d3))


_SEG = dict(m_qkv=(0, 768), m_oz=(768, 1280), m_g=(1280, 1408), a_qk=(1408, 1920), a_vz=(1920, 2432),
            g_qk=(2432, 2688), g_vz=(2688, 3200), g_r=(3200, 3328), d_qkv=(3328, 4096), d_z=(4096, 4352),
            d_g=(4352, 4480))
P_PAD = 4480


def _permute_w_in(w):
    d = w.shape[0]
    z = lambda n: jnp.zeros((d, n), w.dtype)
    m0, a0, g0, d0 = 0, 1296, 2320, 3120
    cols = [w[:, m0:m0 + 1280], w[:, m0 + 1280:m0 + 1296], z(112),
            w[:, a0:a0 + 512], w[:, a0 + 512:a0 + 1024],
            w[:, g0:g0 + 256], w[:, g0 + 256:g0 + 768], w[:, g0 + 768:g0 + 800], z(96),
            w[:, d0:d0 + 768], w[:, d0 + 768:d0 + 1024], w[:, d0 + 1024:d0 + 1040], z(112)]
    wp = jnp.concatenate(cols, axis=1)
    wg = jnp.concatenate([wp[:, 1280:1408], wp[:, 4352:4480]], axis=1).T
    return wp.astype(MXU_DT), wg.astype(MXU_DT)


def _gate_fns(xm, xd, pm, pd, lane):
    gm = jnp.where(lane < 8, xm + pm[0], jax.nn.log_sigmoid(xm + pm[0]))
    gd = jnp.where(lane < 8, jax.nn.sigmoid(xd), -jnp.exp(pd[0]) * jax.nn.softplus(xd + pd[1]))
    return gm, gd


def _inproj_body(x_ref, nw_ref, mod_ref, wp_ref, wg_ref, pm_ref, pd_ref, pmT_ref, pdT_ref, wup_ref, bgk_ref,
                 m_qkv, m_oz, m_g, m_gT, a_qk, a_vz, g_qk, g_vz, g_la, d_qkv, d_z, d_g, d_gT):
    x = x_ref[...]
    h = x * lax.rsqrt(jnp.mean(x * x, axis=-1, keepdims=True) + EPS) * nw_ref[...]
    h = (h * (1.0 + mod_ref[1:2, :]) + mod_ref[0:1, :]).astype(MXU_DT)
    u = jnp.dot(h, wp_ref[...], preferred_element_type=F32)
    seg = lambda name: u[:, _SEG[name][0]:_SEG[name][1]]
    m_qkv[...] = seg("m_qkv")
    m_oz[...] = seg("m_oz")
    a_qk[...] = seg("a_qk")
    a_vz[...] = seg("a_vz")
    g_qk[...] = seg("g_qk")
    g_vz[...] = seg("g_vz")
    d_qkv[...] = seg("d_qkv")
    d_z[...] = seg("d_z")
    lane = lax.broadcasted_iota(jnp.int32, (1, 128), 1)
    gm, gd = _gate_fns(seg("m_g"), seg("d_g"), (pm_ref[0:1, :],), (pd_ref[0:1, :], pd_ref[1:2, :]), lane)
    m_g[...] = gm
    d_g[...] = gd
    ut = lax.dot_general(wg_ref[...], h, (((1,), (1,)), ((), ())), preferred_element_type=F32)
    row = lax.broadcasted_iota(jnp.int32, (128, 1), 0)
    gmt, gdt = _gate_fns(ut[0:128], ut[128:256], (pmT_ref[:, 0:1],), (pdT_ref[:, 0:1], pdT_ref[:, 1:2]), row)
    m_gT[...] = gmt
    d_gT[...] = gdt
    gk = _mm(seg("g_r"), wup_ref[...]) + bgk_ref[...]
    g_la[...] = jax.nn.log_sigmoid(gk) / GLA_TAU


def _inproj(xs, nw, modsel, wp, wg, pm, pd, wup, bgk, n_ctx_tiles):
    b, t, d = xs.shape
    tm = TOKEN_TILE
    tok = lambda w: pl.BlockSpec((None, tm, w), lambda bi, i: (bi, i, 0))
    tokT = pl.BlockSpec((None, 128, tm), lambda bi, i: (bi, 0, i))
    full = lambda a: pl.BlockSpec(a.shape, lambda bi, i: (0,) * a.ndim)
    widths = [768, 512, 128, None, 512, 512, 256, 512, 256, 768, 256, 128, None]
    out_shape, out_specs = [], []
    for w in widths:
        if w is None:
            out_shape.append(jax.ShapeDtypeStruct((b, 128, t), F32))
            out_specs.append(tokT)
        else:
            out_shape.append(jax.ShapeDtypeStruct((b, t, w), F32))
            out_specs.append(tok(w))
    pmT, pdT = pm.T, pd.T
    return pl.pallas_call(
        _inproj_body,
        out_shape=out_shape,
        grid=(b, t // tm),
        in_specs=[tok(d), full(nw),
                  pl.BlockSpec((None, None, 2, d), lambda bi, i: (bi, (i >= n_ctx_tiles).astype(jnp.int32), 0, 0)),
                  full(wp), full(wg), full(pm), full(pd), full(pmT), full(pdT), full(wup), full(bgk)],
        out_specs=out_specs,
        compiler_params=_params("parallel", "parallel"),
        name="inproj",
    )(xs, nw, modsel, wp, wg, pm, pd, pmT, pdT, wup, bgk)


def _cum_mats_np():
    i = np.arange(CHUNK)
    fwd = (i[None, :] <= i[:, None]).astype(np.float32)
    return np.stack([fwd, fwd.T])


def _step_block(d, n, n_ctx_steps, n_steps):
    bwd = jnp.where(n < n_ctx_steps, n_ctx_steps - 1 - n, n_steps - 1 - (n - n_ctx_steps))
    return jnp.where(d == 0, n, bwd)


def _head_rowmax(x):
    return jnp.concatenate(
        [jnp.broadcast_to(jnp.max(x[:, h * HEAD_DIM:(h + 1) * HEAD_DIM], axis=1, keepdims=True), (CHUNK, HEAD_DIM))
         for h in range(N_HEADS)], axis=1)


def _mlstm_body(qkv_f, qkv_b, g_f, g_b, gT_f, gT_b, cm_ref, cmw_ref, bdm_ref, elo_ref, ehi_ref, ind_ref,
                o_f, o_b, cn_ref, m_ref):
    @pl.when(pl.program_id(0) == 0)
    def _():
        cn_ref[...] = jnp.zeros_like(cn_ref)
        m_ref[...] = jnp.zeros_like(m_ref)

    nb = qkv_f.shape[0]
    streams = [(b, d) for b in range(nb) for d in range(2)]
    dirs = [d for _, d in streams]
    bdm = bdm_ref[...]
    bdm2 = jnp.concatenate([bdm, bdm], axis=1).astype(F32)
    ind = ind_ref[...].astype(MXU_DT)
    ones = jnp.ones((CHUNK, GROUP_W), F32)
    dup = lambda a: jnp.concatenate([a, a], axis=1)
    cm = [cm_ref[d] for d in dirs]
    cmw = [cmw_ref[d] for d in dirs]
    elo = [elo_ref[d] for d in dirs]
    ehi = [ehi_ref[d] for d in dirs]
    for j in range(2):
        offs = [(j if d == 0 else 1 - j) * CHUNK for d in dirs]
        src = [(qkv_f, g_f, gT_f) if d == 0 else (qkv_b, g_b, gT_b) for d in dirs]
        qkv = [r[0][b, o:o + CHUNK, :] for (b, _), r, o in zip(streams, src, offs)]
        g = [r[1][b, o:o + CHUNK, :] for (b, _), r, o in zip(streams, src, offs)]
        gt = [r[2][b, :, o:o + CHUNK] for (b, _), r, o in zip(streams, src, offs)]
        bc = _each(_mm_sel, cm, g)
        br = _each(_mm_sel_nt, gt, cm)
        bw = _each(_mm_sel_r, bc, ehi)
        iw = _each(_mm_sel_r, g, elo)
        rw = [jnp.concatenate([a[4 * d + h:4 * d + h + 1, :] - c[8 + 4 * d + h:9 + 4 * d + h, :]
                               for h in range(N_HEADS)], axis=1) for a, c, d in zip(gt, br, dirs)]
        dmat = [jnp.where(m > 0, a + r, -jnp.inf) for m, a, r in zip(cmw, bw, rw)]
        m_prev = [m_ref[i] for i in range(len(streams))]
        inter = _each(lambda a, m: a + m, bw, m_prev)
        m_t = [jnp.maximum(a, _head_rowmax(x)) for a, x in zip(inter, dmat)]
        q = [x[:, :GROUP_W] for x in qkv]
        k = [x[:, GROUP_W:2 * GROUP_W] * HEAD_DIM ** -0.5 for x in qkv]
        v = [x[:, 2 * GROUP_W:] for x in qkv]
        bdk = [jnp.concatenate([x.T.astype(MXU_DT)] * N_HEADS, axis=1) * bdm for x in k]
        s = [_mm(qq, w) * jnp.exp(x - m) for qq, w, x, m in zip(q, bdk, dmat, m_t)]
        w_inter = [jnp.exp(a - m) for a, m in zip(inter, m_t)]
        cn = [cn_ref[i] for i in range(len(streams))]
        acc = [dup(wi) * _mm(qq, c) + _mm(ss, jnp.concatenate([_bd_rows(vv, bdm), ind], axis=1))
               for wi, qq, c, ss, vv in zip(w_inter, q, cn, s, v)]
        hout = [a[:, :GROUP_W] / jnp.maximum(jnp.abs(a[:, GROUP_W:]), jnp.exp(-m)) for a, m in zip(acc, m_t)]
        last = [CHUNK - 1 if d == 0 else 0 for d in dirs]
        b_last = [a[r:r + 1, :] for a, r in zip(bw, last)]
        ds = [bl - a + i_ for bl, a, i_ in zip(b_last, bw, iw)]
        m_new = [jnp.maximum(bl + m, jnp.max(x, axis=0, keepdims=True)) for bl, m, x in zip(b_last, m_prev, ds)]
        for i, ((b, d), off) in enumerate(zip(streams, offs)):
            (o_f if d == 0 else o_b)[b, off:off + CHUNK, :] = hout[i]
            w_s = jnp.exp(ds[i] - m_new[i])
            w_c = jnp.exp(b_last[i] + m_prev[i] - m_new[i])
            upd = _mm_tn(k[i] * w_s, jnp.concatenate([v[i], ones], axis=1))
            cn_ref[i] = dup(w_c) * cn[i] + bdm2 * upd
            m_ref[i] = m_new[i]


def _scan_specs(b, n_ctx_steps, n_steps):
    fwd = lambda n: n
    bwd = lambda n: _step_block(1, n, n_ctx_steps, n_steps)
    tok = lambda w, f, c=0: pl.BlockSpec((b, STEP, w), lambda n: (0, f(n), c))
    tokT = lambda f: pl.BlockSpec((b, 128, STEP), lambda n: (0, 0, f(n)))
    full = lambda a: pl.BlockSpec(a.shape, lambda n: (0,) * a.ndim)
    return fwd, bwd, tok, tokT, full


def _mlstm(m_qkv, m_g, m_gT, wc, ind64, n_ctx_steps):
    b, t, _ = m_qkv.shape
    n_steps = t // STEP
    fwd, bwd, tok, tokT, full = _scan_specs(b, n_ctx_steps, n_steps)
    consts = [wc[k] for k in ("cm", "cmw", "bdm", "e_lo", "e_hi")] + [ind64]
    out = jax.ShapeDtypeStruct((b, t, GROUP_W), F32)
    return pl.pallas_call(
        _mlstm_body,
        out_shape=[out, out],
        grid=(n_steps,),
        in_specs=[tok(3 * GROUP_W, fwd), tok(3 * GROUP_W, bwd), tok(128, fwd), tok(128, bwd), tokT(fwd), tokT(bwd)]
                 + [full(a) for a in consts],
        out_specs=[tok(GROUP_W, fwd), tok(GROUP_W, bwd)],
        scratch_shapes=[pltpu.VMEM((2 * b, GROUP_W, 2 * GROUP_W), F32), pltpu.VMEM((2 * b, 1, GROUP_W), F32)],
        compiler_params=_params("arbitrary"),
        name="mlstm",
    )(m_qkv, m_qkv, m_g, m_g, m_gT, m_gT, *consts)


def _gla_level_mats_np():
    L = CHUNK
    t = np.arange(L)
    sels, valids, pairs = [], [], []
    for rev in (False, True):
        pos_all = t if not rev else L - 1 - t
        eqs, eks, vq, vk, pm = [], [], [], [], []
        for j in range(6):
            w = 32 >> j
            par = pos_all // (2 * w)
            pos = pos_all % (2 * w)
            same = par[:, None] == par[None, :]
            late = pos >= w
            eq = same & late[:, None] & late[None, :] & (pos[None, :] <= pos[:, None])
            ek = same & (~late)[:, None] & (~late)[None, :] & (pos[None, :] > pos[:, None])
            eqs.append(eq)
            eks.append(ek)
            vq.append(late)
            vk.append(~late)
            pm.append(same & late[:, None] & (~late)[None, :])
        pm.append(np.eye(L, dtype=bool))
        sels.append(np.concatenate(eqs + eks, axis=0))
        valids.append(np.concatenate(vq + vk)[:, None])
        pairs.append(np.concatenate(pm, axis=0))
    f = lambda a: np.stack(a).astype(np.float32)
    return f(sels), f(valids), f(pairs)


def _gla_consts():
    sel, val, pair = _gla_level_mats_np()
    nq = 6 * CHUNK
    tile = lambda a: np.concatenate([a] * N_HEADS, axis=-1)
    selq, selk = sel[:, :nq], sel[:, nq:]
    selkt = np.stack([np.concatenate([selk[d, lv * CHUNK:(lv + 1) * CHUNK].T for lv in range(6)], axis=1)
                      for d in range(2)])
    valkt = np.stack([val[d, nq:, 0][None, :] for d in range(2)])
    rh = np.arange(N_HEADS * GLA_DK) // GLA_DK
    ch = np.arange(GROUP_W) // HEAD_DIM
    bdg = (rh[:, None] == ch[None, :]).astype(np.float32)
    return dict(selq=jnp.asarray(selq), valq=jnp.asarray(val[:, :nq]), selkt=jnp.asarray(selkt),
                valkt=jnp.asarray(valkt), pairw=jnp.asarray(tile(pair)), bdg=jnp.asarray(bdg, MXU_DT))


def _gla_body(qk_f, qk_b, v_f, v_b, la_f, la_b, cm_ref, selq_ref, valq_ref, selkt_ref, valkt_ref, pairw_ref,
              bdg_ref, bdm_ref, o_f, o_b, s_ref):
    @pl.when(pl.program_id(0) == 0)
    def _():
        s_ref[...] = jnp.zeros_like(s_ref)

    nb = qk_f.shape[0]
    streams = [(b, d) for b in range(nb) for d in range(2)]
    dirs = [d for _, d in streams]
    nk = N_HEADS * GLA_DK
    bdg = bdg_ref[...]
    bdm = bdm_ref[...]
    bdk = lambda xt: jnp.concatenate([xt.astype(MXU_DT)] * N_HEADS, axis=1) * bdg
    cm = [cm_ref[d] for d in dirs]
    selq = [selq_ref[d] for d in dirs]
    valq = [valq_ref[d] for d in dirs]
    selkt = [selkt_ref[d] for d in dirs]
    valkt = [valkt_ref[d] for d in dirs]
    pairw = [[pairw_ref[d, lv * CHUNK:(lv + 1) * CHUNK, :] for d in dirs] for lv in range(7)]
    for j in range(2):
        offs = [(j if d == 0 else 1 - j) * CHUNK for d in dirs]
        src = [(qk_f, v_f, la_f) if d == 0 else (qk_b, v_b, la_b) for d in dirs]
        qk = [r[0][b, o:o + CHUNK, :] for (b, _), r, o in zip(streams, src, offs)]
        v = [r[1][b, o:o + CHUNK, :] for (b, _), r, o in zip(streams, src, offs)]
        la = [r[2][b, o:o + CHUNK, d * nk:(d + 1) * nk] for (b, d), r, o in zip(streams, src, offs)]
        q = [x[:, :nk] * GLA_DK ** -0.5 for x in qk]
        k = [x[:, nk:] for x in qk]
        lat = [x.T for x in la]
        kt = [x.T for x in k]
        bcum = _each(_mm_sel, cm, la)
        exq = [jnp.exp(_mm_sel(s, x)) * vq for s, x, vq in zip(selq, la, valq)]
        exk = [jnp.exp(_mm_sel_r(x, s)) * vk for s, x, vk in zip(selkt, lat, valkt)]
        a = [p * _mm(qq, bdk(x)) for p, qq, x in zip(pairw[6], q, kt)]
        for lv in range(6):
            sl = slice(lv * CHUNK, (lv + 1) * CHUNK)
            a = [aa + p * _mm(qq * eq[sl], bdk(x * ek[:, sl]))
                 for aa, p, qq, eq, x, ek in zip(a, pairw[lv], q, exq, kt, exk)]
        st = [s_ref[i] for i in range(len(streams))]
        o = [_mm(jnp.concatenate([qq * jnp.exp(bc), aa], axis=1),
                 jnp.concatenate([s.astype(MXU_DT), _bd_rows(vv, bdm)], axis=0))
             for qq, bc, aa, s, vv in zip(q, bcum, a, st, v)]
        last = [CHUNK - 1 if d == 0 else 0 for d in dirs]
        b_last = [x[r:r + 1, :] for x, r in zip(bcum, last)]
        b_last_t = [jnp.sum(x, axis=1, keepdims=True) for x in lat]
        for i, ((b, d), off) in enumerate(zip(streams, offs)):
            (o_f if d == 0 else o_b)[b, off:off + CHUNK, :] = o[i]
            upd = _mm_tn(k[i] * jnp.exp(b_last[i] - bcum[i]), v[i])
            s_ref[i] = jnp.exp(b_last_t[i]) * st[i] + bdg.astype(F32) * upd


def _gla(g_qk, g_vz, g_la, wc, gc, n_ctx_steps):
    b, t, _ = g_qk.shape
    n_steps = t // STEP
    fwd, bwd, tok, tokT, full = _scan_specs(b, n_ctx_steps, n_steps)
    consts = [wc["cm"]] + [gc[k] for k in ("selq", "valq", "selkt", "valkt", "pairw", "bdg")] + [wc["bdm"]]
    out = jax.ShapeDtypeStruct((b, t, GROUP_W), F32)
    w = 2 * N_HEADS * GLA_DK
    return pl.pallas_call(
        _gla_body,
        out_shape=[out, out],
        grid=(n_steps,),
        in_specs=[tok(w, fwd), tok(w, bwd), tok(GROUP_W, fwd), tok(GROUP_W, bwd),
                  tok(w, fwd), tok(w, bwd)] + [full(a) for a in consts],
        out_specs=[tok(GROUP_W, fwd), tok(GROUP_W, bwd)],
        scratch_shapes=[pltpu.VMEM((2 * b, N_HEADS * GLA_DK, GROUP_W), F32)],
        compiler_params=_params("arbitrary"),
        name="gla",
    )(g_qk, g_qk, g_vz, g_vz, g_la, g_la, *consts)


def _gdn_prep_body(prev_ref, cur_ref, next_ref, w_ref, ind_ref, o_ref, ext_ref, *, n_ctx_tiles, n_tiles):
    i = pl.program_id(1)
    tm = cur_ref.shape[0]
    first = (i == 0) | (i == n_ctx_tiles)
    last = (i == n_ctx_tiles - 1) | (i == n_tiles - 1)
    ext_ref[0:8, :] = jnp.where(first, 0.0, prev_ref[...])
    ext_ref[8:8 + tm, :] = cur_ref[...]
    ext_ref[8 + tm:16 + tm, :] = jnp.where(last, 0.0, next_ref[...])
    half = GDN_CONV // 2
    y = jnp.zeros(cur_ref.shape, F32)
    for j in range(GDN_CONV):
        y = y + ext_ref[8 - half + j:8 - half + j + tm, :] * w_ref[j:j + 1, :]
    y = jax.nn.silu(y)
    ind = ind_ref[...]
    q, k = y[:, :GROUP_W], y[:, GROUP_W:2 * GROUP_W]
    qn = q * lax.rsqrt(_mm_sel_r(q * q, ind) + EPS) * HEAD_DIM ** -0.5
    kn = k * lax.rsqrt(_mm_sel_r(k * k, ind) + EPS)
    o_ref[:, :GROUP_W] = qn
    o_ref[:, GROUP_W:2 * GROUP_W] = kn
    o_ref[:, 2 * GROUP_W:] = y[:, 2 * GROUP_W:]


def _gdn_prep(d_qkv, conv_w, ind64, n_ctx_tiles):
    b, t, w = d_qkv.shape
    tm = TOKEN_TILE
    n_tiles = t // tm
    r = tm // 8
    n8 = t // 8
    return pl.pallas_call(
        functools.partial(_gdn_prep_body, n_ctx_tiles=n_ctx_tiles, n_tiles=n_tiles),
        out_shape=jax.ShapeDtypeStruct((b, t, w), F32),
        grid=(b, n_tiles),
        in_specs=[pl.BlockSpec((None, 8, w), lambda bi, i: (bi, jnp.maximum(i * r - 1, 0), 0)),
                  pl.BlockSpec((None, tm, w), lambda bi, i: (bi, i, 0)),
                  pl.BlockSpec((None, 8, w), lambda bi, i: (bi, jnp.minimum((i + 1) * r, n8 - 1), 0)),
                  pl.BlockSpec(conv_w.shape, lambda bi, i: (0, 0)),
                  pl.BlockSpec(ind64.shape, lambda bi, i: (0, 0))],
        out_specs=pl.BlockSpec((None, tm, w), lambda bi, i: (bi, i, 0)),
        scratch_shapes=[pltpu.VMEM((tm + 16, w), F32)],
        compiler_params=_params("parallel", "parallel"),
        name="gdn_prep",
    )(d_qkv, d_qkv, d_qkv, conv_w, ind64)


def _wide_consts():
    cm = _cum_mats_np()
    eye = np.eye(CHUNK, dtype=np.float32)
    pair = _gla_level_mats_np()[2].reshape(2, 7, CHUNK, CHUNK)[:, :6]
    tile = lambda a: np.concatenate([a] * N_HEADS, axis=-1)
    hb = np.arange(GROUP_W) // HEAD_DIM
    bdm = (hb[:, None] == hb[None, :]).astype(np.float32)
    e_lo = np.zeros((2, 128, GROUP_W), np.float32)
    e_hi = np.zeros((2, 128, GROUP_W), np.float32)
    for d in range(2):
        for h in range(N_HEADS):
            e_lo[d, d * 4 + h, h * HEAD_DIM:(h + 1) * HEAD_DIM] = 1.0
            e_hi[d, 8 + d * 4 + h, h * HEAD_DIM:(h + 1) * HEAD_DIM] = 1.0
    return dict(cm=jnp.asarray(cm), cmw=jnp.asarray(tile(cm)), strictw=jnp.asarray(tile(cm - eye)),
                eyew=jnp.asarray(tile(eye)), pairw=jnp.asarray(tile(pair).reshape(2, 6 * CHUNK, GROUP_W)),
                bdm=jnp.asarray(bdm, MXU_DT), e_lo=jnp.asarray(e_lo), e_hi=jnp.asarray(e_hi))


def _bd_rows(x, bdm):
    xb = x.astype(MXU_DT)
    return jnp.concatenate([xb] * N_HEADS, axis=0) * bdm


def _each(f, *lists):
    return [f(*a) for a in zip(*lists)]


def _gdn_body(qkv_f, qkv_b, g_f, g_b, gT_f, gT_b, cm_ref, cmw_ref, strictw_ref, eyew_ref, pairw_ref, bdm_ref,
              elo_ref, ehi_ref, o_f, o_b, s_ref):
    @pl.when(pl.program_id(0) == 0)
    def _():
        s_ref[...] = jnp.zeros_like(s_ref)

    nb = qkv_f.shape[0]
    streams = [(b, d) for b in range(nb) for d in range(2)]
    dirs = [d for _, d in streams]
    bdm = bdm_ref[...]
    bdm2 = jnp.concatenate([bdm, bdm], axis=1)
    eyew = eyew_ref[...]
    cm = [cm_ref[d] for d in dirs]
    cmw = [cmw_ref[d] for d in dirs]
    strictw = [strictw_ref[d] for d in dirs]
    pairw = [[pairw_ref[d, lv * CHUNK:(lv + 1) * CHUNK, :] for d in dirs] for lv in range(6)]
    elo = [elo_ref[d] for d in dirs]
    ehi = [ehi_ref[d] for d in dirs]
    for j in range(2):
        offs = [(j if d == 0 else 1 - j) * CHUNK for d in dirs]
        src = [(qkv_f, g_f, gT_f) if d == 0 else (qkv_b, g_b, gT_b) for d in dirs]
        qkv = [r[0][b, o:o + CHUNK, :] for (b, _), r, o in zip(streams, src, offs)]
        g = [r[1][b, o:o + CHUNK, :] for (b, _), r, o in zip(streams, src, offs)]
        gt = [r[2][b, :, o:o + CHUNK] for (b, _), r, o in zip(streams, src, offs)]
        gc = _each(_mm_sel, cm, g)
        gr = _each(_mm_sel_nt, gt, cm)
        beta = _each(_mm_sel_r, g, elo)
        gw = _each(_mm_sel_r, gc, ehi)
        grow = [jnp.concatenate([x[8 + 4 * d + h:9 + 4 * d + h, :] for h in range(N_HEADS)], axis=1)
                for x, d in zip(gr, dirs)]
        decay = [jnp.exp(jnp.where(m > 0, a - r, -jnp.inf)) for m, a, r in zip(cmw, gw, grow)]
        q = [x[:, :GROUP_W] for x in qkv]
        k = [x[:, GROUP_W:2 * GROUP_W] for x in qkv]
        v = [x[:, 2 * GROUP_W:] for x in qkv]
        kb = _each(lambda a, b_: a * b_, k, beta)
        bdk = [jnp.concatenate([x.T.astype(MXU_DT)] * N_HEADS, axis=1) * bdm for x in k]
        gq = _each(lambda a, b_, w: _mm(jnp.concatenate([a, b_], axis=0), w), kb, q, bdk)
        lower = [s * x[:CHUNK] * dc for s, x, dc in zip(strictw, gq, decay)]
        attn = [x[CHUNK:] * dc for x, dc in zip(gq, decay)]
        tw = [eyew - lo * p for lo, p in zip(lower, pairw[5])]
        for lv in range(4, -1, -1):
            y = [_mm(lo * p, _bd_rows(t, bdm)) for lo, p, t in zip(lower, pairw[lv], tw)]
            tw = [t - _mm(t, _bd_rows(yy, bdm)) for t, yy in zip(tw, y)]
        rhs = [jnp.concatenate([vv * b_, kk * jnp.exp(a)], axis=1).astype(MXU_DT)
               for vv, b_, kk, a in zip(v, beta, kb, gw)]
        uw = [_mm(t, jnp.concatenate([r] * N_HEADS, axis=0) * bdm2) for t, r in zip(tw, rhs)]
        st = [s_ref[i] for i in range(len(streams))]
        v_new = [x[:, :GROUP_W] - _mm(x[:, GROUP_W:], s) for x, s in zip(uw, st)]
        o = [_mm(jnp.concatenate([qq * jnp.exp(a), at], axis=1),
                 jnp.concatenate([s.astype(MXU_DT), _bd_rows(vn, bdm)], axis=0))
             for qq, a, at, s, vn in zip(q, gw, attn, st, v_new)]
        last = [CHUNK - 1 if d == 0 else 0 for d in dirs]
        g_last = [a[r:r + 1, :] for a, r in zip(gw, last)]
        for i, ((b, d), off) in enumerate(zip(streams, offs)):
            (o_f if d == 0 else o_b)[b, off:off + CHUNK, :] = o[i]
            upd = _mm_tn(k[i] * jnp.exp(g_last[i] - gw[i]), v_new[i])
            s_ref[i] = jnp.exp(g_last[i]) * st[i] + bdm.astype(F32) * upd


def _gdn(d_qkv, d_g, d_gT, wc, n_ctx_steps):
    b, t, _ = d_qkv.shape
    n_steps = t // STEP
    fwd, bwd, tok, tokT, full = _scan_specs(b, n_ctx_steps, n_steps)
    consts = [wc[k] for k in ("cm", "cmw", "strictw", "eyew", "pairw", "bdm", "e_lo", "e_hi")]
    out = jax.ShapeDtypeStruct((b, t, GROUP_W), F32)
    return pl.pallas_call(
        _gdn_body,
        out_shape=[out, out],
        grid=(n_steps,),
        in_specs=[tok(3 * GROUP_W, fwd), tok(3 * GROUP_W, bwd), tok(128, fwd), tok(128, bwd), tokT(fwd), tokT(bwd)]
                 + [full(a) for a in consts],
        out_specs=[tok(GROUP_W, fwd), tok(GROUP_W, bwd)],
        scratch_shapes=[pltpu.VMEM((2 * b, GROUP_W, GROUP_W), F32)],
        compiler_params=_params("arbitrary"),
        name="gdn",
    )(d_qkv, d_qkv, d_g, d_g, d_gT, d_gT, *consts)


def _rope_tables(n_ctx, n_lat):
    rows = n_lat // GRID_W
    row = np.repeat(np.arange(rows), GRID_W).astype(np.float32)
    col = np.tile(np.arange(GRID_W), rows).astype(np.float32)
    half = DA_QK // 2
    inv = np.power(ROPE_BASE, -np.arange(0, half, 2, dtype=np.float32) / half).astype(np.float32)

    def tab(p):
        ang = p[:, None] * inv
        ang = np.concatenate([ang, ang], axis=-1)
        return np.cos(ang), np.sin(ang)

    cr, sr = tab(row)
    cc, sc = tab(col)
    cos = np.concatenate([cr, cc], -1)
    sin = np.concatenate([sr, sc], -1)
    cos = np.concatenate([np.ones((n_ctx, DA_QK), np.float32), cos], 0)
    sin = np.concatenate([np.zeros((n_ctx, DA_QK), np.float32), sin], 0)
    reps = GROUP_W // DA_QK
    cos, sin = np.tile(cos, (1, reps)), np.tile(sin, (1, reps))
    first = (np.arange(GROUP_W) % 16) < 8
    sin_a = np.where(first, -sin, 0.0)
    sin_b = np.where(first, 0.0, sin)
    return jnp.asarray(cos, F32), jnp.asarray(sin_a, F32), jnp.asarray(sin_b, F32)


def _diff_prep_body(qk_ref, v_ref, cos_ref, sa_ref, sb_ref, nw_ref, ind_ref, qt_out, k_out, vt_out):
    ind = ind_ref[...]
    cos, sa, sb = cos_ref[...], sa_ref[...], sb_ref[...]

    def norm_rope(x, w):
        xn = x * lax.rsqrt(_mm_sel_r(x * x, ind) * (1.0 / DA_QK) + EPS) * w
        return xn * cos + pltpu.roll(xn, GROUP_W - 8, 1) * sa + pltpu.roll(xn, 8, 1) * sb

    qk = qk_ref[...]
    q = norm_rope(qk[:, :GROUP_W], nw_ref[0:1, :]) * (DA_QK ** -0.5 * math.log2(math.e))
    k = norm_rope(qk[:, GROUP_W:], nw_ref[1:2, :])
    qt_out[...] = q.T.astype(qt_out.dtype)
    for g in range(GROUP_W // DA_QK):
        k_out[g] = k[:, g * DA_QK:(g + 1) * DA_QK].astype(k_out.dtype)
    vt = v_ref[...].T
    tm = vt.shape[1]
    tail = jnp.concatenate([jnp.ones((8, tm), F32), jnp.zeros((V_ROWS - HEAD_DIM - 8, tm), F32)], axis=0)
    for h in range(N_HEADS):
        vt_out[h] = jnp.concatenate([vt[h * HEAD_DIM:(h + 1) * HEAD_DIM], tail], axis=0).astype(vt_out.dtype)


def _diff_prep(a_qk, a_vz, tables, nw, ind32):
    b, t, _ = a_qk.shape
    tm = TOKEN_TILE
    cos, sa, sb = tables
    tab = pl.BlockSpec((tm, GROUP_W), lambda bi, i: (i, 0))
    ng = GROUP_W // DA_QK
    return pl.pallas_call(
        _diff_prep_body,
        out_shape=[jax.ShapeDtypeStruct((b, GROUP_W, t), MXU_DT),
                   jax.ShapeDtypeStruct((b, ng, t, DA_QK), MXU_DT),
                   jax.ShapeDtypeStruct((b, N_HEADS, V_ROWS, t), MXU_DT)],
        grid=(b, t // tm),
        in_specs=[pl.BlockSpec((None, tm, 2 * GROUP_W), lambda bi, i: (bi, i, 0)),
                  pl.BlockSpec((None, tm, GROUP_W), lambda bi, i: (bi, i, 0)),
                  tab, tab, tab,
                  pl.BlockSpec(nw.shape, lambda bi, i: (0, 0)),
                  pl.BlockSpec(ind32.shape, lambda bi, i: (0, 0))],
        out_specs=[pl.BlockSpec((None, GROUP_W, tm), lambda bi, i: (bi, 0, i)),
                   pl.BlockSpec((None, ng, tm, DA_QK), lambda bi, i: (bi, 0, i, 0)),
                   pl.BlockSpec((None, N_HEADS, V_ROWS, tm), lambda bi, i: (bi, 0, 0, i))],
        compiler_params=_params("parallel", "parallel"),
        name="diff_prep",
    )(a_qk, a_vz, cos, sa, sb, nw, ind32)


def _diff_attn_body(lam_ref, qt_ref, k_ref, vp_ref, vc_ref, o_ref, acc_ref, m_ref, e_ref, *, lam_init):
    j = pl.program_id(3)

    @pl.when(j == 0)
    def _():
        acc_ref[...] = jnp.zeros_like(acc_ref)
        m_ref[...] = jnp.full_like(m_ref, -jnp.inf)
        e_ref[...] = jnp.zeros_like(e_ref)

    vp = vp_ref[...]
    s = [jnp.dot(k_ref[p], qt_ref[p * DA_QK:(p + 1) * DA_QK, :], preferred_element_type=F32)
         for p in range(2)]
    pv = [jnp.dot(vp, e_ref[p], preferred_element_type=F32) for p in range(2)]
    for p in range(2):
        m_old = m_ref[p]
        m_new = jnp.maximum(m_old, jnp.max(s[p], axis=0, keepdims=True))
        e_ref[p] = jnp.exp2(s[p] - m_new).astype(e_ref.dtype)
        acc_ref[p] = jnp.exp2(m_old - m_new) * (acc_ref[p] + pv[p])
        m_ref[p] = m_new

    @pl.when(j == pl.num_programs(3) - 1)
    def _():
        lam = lam_ref[...]
        lam_full = (jnp.exp(jnp.sum(lam[0:1] * lam[1:2], keepdims=True))
                    - jnp.exp(jnp.sum(lam[2:3] * lam[3:4], keepdims=True)) + lam_init)
        vc = vc_ref[...]
        a0 = acc_ref[0] + jnp.dot(vc, e_ref[0], preferred_element_type=F32)
        a1 = acc_ref[1] + jnp.dot(vc, e_ref[1], preferred_element_type=F32)
        ot = (a0[:HEAD_DIM] / a0[HEAD_DIM:HEAD_DIM + 1]
              - lam_full * (a1[:HEAD_DIM] / a1[HEAD_DIM:HEAD_DIM + 1]))
        o_ref[...] = ot.T


def _diff_attn(lam, qt, k, vt, lam_init, n_keys, tq, tk):
    b, _, n_q = qt.shape
    return pl.pallas_call(
        functools.partial(_diff_attn_body, lam_init=lam_init),
        out_shape=jax.ShapeDtypeStruct((b, N_HEADS, n_q, HEAD_DIM), F32),
        grid=(b, N_HEADS, n_q // tq, n_keys // tk),
        in_specs=[pl.BlockSpec(lam.shape, lambda bi, h, i, j: (0, 0)),
                  pl.BlockSpec((None, 2 * DA_QK, tq), lambda bi, h, i, j: (bi, h, i)),
                  pl.BlockSpec((None, 2, tk, DA_QK), lambda bi, h, i, j: (bi, h, j, 0)),
                  pl.BlockSpec((None, None, V_ROWS, tk), lambda bi, h, i, j: (bi, h, 0, jnp.maximum(j - 1, 0))),
                  pl.BlockSpec((None, None, V_ROWS, tk), lambda bi, h, i, j: (bi, h, 0, j))],
        out_specs=pl.BlockSpec((None, None, tq, HEAD_DIM), lambda bi, h, i, j: (bi, h, i, 0)),
        scratch_shapes=[pltpu.VMEM((2, V_ROWS, tq), F32), pltpu.VMEM((2, 1, tq), F32),
                        pltpu.VMEM((2, tk, tq), MXU_DT)],
        compiler_params=_params("parallel", "parallel", "parallel", "arbitrary"),
        name="diff_attn",
    )(lam, qt, k, vt, vt)


def _finish_body(x_ref, mhf_ref, mhb_ref, moz_ref, ao_ref, az_ref, ghf_ref, ghb_ref, gz_ref, dhf_ref, dhb_ref, dz_ref,
                 nw_ref, ind_ref, wout_ref, gt_ref, o_ref, *, lam_init):
    ind = ind_ref[...]

    def head_norm(hh, w):
        return hh * lax.rsqrt(_mm_sel_r(hh * hh, ind) * (1.0 / HEAD_DIM) + EPS) * w

    moz = moz_ref[...]
    ym = (head_norm(mhf_ref[...] + mhb_ref[...], nw_ref[0:1, :]) * jax.nn.sigmoid(moz[:, :GROUP_W])
          * jax.nn.silu(moz[:, GROUP_W:]))
    ao = jnp.concatenate([ao_ref[h] for h in range(N_HEADS)], axis=1)
    ya = head_norm(ao, nw_ref[1:2, :]) * (1.0 - lam_init) * jax.nn.silu(az_ref[...])
    yg = head_norm(ghf_ref[...] + ghb_ref[...], nw_ref[2:3, :]) * jax.nn.silu(gz_ref[...])
    yd = head_norm(dhf_ref[...] + dhb_ref[...], nw_ref[3:4, :]) * jax.nn.silu(dz_ref[...])
    y = jnp.concatenate([ym, ya, yg, yd], axis=1).astype(MXU_DT)
    o_ref[...] = x_ref[...] + gt_ref[...] * jnp.dot(y, wout_ref[...], preferred_element_type=F32)


def _finish(xs, mh, m_oz, ao, a_vz, gh, g_vz, dh, d_z, nw, ind64, wout, gtsel, lam_init, n_ctx_tiles):
    b, t, d = xs.shape
    tm = TOKEN_TILE
    tok = lambda w, c=0: pl.BlockSpec((None, tm, w), lambda bi, i: (bi, i, c))
    g = tok(GROUP_W)
    full = lambda a: pl.BlockSpec(a.shape, lambda bi, i: (0,) * a.ndim)
    return pl.pallas_call(
        functools.partial(_finish_body, lam_init=lam_init),
        out_shape=jax.ShapeDtypeStruct((b, t, d), F32),
        grid=(b, t // tm),
        in_specs=[tok(d), g, g, tok(2 * GROUP_W),
                  pl.BlockSpec((None, N_HEADS, tm, HEAD_DIM), lambda bi, i: (bi, 0, i, 0)),
                  tok(GROUP_W, 1), g, g, tok(GROUP_W, 1), g, g, g,
                  full(nw), full(ind64), full(wout),
                  pl.BlockSpec((None, None, 1, d), lambda bi, i: (bi, (i >= n_ctx_tiles).astype(jnp.int32), 0, 0))],
        out_specs=tok(d),
        compiler_params=_params("parallel", "parallel"),
        name="finish",
    )(xs, mh[0], mh[1], m_oz, ao, a_vz, gh[0], gh[1], g_vz, dh[0], dh[1], d_z, nw, ind64, wout, gtsel)


def kernel(x, c, ctx, c_ctx, norm_w, w_mod, b_mod, w_in, w_out, mlstm_b_i, mlstm_b_f, mlstm_norm, diff_q_norm,
           diff_k_norm, diff_lambda, diff_norm, gla_w_up, gla_b, gla_norm, gdn_conv, gdn_a_log, gdn_dt_bias,
           gdn_norm):
    bsz, n_lat, d = x.shape
    n_ctx = ctx.shape[1]
    depth = w_in.shape[0]
    t = n_ctx + n_lat
    assert n_ctx % TOKEN_TILE == 0 and n_lat % TOKEN_TILE == 0 and n_lat % GRID_W == 0
    assert n_lat % ATTN_TQ == 0 and t % ATTN_TK == 0 and n_ctx % STEP == 0
    n_ctx_tiles = n_ctx // TOKEN_TILE
    n_ctx_steps = n_ctx // STEP

    cc = jnp.concatenate([c, c_ctx[None, :], jnp.zeros((8 - (bsz + 1) % 8, d), F32)], axis=0)
    mod = _modulation(cc, w_mod, b_mod)

    wide = _wide_consts()
    gla_consts = _gla_consts()
    ind64 = _group_ones(GROUP_W, HEAD_DIM)
    ind32 = _group_ones(GROUP_W, DA_QK)
    tables = _rope_tables(n_ctx, n_lat)
    pad = lambda a, n: jnp.concatenate([a, jnp.zeros(a.shape[:-1] + (n - a.shape[-1],), a.dtype)], axis=-1)

    xs = jnp.concatenate([ctx, x], axis=1)
    for l in range(depth):
        lam_init = 0.8 - 0.6 * math.exp(-0.3 * l)
        sh, sc, gt = jnp.split(mod[l], 3, axis=-1)
        lat = jnp.stack([sh[:bsz], sc[:bsz]], axis=1)
        cx = jnp.broadcast_to(jnp.stack([sh[bsz], sc[bsz]], axis=0)[None], (bsz, 2, d))
        modsel = jnp.stack([cx, lat], axis=1)
        gtsel = jnp.stack([jnp.broadcast_to(gt[bsz][None], (bsz, d)), gt[:bsz]], axis=1)[:, :, None, :]
        wp, wg = _permute_w_in(w_in[l])
        pm = pad(jnp.concatenate([mlstm_b_i[l].reshape(1, 8), mlstm_b_f[l].reshape(1, 8)], axis=1), 128)
        pm = jnp.concatenate([pm, jnp.zeros((7, 128), F32)], axis=0)
        z8 = jnp.zeros((1, 8), F32)
        pd = jnp.concatenate([pad(jnp.concatenate([z8, gdn_a_log[l].reshape(1, 8)], axis=1), 128),
                              pad(jnp.concatenate([z8, gdn_dt_bias[l].reshape(1, 8)], axis=1), 128),
                              jnp.zeros((6, 128), F32)], axis=0)
        wup = jnp.zeros((128, 256), F32)
        wup = wup.at[0:GLA_RANK, 0:128].set(gla_w_up[l, 0]).at[GLA_RANK:2 * GLA_RANK, 128:256].set(gla_w_up[l, 1])
        bgk = gla_b[l].reshape(1, 256)

        (m_qkv, m_oz, m_g, m_gT, a_qk, a_vz, g_qk, g_vz, g_la, d_qkv, d_z, d_g, d_gT) = _inproj(
            xs, norm_w[l][None, :], modsel, wp, wg, pm, pd, wup.astype(MXU_DT), bgk, n_ctx_tiles)

        mh = _mlstm(m_qkv, m_g, m_gT, wide, ind64, n_ctx_steps)

        qkn = jnp.stack([jnp.tile(diff_q_norm[l], GROUP_W // DA_QK), jnp.tile(diff_k_norm[l], GROUP_W // DA_QK)])
        aqt, ak, avt = _diff_prep(a_qk, a_vz, tables, qkn, ind32)
        lam = pad(diff_lambda[l], 128)
        ao_ctx = _diff_attn(lam, aqt[:, :, :n_ctx], ak, avt, lam_init, n_ctx, n_ctx, n_ctx)
        ao_lat = _diff_attn(lam, aqt[:, :, n_ctx:], ak, avt, lam_init, t, ATTN_TQ, ATTN_TK)
        ao = jnp.concatenate([ao_ctx, ao_lat], axis=2)

        gh = _gla(g_qk, g_vz, g_la, wide, gla_consts, n_ctx_steps)

        dq = _gdn_prep(d_qkv, gdn_conv[l], ind64, n_ctx_tiles)
        dh = _gdn(dq, d_g, d_gT, wide, n_ctx_steps)

        nws = jnp.stack([mlstm_norm[l], diff_norm[l], gla_norm[l], gdn_norm[l]])
        xs = _finish(xs, mh, m_oz, ao, a_vz, gh, g_vz, dh, d_z, nws, ind64, w_out[l].astype(MXU_DT),
                     gtsel, lam_init, n_ctx_tiles)
    return xs[:, n_ctx:]
```

```python
import functools
import math

import jax
import jax.numpy as jnp
import numpy as np
from jax import lax
from jax.experimental import pallas as pl
from jax.experimental.pallas import tpu as pltpu

F32 = jnp.float32
MXU_DT = jnp.bfloat16
EPS = 1e-6
N_HEADS = 4
HEAD_DIM = 64
GROUP_W = N_HEADS * HEAD_DIM
CHUNK = 64
STEP = 2 * CHUNK
DA_QK = 32
GLA_DK = 32
GLA_RANK = 16
GLA_TAU = 16.0
GDN_CONV = 5
GRID_W = 64
ROPE_BASE = 10000.0
TOKEN_TILE = 256
ATTN_TQ = 1024
ATTN_TK = 768
V_ROWS = 80
VMEM_LIMIT = 48 * 1024 * 1024
NEG_BIG = -1e30


def _mm(a, b):
    return jnp.dot(a.astype(MXU_DT), b.astype(MXU_DT), preferred_element_type=F32)


def _mm_nt(a, b):
    return lax.dot_general(a.astype(MXU_DT), b.astype(MXU_DT), (((1,), (1,)), ((), ())),
                           preferred_element_type=F32)


def _mm_tn(a, b):
    return lax.dot_general(a.astype(MXU_DT), b.astype(MXU_DT), (((0,), (0,)), ((), ())),
                           preferred_element_type=F32)


def _split2(x):
    hi = x.astype(MXU_DT)
    lo = (x - hi.astype(F32)).astype(MXU_DT)
    return hi, lo


def _mm_sel(sel, x):
    hi, lo = _split2(x)
    s = sel.astype(MXU_DT)
    return (jnp.dot(s, hi, preferred_element_type=F32) + jnp.dot(s, lo, preferred_element_type=F32))


def _mm_sel_r(x, sel):
    hi, lo = _split2(x)
    s = sel.astype(MXU_DT)
    return (jnp.dot(hi, s, preferred_element_type=F32) + jnp.dot(lo, s, preferred_element_type=F32))


def _mm_sel_nt(x, sel):
    hi, lo = _split2(x)
    s = sel.astype(MXU_DT)
    dn = (((1,), (1,)), ((), ()))
    return (lax.dot_general(hi, s, dn, preferred_element_type=F32)
            + lax.dot_general(lo, s, dn, preferred_element_type=F32))


def _group_ones(width, group):
    g = np.arange(width) // group
    return jnp.asarray((g[:, None] == g[None, :]).astype(np.float32))


def _params(*sem):
    return pltpu.CompilerParams(dimension_semantics=sem, vmem_limit_bytes=VMEM_LIMIT)


def _mod_body(cc_ref, w_ref, b_ref, o_ref):
    o_ref[...] = _mm(jax.nn.silu(cc_ref[...]), w_ref[...]) + b_ref[...]


def _modulation(cc, w_mod, b_mod):
    depth, d, d3 = w_mod.shape
    tn = 1024
    return pl.pallas_call(
        _mod_body,
        out_shape=jax.ShapeDtypeStruct((depth, cc.shape[0], d3), F32),
        grid=(depth, d3 // tn),
        in_specs=[pl.BlockSpec(cc.shape, lambda l, j: (0, 0)),
                  pl.BlockSpec((None, d, tn), lambda l, j: (l, 0, j)),
                  pl.BlockSpec((None, 1, tn), lambda l, j: (l, 0, j))],
        out_specs=pl.BlockSpec((None, cc.shape[0], tn), lambda l, j: (l, 0, j)),
        compiler_params=_params("parallel", "parallel"),
        name="modulation",
    )(cc, w_mod, b_mod.reshape(depth, 1, d3))


_SEG = dict(m_qkv=(0, 768), m_oz=(768, 1280), m_g=(1280, 1408), a_qk=(1408, 1920), a_vz=(1920, 2432),
            g_qk=(2432, 2688), g_vz=(2688, 3200), g_r=(3200, 3328), d_qkv=(3328, 4096), d_z=(4096, 4352),
            d_g=(4352, 4480))
P_PAD = 4480


def _permute_w_in(w):
    d = w.shape[0]
    z = lambda n: jnp.zeros((d, n), w.dtype)
    m0, a0, g0, d0 = 0, 1296, 2320, 3120
    cols = [w[:, m0:m0 + 1280], w[:, m0 + 1280:m0 + 1296], z(112),
            w[:, a0:a0 + 512], w[:, a0 + 512:a0 + 1024],
            w[:, g0:g0 + 256], w[:, g0 + 256:g0 + 768], w[:, g0 + 768:g0 + 800], z(96),
            w[:, d0:d0 + 768], w[:, d0 + 768:d0 + 1024], w[:, d0 + 1024:d0 + 1040], z(112)]
    wp = jnp.concatenate(cols, axis=1)
    wg = jnp.concatenate([wp[:, 1280:1408], wp[:, 4352:4480]], axis=1).T
    return wp.astype(MXU_DT), wg.astype(MXU_DT)


def _gate_fns(xm, xd, pm, pd, lane):
    gm = jnp.where(lane < 8, xm + pm[0], jax.nn.log_sigmoid(xm + pm[0]))
    gd = jnp.where(lane < 8, jax.nn.sigmoid(xd), -jnp.exp(pd[0]) * jax.nn.softplus(xd + pd[1]))
    return gm, gd


def _inproj_body(x_ref, nw_ref, mod_ref, wp_ref, wg_ref, pm_ref, pd_ref, pmT_ref, pdT_ref, wup_ref, bgk_ref,
                 m_qkv, m_oz, m_g, m_gT, a_qk, a_vz, g_qk, g_vz, g_la, d_qkv, d_z, d_g, d_gT):
    x = x_ref[...]
    h = x * lax.rsqrt(jnp.mean(x * x, axis=-1, keepdims=True) + EPS) * nw_ref[...]
    h = (h * (1.0 + mod_ref[1:2, :]) + mod_ref[0:1, :]).astype(MXU_DT)
    u = jnp.dot(h, wp_ref[...], preferred_element_type=F32)
    seg = lambda name: u[:, _SEG[name][0]:_SEG[name][1]]
    m_qkv[...] = seg("m_qkv")
    m_oz[...] = seg("m_oz")
    a_qk[...] = seg("a_qk")
    a_vz[...] = seg("a_vz")
    g_qk[...] = seg("g_qk")
    g_vz[...] = seg("g_vz")
    d_qkv[...] = seg("d_qkv")
    d_z[...] = seg("d_z")
    lane = lax.broadcasted_iota(jnp.int32, (1, 128), 1)
    gm, gd = _gate_fns(seg("m_g"), seg("d_g"), (pm_ref[0:1, :],), (pd_ref[0:1, :], pd_ref[1:2, :]), lane)
    m_g[...] = gm
    d_g[...] = gd
    ut = lax.dot_general(wg_ref[...], h, (((1,), (1,)), ((), ())), preferred_element_type=F32)
    row = lax.broadcasted_iota(jnp.int32, (128, 1), 0)
    gmt, gdt = _gate_fns(ut[0:128], ut[128:256], (pmT_ref[:, 0:1],), (pdT_ref[:, 0:1], pdT_ref[:, 1:2]), row)
    m_gT[...] = gmt
    d_gT[...] = gdt
    gk = _mm(seg("g_r"), wup_ref[...]) + bgk_ref[...]
    g_la[...] = jax.nn.log_sigmoid(gk) / GLA_TAU


def _inproj(xs, nw, modsel, wp, wg, pm, pd, wup, bgk, n_ctx_tiles):
    b, t, d = xs.shape
    tm = TOKEN_TILE
    tok = lambda w: pl.BlockSpec((None, tm, w), lambda bi, i: (bi, i, 0))
    tokT = pl.BlockSpec((None, 128, tm), lambda bi, i: (bi, 0, i))
    full = lambda a: pl.BlockSpec(a.shape, lambda bi, i: (0,) * a.ndim)
    widths = [768, 512, 128, None, 512, 512, 256, 512, 256, 768, 256, 128, None]
    out_shape, out_specs = [], []
    for w in widths:
        if w is None:
            out_shape.append(jax.ShapeDtypeStruct((b, 128, t), F32))
            out_specs.append(tokT)
        else:
            out_shape.append(jax.ShapeDtypeStruct((b, t, w), F32))
            out_specs.append(tok(w))
    pmT, pdT = pm.T, pd.T
    return pl.pallas_call(
        _inproj_body,
        out_shape=out_shape,
        grid=(b, t // tm),
        in_specs=[tok(d), full(nw),
                  pl.BlockSpec((None, None, 2, d), lambda bi, i: (bi, (i >= n_ctx_tiles).astype(jnp.int32), 0, 0)),
                  full(wp), full(wg), full(pm), full(pd), full(pmT), full(pdT), full(wup), full(bgk)],
        out_specs=out_specs,
        compiler_params=_params("parallel", "parallel"),
        name="inproj",
    )(xs, nw, modsel, wp, wg, pm, pd, pmT, pdT, wup, bgk)


def _cum_mats_np():
    i = np.arange(CHUNK)
    fwd = (i[None, :] <= i[:, None]).astype(np.float32)
    return np.stack([fwd, fwd.T])


def _step_block(d, n, n_ctx_steps, n_steps):
    bwd = jnp.where(n < n_ctx_steps, n_ctx_steps - 1 - n, n_steps - 1 - (n - n_ctx_steps))
    return jnp.where(d == 0, n, bwd)


def _head_rowmax(x):
    return jnp.concatenate(
        [jnp.broadcast_to(jnp.max(x[:, h * HEAD_DIM:(h + 1) * HEAD_DIM], axis=1, keepdims=True), (CHUNK, HEAD_DIM))
         for h in range(N_HEADS)], axis=1)


def _mlstm_body(qkv_f, qkv_b, g_f, g_b, gT_f, gT_b, cm2_ref, cmw_ref, bdm_ref, elo_ref, ehi_ref, ind_ref,
                o_f, o_b, cn_ref, m_ref):
    @pl.when(pl.program_id(0) == 0)
    def _():
        cn_ref[...] = jnp.zeros_like(cn_ref)
        m_ref[...] = jnp.zeros_like(m_ref)

    nb = qkv_f.shape[0]
    streams = [(b, d) for b in range(nb) for d in range(2)]
    dirs = [d for _, d in streams]
    bdm = bdm_ref[...]
    bdm2 = jnp.concatenate([bdm, bdm], axis=1).astype(F32)
    ind = ind_ref[...].astype(MXU_DT)
    ones = jnp.ones((CHUNK, GROUP_W), F32)
    dup = lambda a: jnp.concatenate([a, a], axis=1)
    cm2 = [cm2_ref[d] for d in dirs]
    cmw = [cmw_ref[d] for d in dirs]
    elo = [elo_ref[d] for d in dirs]
    ehi = [ehi_ref[d] for d in dirs]
    g_all = [(g_f if d == 0 else g_b)[b] for b, d in streams]
    gt_all = [(gT_f if d == 0 else gT_b)[b] for b, d in streams]
    bc_all = _each(_mm_sel, cm2, g_all)
    br_all = _each(_mm_sel_nt, gt_all, cm2)
    bw_all = _each(_mm_sel_r, bc_all, ehi)
    iw_all = _each(_mm_sel_r, g_all, elo)
    for j in range(2):
        offs = [(j if d == 0 else 1 - j) * CHUNK for d in dirs]
        qkv = [(qkv_f if d == 0 else qkv_b)[b, o:o + CHUNK, :] for (b, d), o in zip(streams, offs)]
        gt = [x[:, o:o + CHUNK] for x, o in zip(gt_all, offs)]
        br = [x[:, o:o + CHUNK] for x, o in zip(br_all, offs)]
        bw = [x[o:o + CHUNK] for x, o in zip(bw_all, offs)]
        iw = [x[o:o + CHUNK] for x, o in zip(iw_all, offs)]
        rw = [jnp.concatenate([a[4 * d + h:4 * d + h + 1, :] - c[8 + 4 * d + h:9 + 4 * d + h, :]
                               for h in range(N_HEADS)], axis=1) for a, c, d in zip(gt, br, dirs)]
        dmat = [jnp.where(m > 0, a + r, -jnp.inf) for m, a, r in zip(cmw, bw, rw)]
        m_prev = [m_ref[i] for i in range(len(streams))]
        inter = _each(lambda a, m: a + m, bw, m_prev)
        m_t = [jnp.maximum(a, _head_rowmax(x)) for a, x in zip(inter, dmat)]
        q = [x[:, :GROUP_W] for x in qkv]
        k = [x[:, GROUP_W:2 * GROUP_W] * HEAD_DIM ** -0.5 for x in qkv]
        v = [x[:, 2 * GROUP_W:] for x in qkv]
        bdk = [jnp.concatenate([x.T.astype(MXU_DT)] * N_HEADS, axis=1) * bdm for x in k]
        s = [_mm(qq, w) * jnp.exp(x - m) for qq, w, x, m in zip(q, bdk, dmat, m_t)]
        w_inter = [jnp.exp(a - m) for a, m in zip(inter, m_t)]
        cn = [cn_ref[i] for i in range(len(streams))]
        acc = [dup(wi) * _mm(qq, c) + _mm(ss, jnp.concatenate([_bd_rows(vv, bdm), ind], axis=1))
               for wi, qq, c, ss, vv in zip(w_inter, q, cn, s, v)]
        hout = [a[:, :GROUP_W] / jnp.maximum(jnp.abs(a[:, GROUP_W:]), jnp.exp(-m)) for a, m in zip(acc, m_t)]
        last = [CHUNK - 1 if d == 0 else 0 for d in dirs]
        b_last = [a[r:r + 1, :] for a, r in zip(bw, last)]
        ds = [bl - a + i_ for bl, a, i_ in zip(b_last, bw, iw)]
        m_new = [jnp.maximum(bl + m, jnp.max(x, axis=0, keepdims=True)) for bl, m, x in zip(b_last, m_prev, ds)]
        for i, ((b, d), off) in enumerate(zip(streams, offs)):
            (o_f if d == 0 else o_b)[b, off:off + CHUNK, :] = hout[i]
            w_s = jnp.exp(ds[i] - m_new[i])
            w_c = jnp.exp(b_last[i] + m_prev[i] - m_new[i])
            upd = _mm_tn(k[i] * w_s, jnp.concatenate([v[i], ones], axis=1))
            cn_ref[i] = dup(w_c) * cn[i] + bdm2 * upd
            m_ref[i] = m_new[i]


def _scan_specs(b, n_ctx_steps, n_steps):
    fwd = lambda n: n
    bwd = lambda n: _step_block(1, n, n_ctx_steps, n_steps)
    tok = lambda w, f, c=0: pl.BlockSpec((b, STEP, w), lambda n: (0, f(n), c))
    tokT = lambda f: pl.BlockSpec((b, 128, STEP), lambda n: (0, 0, f(n)))
    full = lambda a: pl.BlockSpec(a.shape, lambda n: (0,) * a.ndim)
    return fwd, bwd, tok, tokT, full


def _mlstm(m_qkv, m_g, m_gT, wc, ind64, n_ctx_steps):
    b, t, _ = m_qkv.shape
    n_steps = t // STEP
    fwd, bwd, tok, tokT, full = _scan_specs(b, n_ctx_steps, n_steps)
    consts = [wc[k] for k in ("cm2", "cmw", "bdm", "e_lo", "e_hi")] + [ind64]
    out = jax.ShapeDtypeStruct((b, t, GROUP_W), F32)
    return pl.pallas_call(
        _mlstm_body,
        out_shape=[out, out],
        grid=(n_steps,),
        in_specs=[tok(3 * GROUP_W, fwd), tok(3 * GROUP_W, bwd), tok(128, fwd), tok(128, bwd), tokT(fwd), tokT(bwd)]
                 + [full(a) for a in consts],
        out_specs=[tok(GROUP_W, fwd), tok(GROUP_W, bwd)],
        scratch_shapes=[pltpu.VMEM((2 * b, GROUP_W, 2 * GROUP_W), F32), pltpu.VMEM((2 * b, 1, GROUP_W), F32)],
        compiler_params=_params("arbitrary"),
        name="mlstm",
    )(m_qkv, m_qkv, m_g, m_g, m_gT, m_gT, *consts)


def _gla_level_mats_np():
    L = CHUNK
    t = np.arange(L)
    sels, valids, pairs = [], [], []
    for rev in (False, True):
        pos_all = t if not rev else L - 1 - t
        eqs, eks, vq, vk, pm = [], [], [], [], []
        for j in range(6):
            w = 32 >> j
            par = pos_all // (2 * w)
            pos = pos_all % (2 * w)
            same = par[:, None] == par[None, :]
            late = pos >= w
            eq = same & late[:, None] & late[None, :] & (pos[None, :] <= pos[:, None])
            ek = same & (~late)[:, None] & (~late)[None, :] & (pos[None, :] > pos[:, None])
            eqs.append(eq)
            eks.append(ek)
            vq.append(late)
            vk.append(~late)
            pm.append(same & late[:, None] & (~late)[None, :])
        pm.append(np.eye(L, dtype=bool))
        sels.append(np.concatenate(eqs + eks, axis=0))
        valids.append(np.concatenate(vq + vk)[:, None])
        pairs.append(np.concatenate(pm, axis=0))
    f = lambda a: np.stack(a).astype(np.float32)
    return f(sels), f(valids), f(pairs)


def _gla_consts():
    sel, val, pair = _gla_level_mats_np()
    nq = 6 * CHUNK
    tile = lambda a: np.concatenate([a] * N_HEADS, axis=-1)
    selq, selk = sel[:, :nq], sel[:, nq:]
    selkt = np.stack([np.concatenate([selk[d, lv * CHUNK:(lv + 1) * CHUNK].T for lv in range(6)], axis=1)
                      for d in range(2)])
    valkt = np.stack([val[d, nq:, 0][None, :] for d in range(2)])
    rh = np.arange(N_HEADS * GLA_DK) // GLA_DK
    ch = np.arange(GROUP_W) // HEAD_DIM
    bdg = (rh[:, None] == ch[None, :]).astype(np.float32)
    return dict(selq=jnp.asarray(selq), valq=jnp.asarray(val[:, :nq]), selkt=jnp.asarray(selkt),
                valkt=jnp.asarray(valkt), pairw=jnp.asarray(tile(pair)), bdg=jnp.asarray(bdg, MXU_DT))


def _gla_body(qk_f, qk_b, v_f, v_b, la_f, la_b, cm_ref, selq_ref, valq_ref, selkt_ref, valkt_ref, pairw_ref,
              bdg_ref, bdm_ref, o_f, o_b, s_ref):
    @pl.when(pl.program_id(0) == 0)
    def _():
        s_ref[...] = jnp.zeros_like(s_ref)

    nb = qk_f.shape[0]
    streams = [(b, d) for b in range(nb) for d in range(2)]
    dirs = [d for _, d in streams]
    nk = N_HEADS * GLA_DK
    bdg = bdg_ref[...]
    bdm = bdm_ref[...]
    bdk = lambda xt: jnp.concatenate([xt.astype(MXU_DT)] * N_HEADS, axis=1) * bdg
    cm = [cm_ref[d] for d in dirs]
    selq = [selq_ref[d] for d in dirs]
    valq = [valq_ref[d] for d in dirs]
    selkt = [selkt_ref[d] for d in dirs]
    valkt = [valkt_ref[d] for d in dirs]
    pairw = [[pairw_ref[d, lv * CHUNK:(lv + 1) * CHUNK, :] for d in dirs] for lv in range(7)]
    ns = len(streams)
    items = [(si, j) for j in range(2) for si in range(ns)]
    idir = [dirs[si] for si, _ in items]
    offs = [(j if dirs[si] == 0 else 1 - j) * CHUNK for si, j in items]
    per = lambda lst: [lst[si] for si, _ in items]
    src = [(qk_f, v_f, la_f) if d == 0 else (qk_b, v_b, la_b) for d in idir]
    bidx = [streams[si][0] for si, _ in items]
    qk = [r[0][b, o:o + CHUNK, :] for b, r, o in zip(bidx, src, offs)]
    v = [r[1][b, o:o + CHUNK, :] for b, r, o in zip(bidx, src, offs)]
    la = [r[2][b, o:o + CHUNK, d * nk:(d + 1) * nk] for b, d, r, o in zip(bidx, idir, src, offs)]
    q = [x[:, :nk] * GLA_DK ** -0.5 for x in qk]
    k = [x[:, nk:] for x in qk]
    lat = [x.T for x in la]
    kt = [x.T for x in k]
    bcum = _each(_mm_sel, per(cm), la)
    exq = [jnp.exp(_mm_sel(s, x)) * vq for s, x, vq in zip(per(selq), la, per(valq))]
    exk = [jnp.exp(_mm_sel_r(x, s)) * vk for s, x, vk in zip(per(selkt), lat, per(valkt))]
    a = [p * _mm(qq, bdk(x)) for p, qq, x in zip(per(pairw[6]), q, kt)]
    for lv in range(6):
        sl = slice(lv * CHUNK, (lv + 1) * CHUNK)
        a = [aa + p * _mm(qq * eq[sl], bdk(x * ek[:, sl]))
             for aa, p, qq, eq, x, ek in zip(a, per(pairw[lv]), q, exq, kt, exk)]
    q_in = [qq * jnp.exp(bc) for qq, bc in zip(q, bcum)]
    b_last = [x[(CHUNK - 1 if d == 0 else 0):(CHUNK if d == 0 else 1), :] for x, d in zip(bcum, idir)]
    upd = [bdg.astype(F32) * _mm_tn(kk * jnp.exp(bl - bc), vv) for kk, bl, bc, vv in zip(k, b_last, bcum, v)]
    decay_t = [jnp.exp(jnp.sum(x, axis=1, keepdims=True)) for x in lat]
    st = [s_ref[si] for si in range(ns)]
    for j in range(2):
        cur = range(j * ns, (j + 1) * ns)
        o = [_mm(jnp.concatenate([q_in[i], a[i]], axis=1),
                 jnp.concatenate([s.astype(MXU_DT), _bd_rows(v[i], bdm)], axis=0)) for i, s in zip(cur, st)]
        for i, (b, d), oo in zip(cur, streams, o):
            (o_f if d == 0 else o_b)[b, offs[i]:offs[i] + CHUNK, :] = oo
        st = [decay_t[i] * s + upd[i] for i, s in zip(cur, st)]
    for si in range(ns):
        s_ref[si] = st[si]


def _gla(g_qk, g_vz, g_la, wc, gc, n_ctx_steps):
    b, t, _ = g_qk.shape
    n_steps = t // STEP
    fwd, bwd, tok, tokT, full = _scan_specs(b, n_ctx_steps, n_steps)
    consts = [wc["cm"]] + [gc[k] for k in ("selq", "valq", "selkt", "valkt", "pairw", "bdg")] + [wc["bdm"]]
    out = jax.ShapeDtypeStruct((b, t, GROUP_W), F32)
    w = 2 * N_HEADS * GLA_DK
    return pl.pallas_call(
        _gla_body,
        out_shape=[out, out],
        grid=(n_steps,),
        in_specs=[tok(w, fwd), tok(w, bwd), tok(GROUP_W, fwd), tok(GROUP_W, bwd),
                  tok(w, fwd), tok(w, bwd)] + [full(a) for a in consts],
        out_specs=[tok(GROUP_W, fwd), tok(GROUP_W, bwd)],
        scratch_shapes=[pltpu.VMEM((2 * b, N_HEADS * GLA_DK, GROUP_W), F32)],
        compiler_params=_params("arbitrary"),
        name="gla",
    )(g_qk, g_qk, g_vz, g_vz, g_la, g_la, *consts)


def _gdn_prep_body(prev_ref, cur_ref, next_ref, w_ref, ind_ref, o_ref, ext_ref, *, n_ctx_tiles, n_tiles):
    i = pl.program_id(1)
    tm = cur_ref.shape[0]
    first = (i == 0) | (i == n_ctx_tiles)
    last = (i == n_ctx_tiles - 1) | (i == n_tiles - 1)
    ext_ref[0:8, :] = jnp.where(first, 0.0, prev_ref[...])
    ext_ref[8:8 + tm, :] = cur_ref[...]
    ext_ref[8 + tm:16 + tm, :] = jnp.where(last, 0.0, next_ref[...])
    half = GDN_CONV // 2
    y = jnp.zeros(cur_ref.shape, F32)
    for j in range(GDN_CONV):
        y = y + ext_ref[8 - half + j:8 - half + j + tm, :] * w_ref[j:j + 1, :]
    y = jax.nn.silu(y)
    ind = ind_ref[...]
    q, k = y[:, :GROUP_W], y[:, GROUP_W:2 * GROUP_W]
    qn = q * lax.rsqrt(_mm_sel_r(q * q, ind) + EPS) * HEAD_DIM ** -0.5
    kn = k * lax.rsqrt(_mm_sel_r(k * k, ind) + EPS)
    o_ref[:, :GROUP_W] = qn
    o_ref[:, GROUP_W:2 * GROUP_W] = kn
    o_ref[:, 2 * GROUP_W:] = y[:, 2 * GROUP_W:]


def _gdn_prep(d_qkv, conv_w, ind64, n_ctx_tiles):
    b, t, w = d_qkv.shape
    tm = TOKEN_TILE
    n_tiles = t // tm
    r = tm // 8
    n8 = t // 8
    return pl.pallas_call(
        functools.partial(_gdn_prep_body, n_ctx_tiles=n_ctx_tiles, n_tiles=n_tiles),
        out_shape=jax.ShapeDtypeStruct((b, t, w), F32),
        grid=(b, n_tiles),
        in_specs=[pl.BlockSpec((None, 8, w), lambda bi, i: (bi, jnp.maximum(i * r - 1, 0), 0)),
                  pl.BlockSpec((None, tm, w), lambda bi, i: (bi, i, 0)),
                  pl.BlockSpec((None, 8, w), lambda bi, i: (bi, jnp.minimum((i + 1) * r, n8 - 1), 0)),
                  pl.BlockSpec(conv_w.shape, lambda bi, i: (0, 0)),
                  pl.BlockSpec(ind64.shape, lambda bi, i: (0, 0))],
        out_specs=pl.BlockSpec((None, tm, w), lambda bi, i: (bi, i, 0)),
        scratch_shapes=[pltpu.VMEM((tm + 16, w), F32)],
        compiler_params=_params("parallel", "parallel"),
        name="gdn_prep",
    )(d_qkv, d_qkv, d_qkv, conv_w, ind64)


def _wide_consts():
    cm = _cum_mats_np()
    eye = np.eye(CHUNK, dtype=np.float32)
    pair = _gla_level_mats_np()[2].reshape(2, 7, CHUNK, CHUNK)[:, :6]
    tile = lambda a: np.concatenate([a] * N_HEADS, axis=-1)
    hb = np.arange(GROUP_W) // HEAD_DIM
    bdm = (hb[:, None] == hb[None, :]).astype(np.float32)
    e_lo = np.zeros((2, 128, GROUP_W), np.float32)
    e_hi = np.zeros((2, 128, GROUP_W), np.float32)
    for d in range(2):
        for h in range(N_HEADS):
            e_lo[d, d * 4 + h, h * HEAD_DIM:(h + 1) * HEAD_DIM] = 1.0
            e_hi[d, 8 + d * 4 + h, h * HEAD_DIM:(h + 1) * HEAD_DIM] = 1.0
    z = np.zeros_like(cm)
    cm2 = np.concatenate([np.concatenate([cm, z], axis=2), np.concatenate([z, cm], axis=2)], axis=1)
    return dict(cm=jnp.asarray(cm), cm2=jnp.asarray(cm2), cmw=jnp.asarray(tile(cm)), strictw=jnp.asarray(tile(cm - eye)),
                eyew=jnp.asarray(tile(eye)), pairw=jnp.asarray(tile(pair).reshape(2, 6 * CHUNK, GROUP_W)),
                bdm=jnp.asarray(bdm, MXU_DT), e_lo=jnp.asarray(e_lo), e_hi=jnp.asarray(e_hi))


def _bd_rows(x, bdm):
    xb = x.astype(MXU_DT)
    return jnp.concatenate([xb] * N_HEADS, axis=0) * bdm


def _each(f, *lists):
    return [f(*a) for a in zip(*lists)]


def _gdn_body(qkv_f, qkv_b, g_f, g_b, gT_f, gT_b, cm2_ref, cmw_ref, strictw_ref, eyew_ref, pairw_ref, bdm_ref,
              elo_ref, ehi_ref, o_f, o_b, s_ref):
    @pl.when(pl.program_id(0) == 0)
    def _():
        s_ref[...] = jnp.zeros_like(s_ref)

    nb = qkv_f.shape[0]
    streams = [(b, d) for b in range(nb) for d in range(2)]
    dirs = [d for _, d in streams]
    bdm = bdm_ref[...]
    bdm2 = jnp.concatenate([bdm, bdm], axis=1)
    eyew = eyew_ref[...]
    cm2 = [cm2_ref[d] for d in dirs]
    cmw = [cmw_ref[d] for d in dirs]
    strictw = [strictw_ref[d] for d in dirs]
    pairw = [[pairw_ref[d, lv * CHUNK:(lv + 1) * CHUNK, :] for d in dirs] for lv in range(6)]
    elo = [elo_ref[d] for d in dirs]
    ehi = [ehi_ref[d] for d in dirs]
    g_all = [(g_f if d == 0 else g_b)[b] for b, d in streams]
    gt_all = [(gT_f if d == 0 else gT_b)[b] for b, d in streams]
    gc_all = _each(_mm_sel, cm2, g_all)
    gr_all = _each(_mm_sel_nt, gt_all, cm2)
    beta_all = _each(_mm_sel_r, g_all, elo)
    gw_all = _each(_mm_sel_r, gc_all, ehi)
    ns = len(streams)
    items = [(si, j) for j in range(2) for si in range(ns)]
    idir = [dirs[si] for si, _ in items]
    offs = [(j if dirs[si] == 0 else 1 - j) * CHUNK for si, j in items]
    per = lambda lst: [lst[si] for si, _ in items]
    qkv = [(qkv_f if d == 0 else qkv_b)[streams[si][0], o:o + CHUNK, :] for (si, _), d, o in zip(items, idir, offs)]
    gr = [x[:, o:o + CHUNK] for x, o in zip(per(gr_all), offs)]
    beta = [x[o:o + CHUNK] for x, o in zip(per(beta_all), offs)]
    gw = [x[o:o + CHUNK] for x, o in zip(per(gw_all), offs)]
    grow = [jnp.concatenate([x[8 + 4 * d + h:9 + 4 * d + h, :] for h in range(N_HEADS)], axis=1)
            for x, d in zip(gr, idir)]
    decay = [jnp.exp(jnp.where(m > 0, a - r, -jnp.inf)) for m, a, r in zip(per(cmw), gw, grow)]
    q = [x[:, :GROUP_W] for x in qkv]
    k = [x[:, GROUP_W:2 * GROUP_W] for x in qkv]
    v = [x[:, 2 * GROUP_W:] for x in qkv]
    kb = _each(lambda a, b_: a * b_, k, beta)
    bdk = [jnp.concatenate([x.T.astype(MXU_DT)] * N_HEADS, axis=1) * bdm for x in k]
    gq = _each(lambda a, b_, w: _mm(jnp.concatenate([a, b_], axis=0), w), kb, q, bdk)
    lower = [s * x[:CHUNK] * dc for s, x, dc in zip(per(strictw), gq, decay)]
    attn = [x[CHUNK:] * dc for x, dc in zip(gq, decay)]
    tw = [eyew - lo * p for lo, p in zip(lower, per(pairw[5]))]
    for lv in range(4, -1, -1):
        y = [_mm(lo * p, _bd_rows(t, bdm)) for lo, p, t in zip(lower, per(pairw[lv]), tw)]
        tw = [t - _mm(t, _bd_rows(yy, bdm)) for t, yy in zip(tw, y)]
    rhs = [jnp.concatenate([vv * b_, kk * jnp.exp(a)], axis=1).astype(MXU_DT)
           for vv, b_, kk, a in zip(v, beta, kb, gw)]
    uw = [_mm(t, jnp.concatenate([r] * N_HEADS, axis=0) * bdm2) for t, r in zip(tw, rhs)]
    q_in = [qq * jnp.exp(a) for qq, a in zip(q, gw)]
    g_last = [a[(CHUNK - 1 if d == 0 else 0):(CHUNK if d == 0 else 1), :] for a, d in zip(gw, idir)]
    k_out = [kk * jnp.exp(gl - a) for kk, gl, a in zip(k, g_last, gw)]
    st = [s_ref[si] for si in range(ns)]
    for j in range(2):
        cur = range(j * ns, (j + 1) * ns)
        v_new = [uw[i][:, :GROUP_W] - _mm(uw[i][:, GROUP_W:], s) for i, s in zip(cur, st)]
        o = [_mm(jnp.concatenate([q_in[i], attn[i]], axis=1),
                 jnp.concatenate([s.astype(MXU_DT), _bd_rows(vn, bdm)], axis=0))
             for i, s, vn in zip(cur, st, v_new)]
        for i, (b, d), oo in zip(cur, streams, o):
            (o_f if d == 0 else o_b)[b, offs[i]:offs[i] + CHUNK, :] = oo
        st = [jnp.exp(g_last[i]) * s + bdm.astype(F32) * _mm_tn(k_out[i], vn) for i, s, vn in zip(cur, st, v_new)]
    for si in range(ns):
        s_ref[si] = st[si]


def _gdn(d_qkv, d_g, d_gT, wc, n_ctx_steps):
    b, t, _ = d_qkv.shape
    n_steps = t // STEP
    fwd, bwd, tok, tokT, full = _scan_specs(b, n_ctx_steps, n_steps)
    consts = [wc[k] for k in ("cm2", "cmw", "strictw", "eyew", "pairw", "bdm", "e_lo", "e_hi")]
    out = jax.ShapeDtypeStruct((b, t, GROUP_W), F32)
    return pl.pallas_call(
        _gdn_body,
        out_shape=[out, out],
        grid=(n_steps,),
        in_specs=[tok(3 * GROUP_W, fwd), tok(3 * GROUP_W, bwd), tok(128, fwd), tok(128, bwd), tokT(fwd), tokT(bwd)]
                 + [full(a) for a in consts],
        out_specs=[tok(GROUP_W, fwd), tok(GROUP_W, bwd)],
        scratch_shapes=[pltpu.VMEM((2 * b, GROUP_W, GROUP_W), F32)],
        compiler_params=_params("arbitrary"),
        name="gdn",
    )(d_qkv, d_qkv, d_g, d_g, d_gT, d_gT, *consts)


def _rope_tables(n_ctx, n_lat):
    rows = n_lat // GRID_W
    row = np.repeat(np.arange(rows), GRID_W).astype(np.float32)
    col = np.tile(np.arange(GRID_W), rows).astype(np.float32)
    half = DA_QK // 2
    inv = np.power(ROPE_BASE, -np.arange(0, half, 2, dtype=np.float32) / half).astype(np.float32)

    def tab(p):
        ang = p[:, None] * inv
        ang = np.concatenate([ang, ang], axis=-1)
        return np.cos(ang), np.sin(ang)

    cr, sr = tab(row)
    cc, sc = tab(col)
    cos = np.concatenate([cr, cc], -1)
    sin = np.concatenate([sr, sc], -1)
    cos = np.concatenate([np.ones((n_ctx, DA_QK), np.float32), cos], 0)
    sin = np.concatenate([np.zeros((n_ctx, DA_QK), np.float32), sin], 0)
    reps = GROUP_W // DA_QK
    cos, sin = np.tile(cos, (1, reps)), np.tile(sin, (1, reps))
    first = (np.arange(GROUP_W) % 16) < 8
    sin_a = np.where(first, -sin, 0.0)
    sin_b = np.where(first, 0.0, sin)
    return jnp.asarray(cos, F32), jnp.asarray(sin_a, F32), jnp.asarray(sin_b, F32)


def _diff_prep_body(qk_ref, v_ref, cos_ref, sa_ref, sb_ref, nw_ref, ind_ref, qt_out, k_out, vt_out):
    ind = ind_ref[...]
    cos, sa, sb = cos_ref[...], sa_ref[...], sb_ref[...]

    def norm_rope(x, w):
        xn = x * lax.rsqrt(_mm_sel_r(x * x, ind) * (1.0 / DA_QK) + EPS) * w
        return xn * cos + pltpu.roll(xn, GROUP_W - 8, 1) * sa + pltpu.roll(xn, 8, 1) * sb

    qk = qk_ref[...]
    q = norm_rope(qk[:, :GROUP_W], nw_ref[0:1, :]) * (DA_QK ** -0.5 * math.log2(math.e))
    k = norm_rope(qk[:, GROUP_W:], nw_ref[1:2, :])
    qt_out[...] = q.T.astype(qt_out.dtype)
    for g in range(GROUP_W // DA_QK):
        k_out[g] = k[:, g * DA_QK:(g + 1) * DA_QK].astype(k_out.dtype)
    vt = v_ref[...].T
    tm = vt.shape[1]
    tail = jnp.concatenate([jnp.ones((8, tm), F32), jnp.zeros((V_ROWS - HEAD_DIM - 8, tm), F32)], axis=0)
    for h in range(N_HEADS):
        vt_out[h] = jnp.concatenate([vt[h * HEAD_DIM:(h + 1) * HEAD_DIM], tail], axis=0).astype(vt_out.dtype)


def _diff_prep(a_qk, a_vz, tables, nw, ind32):
    b, t, _ = a_qk.shape
    tm = TOKEN_TILE
    cos, sa, sb = tables
    tab = pl.BlockSpec((tm, GROUP_W), lambda bi, i: (i, 0))
    ng = GROUP_W // DA_QK
    return pl.pallas_call(
        _diff_prep_body,
        out_shape=[jax.ShapeDtypeStruct((b, GROUP_W, t), MXU_DT),
                   jax.ShapeDtypeStruct((b, ng, t, DA_QK), MXU_DT),
                   jax.ShapeDtypeStruct((b, N_HEADS, V_ROWS, t), MXU_DT)],
        grid=(b, t // tm),
        in_specs=[pl.BlockSpec((None, tm, 2 * GROUP_W), lambda bi, i: (bi, i, 0)),
                  pl.BlockSpec((None, tm, GROUP_W), lambda bi, i: (bi, i, 0)),
                  tab, tab, tab,
                  pl.BlockSpec(nw.shape, lambda bi, i: (0, 0)),
                  pl.BlockSpec(ind32.shape, lambda bi, i: (0, 0))],
        out_specs=[pl.BlockSpec((None, GROUP_W, tm), lambda bi, i: (bi, 0, i)),
                   pl.BlockSpec((None, ng, tm, DA_QK), lambda bi, i: (bi, 0, i, 0)),
                   pl.BlockSpec((None, N_HEADS, V_ROWS, tm), lambda bi, i: (bi, 0, 0, i))],
        compiler_params=_params("parallel", "parallel"),
        name="diff_prep",
    )(a_qk, a_vz, cos, sa, sb, nw, ind32)


def _diff_attn_body(lam_ref, qt_ref, k_ref, vp_ref, vc_ref, o_ref, acc_ref, m_ref, e_ref, *, lam_init):
    j = pl.program_id(3)

    @pl.when(j == 0)
    def _():
        acc_ref[...] = jnp.zeros_like(acc_ref)
        m_ref[...] = jnp.full_like(m_ref, -jnp.inf)
        e_ref[...] = jnp.zeros_like(e_ref)

    vp = vp_ref[...]
    s = [jnp.dot(k_ref[p], qt_ref[p * DA_QK:(p + 1) * DA_QK, :], preferred_element_type=F32)
         for p in range(2)]
    pv = [jnp.dot(vp, e_ref[p], preferred_element_type=F32) for p in range(2)]
    for p in range(2):
        m_old = m_ref[p]
        m_new = jnp.maximum(m_old, jnp.max(s[p], axis=0, keepdims=True))
        e_ref[p] = jnp.exp2(s[p] - m_new).astype(e_ref.dtype)
        acc_ref[p] = jnp.exp2(m_old - m_new) * (acc_ref[p] + pv[p])
        m_ref[p] = m_new

    @pl.when(j == pl.num_programs(3) - 1)
    def _():
        lam = lam_ref[...]
        lam_full = (jnp.exp(jnp.sum(lam[0:1] * lam[1:2], keepdims=True))
                    - jnp.exp(jnp.sum(lam[2:3] * lam[3:4], keepdims=True)) + lam_init)
        vc = vc_ref[...]
        a0 = acc_ref[0] + jnp.dot(vc, e_ref[0], preferred_element_type=F32)
        a1 = acc_ref[1] + jnp.dot(vc, e_ref[1], preferred_element_type=F32)
        ot = (a0[:HEAD_DIM] / a0[HEAD_DIM:HEAD_DIM + 1]
              - lam_full * (a1[:HEAD_DIM] / a1[HEAD_DIM:HEAD_DIM + 1]))
        o_ref[...] = ot.T


def _diff_attn(lam, qt, k, vt, lam_init, n_keys, tq, tk):
    b, _, n_q = qt.shape
    return pl.pallas_call(
        functools.partial(_diff_attn_body, lam_init=lam_init),
        out_shape=jax.ShapeDtypeStruct((b, N_HEADS, n_q, HEAD_DIM), F32),
        grid=(b, N_HEADS, n_q // tq, n_keys // tk),
        in_specs=[pl.BlockSpec(lam.shape, lambda bi, h, i, j: (0, 0)),
                  pl.BlockSpec((None, 2 * DA_QK, tq), lambda bi, h, i, j: (bi, h, i)),
                  pl.BlockSpec((None, 2, tk, DA_QK), lambda bi, h, i, j: (bi, h, j, 0)),
                  pl.BlockSpec((None, None, V_ROWS, tk), lambda bi, h, i, j: (bi, h, 0, jnp.maximum(j - 1, 0))),
                  pl.BlockSpec((None, None, V_ROWS, tk), lambda bi, h, i, j: (bi, h, 0, j))],
        out_specs=pl.BlockSpec((None, None, tq, HEAD_DIM), lambda bi, h, i, j: (bi, h, i, 0)),
        scratch_shapes=[pltpu.VMEM((2, V_ROWS, tq), F32), pltpu.VMEM((2, 1, tq), F32),
                        pltpu.VMEM((2, tk, tq), MXU_DT)],
        compiler_params=_params("parallel", "parallel", "parallel", "arbitrary"),
        name="diff_attn",
    )(lam, qt, k, vt, vt)


def _finish_body(x_ref, mhf_ref, mhb_ref, moz_ref, ao_ref, az_ref, ghf_ref, ghb_ref, gz_ref, dhf_ref, dhb_ref, dz_ref,
                 nw_ref, ind_ref, wout_ref, gt_ref, o_ref, *, lam_init):
    ind = ind_ref[...]

    def head_norm(hh, w):
        return hh * lax.rsqrt(_mm_sel_r(hh * hh, ind) * (1.0 / HEAD_DIM) + EPS) * w

    moz = moz_ref[...]
    ym = (head_norm(mhf_ref[...] + mhb_ref[...], nw_ref[0:1, :]) * jax.nn.sigmoid(moz[:, :GROUP_W])
          * jax.nn.silu(moz[:, GROUP_W:]))
    ao = jnp.concatenate([ao_ref[h] for h in range(N_HEADS)], axis=1)
    ya = head_norm(ao, nw_ref[1:2, :]) * (1.0 - lam_init) * jax.nn.silu(az_ref[...])
    yg = head_norm(ghf_ref[...] + ghb_ref[...], nw_ref[2:3, :]) * jax.nn.silu(gz_ref[...])
    yd = head_norm(dhf_ref[...] + dhb_ref[...], nw_ref[3:4, :]) * jax.nn.silu(dz_ref[...])
    y = jnp.concatenate([ym, ya, yg, yd], axis=1).astype(MXU_DT)
    o_ref[...] = x_ref[...] + gt_ref[...] * jnp.dot(y, wout_ref[...], preferred_element_type=F32)


def _finish(xs, mh, m_oz, ao, a_vz, gh, g_vz, dh, d_z, nw, ind64, wout, gtsel, lam_init, n_ctx_tiles):
    b, t, d = xs.shape
    tm = TOKEN_TILE
    tok = lambda w, c=0: pl.BlockSpec((None, tm, w), lambda bi, i: (bi, i, c))
    g = tok(GROUP_W)
    full = lambda a: pl.BlockSpec(a.shape, lambda bi, i: (0,) * a.ndim)
    return pl.pallas_call(
        functools.partial(_finish_body, lam_init=lam_init),
        out_shape=jax.ShapeDtypeStruct((b, t, d), F32),
        grid=(b, t // tm),
        in_specs=[tok(d), g, g, tok(2 * GROUP_W),
                  pl.BlockSpec((None, N_HEADS, tm, HEAD_DIM), lambda bi, i: (bi, 0, i, 0)),
                  tok(GROUP_W, 1), g, g, tok(GROUP_W, 1), g, g, g,
                  full(nw), full(ind64), full(wout),
                  pl.BlockSpec((None, None, 1, d), lambda bi, i: (bi, (i >= n_ctx_tiles).astype(jnp.int32), 0, 0))],
        out_specs=tok(d),
        compiler_params=_params("parallel", "parallel"),
        name="finish",
    )(xs, mh[0], mh[1], m_oz, ao, a_vz, gh[0], gh[1], g_vz, dh[0], dh[1], d_z, nw, ind64, wout, gtsel)


def kernel(x, c, ctx, c_ctx, norm_w, w_mod, b_mod, w_in, w_out, mlstm_b_i, mlstm_b_f, mlstm_norm, diff_q_norm,
           diff_k_norm, diff_lambda, diff_norm, gla_w_up, gla_b, gla_norm, gdn_conv, gdn_a_log, gdn_dt_bias,
           gdn_norm):
    bsz, n_lat, d = x.shape
    n_ctx = ctx.shape[1]
    depth = w_in.shape[0]
    t = n_ctx + n_lat
    assert n_ctx % TOKEN_TILE == 0 and n_lat % TOKEN_TILE == 0 and n_lat % GRID_W == 0
    assert n_lat % ATTN_TQ == 0 and t % ATTN_TK == 0 and n_ctx % STEP == 0
    n_ctx_tiles = n_ctx // TOKEN_TILE
    n_ctx_steps = n_ctx // STEP

    cc = jnp.concatenate([c, c_ctx[None, :], jnp.zeros((8 - (bsz + 1) % 8, d), F32)], axis=0)
    mod = _modulation(cc, w_mod, b_mod)

    wide = _wide_consts()
    gla_consts = _gla_consts()
    ind64 = _group_ones(GROUP_W, HEAD_DIM)
    ind32 = _group_ones(GROUP_W, DA_QK)
    tables = _rope_tables(n_ctx, n_lat)
    pad = lambda a, n: jnp.concatenate([a, jnp.zeros(a.shape[:-1] + (n - a.shape[-1],), a.dtype)], axis=-1)

    xs = jnp.concatenate([ctx, x], axis=1)
    for l in range(depth):
        lam_init = 0.8 - 0.6 * math.exp(-0.3 * l)
        sh, sc, gt = jnp.split(mod[l], 3, axis=-1)
        lat = jnp.stack([sh[:bsz], sc[:bsz]], axis=1)
        cx = jnp.broadcast_to(jnp.stack([sh[bsz], sc[bsz]], axis=0)[None], (bsz, 2, d))
        modsel = jnp.stack([cx, lat], axis=1)
        gtsel = jnp.stack([jnp.broadcast_to(gt[bsz][None], (bsz, d)), gt[:bsz]], axis=1)[:, :, None, :]
        wp, wg = _permute_w_in(w_in[l])
        pm = pad(jnp.concatenate([mlstm_b_i[l].reshape(1, 8), mlstm_b_f[l].reshape(1, 8)], axis=1), 128)
        pm = jnp.concatenate([pm, jnp.zeros((7, 128), F32)], axis=0)
        z8 = jnp.zeros((1, 8), F32)
        pd = jnp.concatenate([pad(jnp.concatenate([z8, gdn_a_log[l].reshape(1, 8)], axis=1), 128),
                              pad(jnp.concatenate([z8, gdn_dt_bias[l].reshape(1, 8)], axis=1), 128),
                              jnp.zeros((6, 128), F32)], axis=0)
        wup = jnp.zeros((128, 256), F32)
        wup = wup.at[0:GLA_RANK, 0:128].set(gla_w_up[l, 0]).at[GLA_RANK:2 * GLA_RANK, 128:256].set(gla_w_up[l, 1])
        bgk = gla_b[l].reshape(1, 256)

        (m_qkv, m_oz, m_g, m_gT, a_qk, a_vz, g_qk, g_vz, g_la, d_qkv, d_z, d_g, d_gT) = _inproj(
            xs, norm_w[l][None, :], modsel, wp, wg, pm, pd, wup.astype(MXU_DT), bgk, n_ctx_tiles)

        mh = _mlstm(m_qkv, m_g, m_gT, wide, ind64, n_ctx_steps)

        qkn = jnp.stack([jnp.tile(diff_q_norm[l], GROUP_W // DA_QK), jnp.tile(diff_k_norm[l], GROUP_W // DA_QK)])
        aqt, ak, avt = _diff_prep(a_qk, a_vz, tables, qkn, ind32)
        lam = pad(diff_lambda[l], 128)
        ao_ctx = _diff_attn(lam, aqt[:, :, :n_ctx], ak, avt, lam_init, n_ctx, n_ctx, n_ctx)
        ao_lat = _diff_attn(lam, aqt[:, :, n_ctx:], ak, avt, lam_init, t, ATTN_TQ, ATTN_TK)
        ao = jnp.concatenate([ao_ctx, ao_lat], axis=2)

        gh = _gla(g_qk, g_vz, g_la, wide, gla_consts, n_ctx_steps)

        dq = _gdn_prep(d_qkv, gdn_conv[l], ind64, n_ctx_tiles)
        dh = _gdn(dq, d_g, d_gT, wide, n_ctx_steps)

        nws = jnp.stack([mlstm_norm[l], diff_norm[l], gla_norm[l], gdn_norm[l]])
        xs = _finish(xs, mh, m_oz, ao, a_vz, gh, g_vz, dh, d_z, nws, ind64, w_out[l].astype(MXU_DT),
                     gtsel, lam_init, n_ctx_tiles)
    return xs[:, n_ctx:]
```

```python
import functools
import math

import jax
import jax.numpy as jnp
import numpy as np
from jax import lax
from jax.experimental import pallas as pl
from jax.experimental.pallas import tpu as pltpu

F32 = jnp.float32
MXU_DT = jnp.bfloat16
EPS = 1e-6
N_HEADS = 4
HEAD_DIM = 64
GROUP_W = N_HEADS * HEAD_DIM
CHUNK = 64
STEP = 2 * CHUNK
DA_QK = 32
GLA_DK = 32
GLA_RANK = 16
GLA_TAU = 16.0
GDN_CONV = 5
GRID_W = 64
ROPE_BASE = 10000.0
TOKEN_TILE = 256
ATTN_TQ = 2048
ATTN_TK = 768
V_ROWS = 80
VMEM_LIMIT = 48 * 1024 * 1024
NEG_BIG = -1e30


def _mm(a, b):
    return jnp.dot(a.astype(MXU_DT), b.astype(MXU_DT), preferred_element_type=F32)


def _mm_nt(a, b):
    return lax.dot_general(a.astype(MXU_DT), b.astype(MXU_DT), (((1,), (1,)), ((), ())),
                           preferred_element_type=F32)


def _mm_tn(a, b):
    return lax.dot_general(a.astype(MXU_DT), b.astype(MXU_DT), (((0,), (0,)), ((), ())),
                           preferred_element_type=F32)


def _split2(x):
    hi = x.astype(MXU_DT)
    lo = (x - hi.astype(F32)).astype(MXU_DT)
    return hi, lo


def _mm_sel(sel, x):
    hi, lo = _split2(x)
    s = sel.astype(MXU_DT)
    return (jnp.dot(s, hi, preferred_element_type=F32) + jnp.dot(s, lo, preferred_element_type=F32))


def _mm_sel_r(x, sel):
    hi, lo = _split2(x)
    s = sel.astype(MXU_DT)
    return (jnp.dot(hi, s, preferred_element_type=F32) + jnp.dot(lo, s, preferred_element_type=F32))


def _mm_sel_nt(x, sel):
    hi, lo = _split2(x)
    s = sel.astype(MXU_DT)
    dn = (((1,), (1,)), ((), ()))
    return (lax.dot_general(hi, s, dn, preferred_element_type=F32)
            + lax.dot_general(lo, s, dn, preferred_element_type=F32))


def _group_ones(width, group):
    g = np.arange(width) // group
    return jnp.asarray((g[:, None] == g[None, :]).astype(np.float32))


def _params(*sem):
    return pltpu.CompilerParams(dimension_semantics=sem, vmem_limit_bytes=VMEM_LIMIT)


def _mod_body(cc_ref, w_ref, b_ref, o_ref):
    o_ref[...] = _mm(jax.nn.silu(cc_ref[...]), w_ref[...]) + b_ref[...]


def _modulation(cc, w_mod, b_mod):
    depth, d, d3 = w_mod.shape
    tn = 1024
    return pl.pallas_call(
        _mod_body,
        out_shape=jax.ShapeDtypeStruct((depth, cc.shape[0], d3), F32),
        grid=(depth, d3 // tn),
        in_specs=[pl.BlockSpec(cc.shape, lambda l, j: (0, 0)),
                  pl.BlockSpec((None, d, tn), lambda l, j: (l, 0, j)),
                  pl.BlockSpec((None, 1, tn), lambda l, j: (l, 0, j))],
        out_specs=pl.BlockSpec((None, cc.shape[0], tn), lambda l, j: (l, 0, j)),
        compiler_params=_params("parallel", "parallel"),
        name="modulation",
    )(cc, w_mod, b_mod.reshape(depth, 1, d3))


_SEG = dict(m_qkv=(0, 768), m_oz=(768, 1280), m_g=(1280, 1408), a_qk=(1408, 1920), a_vz=(1920, 2432),
            g_qk=(2432, 2688), g_vz=(2688, 3200), g_r=(3200, 3328), d_qkv=(3328, 4096), d_z=(4096, 4352),
            d_g=(4352, 4480))
P_PAD = 4480


def _permute_w_in(w):
    d = w.shape[0]
    z = lambda n: jnp.zeros((d, n), w.dtype)
    m0, a0, g0, d0 = 0, 1296, 2320, 3120
    cols = [w[:, m0:m0 + 1280], w[:, m0 + 1280:m0 + 1296], z(112),
            w[:, a0:a0 + 512], w[:, a0 + 512:a0 + 1024],
            w[:, g0:g0 + 256], w[:, g0 + 256:g0 + 768], w[:, g0 + 768:g0 + 800], z(96),
            w[:, d0:d0 + 768], w[:, d0 + 768:d0 + 1024], w[:, d0 + 1024:d0 + 1040], z(112)]
    wp = jnp.concatenate(cols, axis=1)
    wg = jnp.concatenate([wp[:, 1280:1408], wp[:, 4352:4480]], axis=1).T
    return wp.astype(MXU_DT), wg.astype(MXU_DT)


def _gate_fns(xm, xd, pm, pd, lane):
    gm = jnp.where(lane < 8, xm + pm[0], jax.nn.log_sigmoid(xm + pm[0]))
    gd = jnp.where(lane < 8, jax.nn.sigmoid(xd), -jnp.exp(pd[0]) * jax.nn.softplus(xd + pd[1]))
    return gm, gd


def _inproj_body(x_ref, nw_ref, mod_ref, wp_ref, wg_ref, pm_ref, pd_ref, pmT_ref, pdT_ref, wup_ref, bgk_ref,
                 m_qkv, m_oz, m_g, m_gT, a_qk, a_vz, g_qk, g_vz, g_la, d_qkv, d_z, d_g, d_gT):
    x = x_ref[...]
    h = x * lax.rsqrt(jnp.mean(x * x, axis=-1, keepdims=True) + EPS) * nw_ref[...]
    h = (h * (1.0 + mod_ref[1:2, :]) + mod_ref[0:1, :]).astype(MXU_DT)
    u = jnp.dot(h, wp_ref[...], preferred_element_type=F32)
    seg = lambda name: u[:, _SEG[name][0]:_SEG[name][1]]
    m_qkv[...] = seg("m_qkv")
    m_oz[...] = seg("m_oz")
    a_qk[...] = seg("a_qk")
    a_vz[...] = seg("a_vz")
    g_qk[...] = seg("g_qk")
    g_vz[...] = seg("g_vz")
    d_qkv[...] = seg("d_qkv")
    d_z[...] = seg("d_z")
    lane = lax.broadcasted_iota(jnp.int32, (1, 128), 1)
    gm, gd = _gate_fns(seg("m_g"), seg("d_g"), (pm_ref[0:1, :],), (pd_ref[0:1, :], pd_ref[1:2, :]), lane)
    m_g[...] = gm
    d_g[...] = gd
    ut = lax.dot_general(wg_ref[...], h, (((1,), (1,)), ((), ())), preferred_element_type=F32)
    row = lax.broadcasted_iota(jnp.int32, (128, 1), 0)
    gmt, gdt = _gate_fns(ut[0:128], ut[128:256], (pmT_ref[:, 0:1],), (pdT_ref[:, 0:1], pdT_ref[:, 1:2]), row)
    m_gT[...] = gmt
    d_gT[...] = gdt
    gk = _mm(seg("g_r"), wup_ref[...]) + bgk_ref[...]
    g_la[...] = jax.nn.log_sigmoid(gk) / GLA_TAU


def _inproj(xs, nw, modsel, wp, wg, pm, pd, wup, bgk, n_ctx_tiles):
    b, t, d = xs.shape
    tm = TOKEN_TILE
    tok = lambda w: pl.BlockSpec((None, tm, w), lambda bi, i: (bi, i, 0))
    tokT = pl.BlockSpec((None, 128, tm), lambda bi, i: (bi, 0, i))
    full = lambda a: pl.BlockSpec(a.shape, lambda bi, i: (0,) * a.ndim)
    widths = [768, 512, 128, None, 512, 512, 256, 512, 256, 768, 256, 128, None]
    out_shape, out_specs = [], []
    for w in widths:
        if w is None:
            out_shape.append(jax.ShapeDtypeStruct((b, 128, t), F32))
            out_specs.append(tokT)
        else:
            out_shape.append(jax.ShapeDtypeStruct((b, t, w), F32))
            out_specs.append(tok(w))
    pmT, pdT = pm.T, pd.T
    return pl.pallas_call(
        _inproj_body,
        out_shape=out_shape,
        grid=(b, t // tm),
        in_specs=[tok(d), full(nw),
                  pl.BlockSpec((None, None, 2, d), lambda bi, i: (bi, (i >= n_ctx_tiles).astype(jnp.int32), 0, 0)),
                  full(wp), full(wg), full(pm), full(pd), full(pmT), full(pdT), full(wup), full(bgk)],
        out_specs=out_specs,
        compiler_params=_params("parallel", "parallel"),
        name="inproj",
    )(xs, nw, modsel, wp, wg, pm, pd, pmT, pdT, wup, bgk)


def _cum_mats_np():
    i = np.arange(CHUNK)
    fwd = (i[None, :] <= i[:, None]).astype(np.float32)
    return np.stack([fwd, fwd.T])


def _step_block(d, n, n_ctx_steps, n_steps):
    bwd = jnp.where(n < n_ctx_steps, n_ctx_steps - 1 - n, n_steps - 1 - (n - n_ctx_steps))
    return jnp.where(d == 0, n, bwd)


def _head_rowmax(x):
    return jnp.concatenate(
        [jnp.broadcast_to(jnp.max(x[:, h * HEAD_DIM:(h + 1) * HEAD_DIM], axis=1, keepdims=True), (CHUNK, HEAD_DIM))
         for h in range(N_HEADS)], axis=1)


def _mlstm_body(qkv_f, qkv_b, g_f, g_b, gT_f, gT_b, cm2_ref, cmw_ref, bdm_ref, elo_ref, ehi_ref, ind_ref,
                o_f, o_b, cn_ref, m_ref):
    @pl.when(pl.program_id(0) == 0)
    def _():
        cn_ref[...] = jnp.zeros_like(cn_ref)
        m_ref[...] = jnp.zeros_like(m_ref)

    nb = qkv_f.shape[0]
    streams = [(b, d) for b in range(nb) for d in range(2)]
    dirs = [d for _, d in streams]
    bdm = bdm_ref[...]
    bdm2 = jnp.concatenate([bdm, bdm], axis=1).astype(F32)
    ind = ind_ref[...].astype(MXU_DT)
    ones = jnp.ones((CHUNK, GROUP_W), F32)
    dup = lambda a: jnp.concatenate([a, a], axis=1)
    cm2 = [cm2_ref[d] for d in dirs]
    cmw = [cmw_ref[d] for d in dirs]
    elo = [elo_ref[d] for d in dirs]
    ehi = [ehi_ref[d] for d in dirs]
    g_all = [(g_f if d == 0 else g_b)[b] for b, d in streams]
    gt_all = [(gT_f if d == 0 else gT_b)[b] for b, d in streams]
    bc_all = _each(_mm_sel, cm2, g_all)
    br_all = _each(_mm_sel_nt, gt_all, cm2)
    bw_all = _each(_mm_sel_r, bc_all, ehi)
    iw_all = _each(_mm_sel_r, g_all, elo)
    for j in range(2):
        offs = [(j if d == 0 else 1 - j) * CHUNK for d in dirs]
        qkv = [(qkv_f if d == 0 else qkv_b)[b, o:o + CHUNK, :] for (b, d), o in zip(streams, offs)]
        gt = [x[:, o:o + CHUNK] for x, o in zip(gt_all, offs)]
        br = [x[:, o:o + CHUNK] for x, o in zip(br_all, offs)]
        bw = [x[o:o + CHUNK] for x, o in zip(bw_all, offs)]
        iw = [x[o:o + CHUNK] for x, o in zip(iw_all, offs)]
        rw = [jnp.concatenate([a[4 * d + h:4 * d + h + 1, :] - c[8 + 4 * d + h:9 + 4 * d + h, :]
                               for h in range(N_HEADS)], axis=1) for a, c, d in zip(gt, br, dirs)]
        dmat = [jnp.where(m > 0, a + r, -jnp.inf) for m, a, r in zip(cmw, bw, rw)]
        m_prev = [m_ref[i] for i in range(len(streams))]
        inter = _each(lambda a, m: a + m, bw, m_prev)
        m_t = [jnp.maximum(a, _head_rowmax(x)) for a, x in zip(inter, dmat)]
        q = [x[:, :GROUP_W] for x in qkv]
        k = [x[:, GROUP_W:2 * GROUP_W] * HEAD_DIM ** -0.5 for x in qkv]
        v = [x[:, 2 * GROUP_W:] for x in qkv]
        bdk = [jnp.concatenate([x.T.astype(MXU_DT)] * N_HEADS, axis=1) * bdm for x in k]
        s = [_mm(qq, w) * jnp.exp(x - m) for qq, w, x, m in zip(q, bdk, dmat, m_t)]
        w_inter = [jnp.exp(a - m) for a, m in zip(inter, m_t)]
        cn = [cn_ref[i] for i in range(len(streams))]
        acc = [dup(wi) * _mm(qq, c) + _mm(ss, jnp.concatenate([_bd_rows(vv, bdm), ind], axis=1))
               for wi, qq, c, ss, vv in zip(w_inter, q, cn, s, v)]
        hout = [a[:, :GROUP_W] / jnp.maximum(jnp.abs(a[:, GROUP_W:]), jnp.exp(-m)) for a, m in zip(acc, m_t)]
        last = [CHUNK - 1 if d == 0 else 0 for d in dirs]
        b_last = [a[r:r + 1, :] for a, r in zip(bw, last)]
        ds = [bl - a + i_ for bl, a, i_ in zip(b_last, bw, iw)]
        m_new = [jnp.maximum(bl + m, jnp.max(x, axis=0, keepdims=True)) for bl, m, x in zip(b_last, m_prev, ds)]
        for i, ((b, d), off) in enumerate(zip(streams, offs)):
            (o_f if d == 0 else o_b)[b, off:off + CHUNK, :] = hout[i]
            w_s = jnp.exp(ds[i] - m_new[i])
            w_c = jnp.exp(b_last[i] + m_prev[i] - m_new[i])
            upd = _mm_tn(k[i] * w_s, jnp.concatenate([v[i], ones], axis=1))
            cn_ref[i] = dup(w_c) * cn[i] + bdm2 * upd
            m_ref[i] = m_new[i]


def _scan_specs(b, n_ctx_steps, n_steps):
    fwd = lambda n: n
    bwd = lambda n: _step_block(1, n, n_ctx_steps, n_steps)
    tok = lambda w, f, c=0: pl.BlockSpec((b, STEP, w), lambda n: (0, f(n), c))
    tokT = lambda f: pl.BlockSpec((b, 128, STEP), lambda n: (0, 0, f(n)))
    full = lambda a: pl.BlockSpec(a.shape, lambda n: (0,) * a.ndim)
    return fwd, bwd, tok, tokT, full


def _mlstm(m_qkv, m_g, m_gT, wc, ind64, n_ctx_steps):
    b, t, _ = m_qkv.shape
    n_steps = t // STEP
    fwd, bwd, tok, tokT, full = _scan_specs(b, n_ctx_steps, n_steps)
    consts = [wc[k] for k in ("cm2", "cmw", "bdm", "e_lo", "e_hi")] + [ind64]
    out = jax.ShapeDtypeStruct((b, t, GROUP_W), F32)
    return pl.pallas_call(
        _mlstm_body,
        out_shape=[out, out],
        grid=(n_steps,),
        in_specs=[tok(3 * GROUP_W, fwd), tok(3 * GROUP_W, bwd), tok(128, fwd), tok(128, bwd), tokT(fwd), tokT(bwd)]
                 + [full(a) for a in consts],
        out_specs=[tok(GROUP_W, fwd), tok(GROUP_W, bwd)],
        scratch_shapes=[pltpu.VMEM((2 * b, GROUP_W, 2 * GROUP_W), F32), pltpu.VMEM((2 * b, 1, GROUP_W), F32)],
        compiler_params=_params("arbitrary"),
        name="mlstm",
    )(m_qkv, m_qkv, m_g, m_g, m_gT, m_gT, *consts)


def _gla_level_mats_np():
    L = CHUNK
    t = np.arange(L)
    sels, valids, pairs = [], [], []
    for rev in (False, True):
        pos_all = t if not rev else L - 1 - t
        eqs, eks, vq, vk, pm = [], [], [], [], []
        for j in range(6):
            w = 32 >> j
            par = pos_all // (2 * w)
            pos = pos_all % (2 * w)
            same = par[:, None] == par[None, :]
            late = pos >= w
            eq = same & late[:, None] & late[None, :] & (pos[None, :] <= pos[:, None])
            ek = same & (~late)[:, None] & (~late)[None, :] & (pos[None, :] > pos[:, None])
            eqs.append(eq)
            eks.append(ek)
            vq.append(late)
            vk.append(~late)
            pm.append(same & late[:, None] & (~late)[None, :])
        pm.append(np.eye(L, dtype=bool))
        sels.append(np.concatenate(eqs + eks, axis=0))
        valids.append(np.concatenate(vq + vk)[:, None])
        pairs.append(np.concatenate(pm, axis=0))
    f = lambda a: np.stack(a).astype(np.float32)
    return f(sels), f(valids), f(pairs)


def _gla_consts():
    sel, val, pair = _gla_level_mats_np()
    nq = 6 * CHUNK
    tile = lambda a: np.concatenate([a] * N_HEADS, axis=-1)
    selq, selk = sel[:, :nq], sel[:, nq:]
    selkt = np.stack([np.concatenate([selk[d, lv * CHUNK:(lv + 1) * CHUNK].T for lv in range(6)], axis=1)
                      for d in range(2)])
    valkt = np.stack([val[d, nq:, 0][None, :] for d in range(2)])
    rh = np.arange(N_HEADS * GLA_DK) // GLA_DK
    ch = np.arange(GROUP_W) // HEAD_DIM
    bdg = (rh[:, None] == ch[None, :]).astype(np.float32)
    return dict(selq=jnp.asarray(selq), valq=jnp.asarray(val[:, :nq]), selkt=jnp.asarray(selkt),
                valkt=jnp.asarray(valkt), pairw=jnp.asarray(tile(pair)), bdg=jnp.asarray(bdg, MXU_DT))


def _gla_body(qk_f, qk_b, v_f, v_b, la_f, la_b, cm_ref, selq_ref, valq_ref, selkt_ref, valkt_ref, pairw_ref,
              bdg_ref, bdm_ref, o_f, o_b, s_ref):
    @pl.when(pl.program_id(0) == 0)
    def _():
        s_ref[...] = jnp.zeros_like(s_ref)

    nb = qk_f.shape[0]
    streams = [(b, d) for b in range(nb) for d in range(2)]
    dirs = [d for _, d in streams]
    nk = N_HEADS * GLA_DK
    bdg = bdg_ref[...]
    bdm = bdm_ref[...]
    bdk = lambda xt: jnp.concatenate([xt.astype(MXU_DT)] * N_HEADS, axis=1) * bdg
    cm = [cm_ref[d] for d in dirs]
    selq = [selq_ref[d] for d in dirs]
    valq = [valq_ref[d] for d in dirs]
    selkt = [selkt_ref[d] for d in dirs]
    valkt = [valkt_ref[d] for d in dirs]
    pairw = [[pairw_ref[d, lv * CHUNK:(lv + 1) * CHUNK, :] for d in dirs] for lv in range(7)]
    ns = len(streams)
    items = [(si, j) for j in range(2) for si in range(ns)]
    idir = [dirs[si] for si, _ in items]
    offs = [(j if dirs[si] == 0 else 1 - j) * CHUNK for si, j in items]
    per = lambda lst: [lst[si] for si, _ in items]
    src = [(qk_f, v_f, la_f) if d == 0 else (qk_b, v_b, la_b) for d in idir]
    bidx = [streams[si][0] for si, _ in items]
    qk = [r[0][b, o:o + CHUNK, :] for b, r, o in zip(bidx, src, offs)]
    v = [r[1][b, o:o + CHUNK, :] for b, r, o in zip(bidx, src, offs)]
    la = [r[2][b, o:o + CHUNK, d * nk:(d + 1) * nk] for b, d, r, o in zip(bidx, idir, src, offs)]
    q = [x[:, :nk] * GLA_DK ** -0.5 for x in qk]
    k = [x[:, nk:] for x in qk]
    lat = [x.T for x in la]
    kt = [x.T for x in k]
    bcum = _each(_mm_sel, per(cm), la)
    exq = [jnp.exp(_mm_sel(s, x)) * vq for s, x, vq in zip(per(selq), la, per(valq))]
    exk = [jnp.exp(_mm_sel_r(x, s)) * vk for s, x, vk in zip(per(selkt), lat, per(valkt))]
    a = [p * _mm(qq, bdk(x)) for p, qq, x in zip(per(pairw[6]), q, kt)]
    for lv in range(6):
        sl = slice(lv * CHUNK, (lv + 1) * CHUNK)
        a = [aa + p * _mm(qq * eq[sl], bdk(x * ek[:, sl]))
             for aa, p, qq, eq, x, ek in zip(a, per(pairw[lv]), q, exq, kt, exk)]
    q_in = [qq * jnp.exp(bc) for qq, bc in zip(q, bcum)]
    b_last = [x[(CHUNK - 1 if d == 0 else 0):(CHUNK if d == 0 else 1), :] for x, d in zip(bcum, idir)]
    upd = [bdg.astype(F32) * _mm_tn(kk * jnp.exp(bl - bc), vv) for kk, bl, bc, vv in zip(k, b_last, bcum, v)]
    decay_t = [jnp.exp(jnp.sum(x, axis=1, keepdims=True)) for x in lat]
    st = [s_ref[si] for si in range(ns)]
    for j in range(2):
        cur = range(j * ns, (j + 1) * ns)
        o = [_mm(jnp.concatenate([q_in[i], a[i]], axis=1),
                 jnp.concatenate([s.astype(MXU_DT), _bd_rows(v[i], bdm)], axis=0)) for i, s in zip(cur, st)]
        for i, (b, d), oo in zip(cur, streams, o):
            (o_f if d == 0 else o_b)[b, offs[i]:offs[i] + CHUNK, :] = oo
        st = [decay_t[i] * s + upd[i] for i, s in zip(cur, st)]
    for si in range(ns):
        s_ref[si] = st[si]


def _gla(g_qk, g_vz, g_la, wc, gc, n_ctx_steps):
    b, t, _ = g_qk.shape
    n_steps = t // STEP
    fwd, bwd, tok, tokT, full = _scan_specs(b, n_ctx_steps, n_steps)
    consts = [wc["cm"]] + [gc[k] for k in ("selq", "valq", "selkt", "valkt", "pairw", "bdg")] + [wc["bdm"]]
    out = jax.ShapeDtypeStruct((b, t, GROUP_W), F32)
    w = 2 * N_HEADS * GLA_DK
    return pl.pallas_call(
        _gla_body,
        out_shape=[out, out],
        grid=(n_steps,),
        in_specs=[tok(w, fwd), tok(w, bwd), tok(GROUP_W, fwd), tok(GROUP_W, bwd),
                  tok(w, fwd), tok(w, bwd)] + [full(a) for a in consts],
        out_specs=[tok(GROUP_W, fwd), tok(GROUP_W, bwd)],
        scratch_shapes=[pltpu.VMEM((2 * b, N_HEADS * GLA_DK, GROUP_W), F32)],
        compiler_params=_params("arbitrary"),
        name="gla",
    )(g_qk, g_qk, g_vz, g_vz, g_la, g_la, *consts)


def _gdn_prep_body(prev_ref, cur_ref, next_ref, w_ref, ind_ref, o_ref, ext_ref, *, n_ctx_tiles, n_tiles):
    i = pl.program_id(1)
    tm = cur_ref.shape[0]
    first = (i == 0) | (i == n_ctx_tiles)
    last = (i == n_ctx_tiles - 1) | (i == n_tiles - 1)
    ext_ref[0:8, :] = jnp.where(first, 0.0, prev_ref[...])
    ext_ref[8:8 + tm, :] = cur_ref[...]
    ext_ref[8 + tm:16 + tm, :] = jnp.where(last, 0.0, next_ref[...])
    half = GDN_CONV // 2
    y = jnp.zeros(cur_ref.shape, F32)
    for j in range(GDN_CONV):
        y = y + ext_ref[8 - half + j:8 - half + j + tm, :] * w_ref[j:j + 1, :]
    y = jax.nn.silu(y)
    ind = ind_ref[...]
    q, k = y[:, :GROUP_W], y[:, GROUP_W:2 * GROUP_W]
    qn = q * lax.rsqrt(_mm_sel_r(q * q, ind) + EPS) * HEAD_DIM ** -0.5
    kn = k * lax.rsqrt(_mm_sel_r(k * k, ind) + EPS)
    o_ref[:, :GROUP_W] = qn
    o_ref[:, GROUP_W:2 * GROUP_W] = kn
    o_ref[:, 2 * GROUP_W:] = y[:, 2 * GROUP_W:]


def _gdn_prep(d_qkv, conv_w, ind64, n_ctx_tiles):
    b, t, w = d_qkv.shape
    tm = TOKEN_TILE
    n_tiles = t // tm
    r = tm // 8
    n8 = t // 8
    return pl.pallas_call(
        functools.partial(_gdn_prep_body, n_ctx_tiles=n_ctx_tiles, n_tiles=n_tiles),
        out_shape=jax.ShapeDtypeStruct((b, t, w), F32),
        grid=(b, n_tiles),
        in_specs=[pl.BlockSpec((None, 8, w), lambda bi, i: (bi, jnp.maximum(i * r - 1, 0), 0)),
                  pl.BlockSpec((None, tm, w), lambda bi, i: (bi, i, 0)),
                  pl.BlockSpec((None, 8, w), lambda bi, i: (bi, jnp.minimum((i + 1) * r, n8 - 1), 0)),
                  pl.BlockSpec(conv_w.shape, lambda bi, i: (0, 0)),
                  pl.BlockSpec(ind64.shape, lambda bi, i: (0, 0))],
        out_specs=pl.BlockSpec((None, tm, w), lambda bi, i: (bi, i, 0)),
        scratch_shapes=[pltpu.VMEM((tm + 16, w), F32)],
        compiler_params=_params("parallel", "parallel"),
        name="gdn_prep",
    )(d_qkv, d_qkv, d_qkv, conv_w, ind64)


def _wide_consts():
    cm = _cum_mats_np()
    eye = np.eye(CHUNK, dtype=np.float32)
    pair = _gla_level_mats_np()[2].reshape(2, 7, CHUNK, CHUNK)[:, :6]
    tile = lambda a: np.concatenate([a] * N_HEADS, axis=-1)
    hb = np.arange(GROUP_W) // HEAD_DIM
    bdm = (hb[:, None] == hb[None, :]).astype(np.float32)
    e_lo = np.zeros((2, 128, GROUP_W), np.float32)
    e_hi = np.zeros((2, 128, GROUP_W), np.float32)
    for d in range(2):
        for h in range(N_HEADS):
            e_lo[d, d * 4 + h, h * HEAD_DIM:(h + 1) * HEAD_DIM] = 1.0
            e_hi[d, 8 + d * 4 + h, h * HEAD_DIM:(h + 1) * HEAD_DIM] = 1.0
    z = np.zeros_like(cm)
    cm2 = np.concatenate([np.concatenate([cm, z], axis=2), np.concatenate([z, cm], axis=2)], axis=1)
    return dict(cm=jnp.asarray(cm), cm2=jnp.asarray(cm2), cmw=jnp.asarray(tile(cm)), strictw=jnp.asarray(tile(cm - eye)),
                eyew=jnp.asarray(tile(eye)), pairw=jnp.asarray(tile(pair).reshape(2, 6 * CHUNK, GROUP_W)),
                bdm=jnp.asarray(bdm, MXU_DT), e_lo=jnp.asarray(e_lo), e_hi=jnp.asarray(e_hi))


def _bd_rows(x, bdm):
    xb = x.astype(MXU_DT)
    return jnp.concatenate([xb] * N_HEADS, axis=0) * bdm


def _each(f, *lists):
    return [f(*a) for a in zip(*lists)]


def _gdn_body(qkv_f, qkv_b, g_f, g_b, gT_f, gT_b, cm2_ref, cmw_ref, strictw_ref, eyew_ref, pairw_ref, bdm_ref,
              elo_ref, ehi_ref, o_f, o_b, s_ref):
    @pl.when(pl.program_id(0) == 0)
    def _():
        s_ref[...] = jnp.zeros_like(s_ref)

    nb = qkv_f.shape[0]
    streams = [(b, d) for b in range(nb) for d in range(2)]
    dirs = [d for _, d in streams]
    bdm = bdm_ref[...]
    bdm2 = jnp.concatenate([bdm, bdm], axis=1)
    eyew = eyew_ref[...]
    cm2 = [cm2_ref[d] for d in dirs]
    cmw = [cmw_ref[d] for d in dirs]
    strictw = [strictw_ref[d] for d in dirs]
    pairw = [[pairw_ref[d, lv * CHUNK:(lv + 1) * CHUNK, :] for d in dirs] for lv in range(6)]
    elo = [elo_ref[d] for d in dirs]
    ehi = [ehi_ref[d] for d in dirs]
    g_all = [(g_f if d == 0 else g_b)[b] for b, d in streams]
    gt_all = [(gT_f if d == 0 else gT_b)[b] for b, d in streams]
    gc_all = _each(_mm_sel, cm2, g_all)
    gr_all = _each(_mm_sel_nt, gt_all, cm2)
    beta_all = _each(_mm_sel_r, g_all, elo)
    gw_all = _each(_mm_sel_r, gc_all, ehi)
    ns = len(streams)
    items = [(si, j) for j in range(2) for si in range(ns)]
    idir = [dirs[si] for si, _ in items]
    offs = [(j if dirs[si] == 0 else 1 - j) * CHUNK for si, j in items]
    per = lambda lst: [lst[si] for si, _ in items]
    qkv = [(qkv_f if d == 0 else qkv_b)[streams[si][0], o:o + CHUNK, :] for (si, _), d, o in zip(items, idir, offs)]
    gr = [x[:, o:o + CHUNK] for x, o in zip(per(gr_all), offs)]
    beta = [x[o:o + CHUNK] for x, o in zip(per(beta_all), offs)]
    gw = [x[o:o + CHUNK] for x, o in zip(per(gw_all), offs)]
    grow = [jnp.concatenate([x[8 + 4 * d + h:9 + 4 * d + h, :] for h in range(N_HEADS)], axis=1)
            for x, d in zip(gr, idir)]
    decay = [jnp.exp(jnp.where(m > 0, a - r, -jnp.inf)) for m, a, r in zip(per(cmw), gw, grow)]
    q = [x[:, :GROUP_W] for x in qkv]
    k = [x[:, GROUP_W:2 * GROUP_W] for x in qkv]
    v = [x[:, 2 * GROUP_W:] for x in qkv]
    kb = _each(lambda a, b_: a * b_, k, beta)
    bdk = [jnp.concatenate([x.T.astype(MXU_DT)] * N_HEADS, axis=1) * bdm for x in k]
    gq = _each(lambda a, b_, w: _mm(jnp.concatenate([a, b_], axis=0), w), kb, q, bdk)
    lower = [s * x[:CHUNK] * dc for s, x, dc in zip(per(strictw), gq, decay)]
    attn = [x[CHUNK:] * dc for x, dc in zip(gq, decay)]
    tw = [eyew - lo * p for lo, p in zip(lower, per(pairw[5]))]
    for lv in range(4, -1, -1):
        y = [_mm(lo * p, _bd_rows(t, bdm)) for lo, p, t in zip(lower, per(pairw[lv]), tw)]
        tw = [t - _mm(t, _bd_rows(yy, bdm)) for t, yy in zip(tw, y)]
    rhs = [jnp.concatenate([vv * b_, kk * jnp.exp(a)], axis=1).astype(MXU_DT)
           for vv, b_, kk, a in zip(v, beta, kb, gw)]
    uw = [_mm(t, jnp.concatenate([r] * N_HEADS, axis=0) * bdm2) for t, r in zip(tw, rhs)]
    q_in = [qq * jnp.exp(a) for qq, a in zip(q, gw)]
    g_last = [a[(CHUNK - 1 if d == 0 else 0):(CHUNK if d == 0 else 1), :] for a, d in zip(gw, idir)]
    k_out = [kk * jnp.exp(gl - a) for kk, gl, a in zip(k, g_last, gw)]
    st = [s_ref[si] for si in range(ns)]
    for j in range(2):
        cur = range(j * ns, (j + 1) * ns)
        v_new = [uw[i][:, :GROUP_W] - _mm(uw[i][:, GROUP_W:], s) for i, s in zip(cur, st)]
        o = [_mm(jnp.concatenate([q_in[i], attn[i]], axis=1),
                 jnp.concatenate([s.astype(MXU_DT), _bd_rows(vn, bdm)], axis=0))
             for i, s, vn in zip(cur, st, v_new)]
        for i, (b, d), oo in zip(cur, streams, o):
            (o_f if d == 0 else o_b)[b, offs[i]:offs[i] + CHUNK, :] = oo
        st = [jnp.exp(g_last[i]) * s + bdm.astype(F32) * _mm_tn(k_out[i], vn) for i, s, vn in zip(cur, st, v_new)]
    for si in range(ns):
        s_ref[si] = st[si]


def _gdn(d_qkv, d_g, d_gT, wc, n_ctx_steps):
    b, t, _ = d_qkv.shape
    n_steps = t // STEP
    fwd, bwd, tok, tokT, full = _scan_specs(b, n_ctx_steps, n_steps)
    consts = [wc[k] for k in ("cm2", "cmw", "strictw", "eyew", "pairw", "bdm", "e_lo", "e_hi")]
    out = jax.ShapeDtypeStruct((b, t, GROUP_W), F32)
    return pl.pallas_call(
        _gdn_body,
        out_shape=[out, out],
        grid=(n_steps,),
        in_specs=[tok(3 * GROUP_W, fwd), tok(3 * GROUP_W, bwd), tok(128, fwd), tok(128, bwd), tokT(fwd), tokT(bwd)]
                 + [full(a) for a in consts],
        out_specs=[tok(GROUP_W, fwd), tok(GROUP_W, bwd)],
        scratch_shapes=[pltpu.VMEM((2 * b, GROUP_W, GROUP_W), F32)],
        compiler_params=_params("arbitrary"),
        name="gdn",
    )(d_qkv, d_qkv, d_g, d_g, d_gT, d_gT, *consts)


def _rope_tables(n_ctx, n_lat):
    rows = n_lat // GRID_W
    row = np.repeat(np.arange(rows), GRID_W).astype(np.float32)
    col = np.tile(np.arange(GRID_W), rows).astype(np.float32)
    half = DA_QK // 2
    inv = np.power(ROPE_BASE, -np.arange(0, half, 2, dtype=np.float32) / half).astype(np.float32)

    def tab(p):
        ang = p[:, None] * inv
        ang = np.concatenate([ang, ang], axis=-1)
        return np.cos(ang), np.sin(ang)

    cr, sr = tab(row)
    cc, sc = tab(col)
    cos = np.concatenate([cr, cc], -1)
    sin = np.concatenate([sr, sc], -1)
    cos = np.concatenate([np.ones((n_ctx, DA_QK), np.float32), cos], 0)
    sin = np.concatenate([np.zeros((n_ctx, DA_QK), np.float32), sin], 0)
    reps = GROUP_W // DA_QK
    cos, sin = np.tile(cos, (1, reps)), np.tile(sin, (1, reps))
    first = (np.arange(GROUP_W) % 16) < 8
    sin_a = np.where(first, -sin, 0.0)
    sin_b = np.where(first, 0.0, sin)
    return jnp.asarray(cos, F32), jnp.asarray(sin_a, F32), jnp.asarray(sin_b, F32)


def _diff_prep_body(qk_ref, v_ref, cos_ref, sa_ref, sb_ref, nw_ref, ind_ref, qt_out, k_out, vt_out):
    ind = ind_ref[...]
    cos, sa, sb = cos_ref[...], sa_ref[...], sb_ref[...]

    def norm_rope(x, w):
        xn = x * lax.rsqrt(_mm_sel_r(x * x, ind) * (1.0 / DA_QK) + EPS) * w
        return xn * cos + pltpu.roll(xn, GROUP_W - 8, 1) * sa + pltpu.roll(xn, 8, 1) * sb

    qk = qk_ref[...]
    q = norm_rope(qk[:, :GROUP_W], nw_ref[0:1, :]) * (DA_QK ** -0.5 * math.log2(math.e))
    k = norm_rope(qk[:, GROUP_W:], nw_ref[1:2, :])
    qt_out[...] = q.T.astype(qt_out.dtype)
    for h in range(N_HEADS):
        k_out[h] = k[:, h * HEAD_DIM:(h + 1) * HEAD_DIM].astype(k_out.dtype)
    vt = v_ref[...].T
    tm = vt.shape[1]
    tail = jnp.concatenate([jnp.ones((8, tm), F32), jnp.zeros((V_ROWS - HEAD_DIM - 8, tm), F32)], axis=0)
    for h in range(N_HEADS):
        vt_out[h] = jnp.concatenate([vt[h * HEAD_DIM:(h + 1) * HEAD_DIM], tail], axis=0).astype(vt_out.dtype)


def _diff_prep(a_qk, a_vz, tables, nw, ind32):
    b, t, _ = a_qk.shape
    tm = TOKEN_TILE
    cos, sa, sb = tables
    tab = pl.BlockSpec((tm, GROUP_W), lambda bi, i: (i, 0))
    return pl.pallas_call(
        _diff_prep_body,
        out_shape=[jax.ShapeDtypeStruct((b, GROUP_W, t), MXU_DT),
                   jax.ShapeDtypeStruct((b, N_HEADS, t, HEAD_DIM), MXU_DT),
                   jax.ShapeDtypeStruct((b, N_HEADS, V_ROWS, t), MXU_DT)],
        grid=(b, t // tm),
        in_specs=[pl.BlockSpec((None, tm, 2 * GROUP_W), lambda bi, i: (bi, i, 0)),
                  pl.BlockSpec((None, tm, GROUP_W), lambda bi, i: (bi, i, 0)),
                  tab, tab, tab,
                  pl.BlockSpec(nw.shape, lambda bi, i: (0, 0)),
                  pl.BlockSpec(ind32.shape, lambda bi, i: (0, 0))],
        out_specs=[pl.BlockSpec((None, GROUP_W, tm), lambda bi, i: (bi, 0, i)),
                   pl.BlockSpec((None, N_HEADS, tm, HEAD_DIM), lambda bi, i: (bi, 0, i, 0)),
                   pl.BlockSpec((None, N_HEADS, V_ROWS, tm), lambda bi, i: (bi, 0, 0, i))],
        compiler_params=_params("parallel", "parallel"),
        name="diff_prep",
    )(a_qk, a_vz, cos, sa, sb, nw, ind32)


def _diff_attn_body(lam_ref, qt_ref, k_ref, vp_ref, vc_ref, o_ref, acc_ref, m_ref, e_ref, *, lam_init):
    j = pl.program_id(3)

    @pl.when(j == 0)
    def _():
        acc_ref[...] = jnp.zeros_like(acc_ref)
        m_ref[...] = jnp.full_like(m_ref, -jnp.inf)
        e_ref[...] = jnp.zeros_like(e_ref)

    vp = vp_ref[...]
    kk = k_ref[...]
    qt = qt_ref[...]
    part = lax.broadcasted_iota(jnp.int32, (2 * DA_QK, 1), 0) // DA_QK
    s = [jnp.dot(kk, jnp.where(part == p, qt, jnp.zeros_like(qt)), preferred_element_type=F32)
         for p in range(2)]
    pv = [jnp.dot(vp, e_ref[p], preferred_element_type=F32) for p in range(2)]
    for p in range(2):
        m_old = m_ref[p]
        m_new = jnp.maximum(m_old, jnp.max(s[p], axis=0, keepdims=True))
        e_ref[p] = jnp.exp2(s[p] - m_new).astype(e_ref.dtype)
        acc_ref[p] = jnp.exp2(m_old - m_new) * (acc_ref[p] + pv[p])
        m_ref[p] = m_new

    @pl.when(j == pl.num_programs(3) - 1)
    def _():
        lam = lam_ref[...]
        lam_full = (jnp.exp(jnp.sum(lam[0:1] * lam[1:2], keepdims=True))
                    - jnp.exp(jnp.sum(lam[2:3] * lam[3:4], keepdims=True)) + lam_init)
        vc = vc_ref[...]
        a0 = acc_ref[0] + jnp.dot(vc, e_ref[0], preferred_element_type=F32)
        a1 = acc_ref[1] + jnp.dot(vc, e_ref[1], preferred_element_type=F32)
        ot = (a0[:HEAD_DIM] / a0[HEAD_DIM:HEAD_DIM + 1]
              - lam_full * (a1[:HEAD_DIM] / a1[HEAD_DIM:HEAD_DIM + 1]))
        o_ref[...] = ot.T


def _diff_attn(lam, qt, k, vt, lam_init, n_keys, tq, tk):
    b, _, n_q = qt.shape
    return pl.pallas_call(
        functools.partial(_diff_attn_body, lam_init=lam_init),
        out_shape=jax.ShapeDtypeStruct((b, N_HEADS, n_q, HEAD_DIM), F32),
        grid=(b, N_HEADS, n_q // tq, n_keys // tk),
        in_specs=[pl.BlockSpec(lam.shape, lambda bi, h, i, j: (0, 0)),
                  pl.BlockSpec((None, 2 * DA_QK, tq), lambda bi, h, i, j: (bi, h, i)),
                  pl.BlockSpec((None, None, tk, HEAD_DIM), lambda bi, h, i, j: (bi, h, j, 0)),
                  pl.BlockSpec((None, None, V_ROWS, tk), lambda bi, h, i, j: (bi, h, 0, jnp.maximum(j - 1, 0))),
                  pl.BlockSpec((None, None, V_ROWS, tk), lambda bi, h, i, j: (bi, h, 0, j))],
        out_specs=pl.BlockSpec((None, None, tq, HEAD_DIM), lambda bi, h, i, j: (bi, h, i, 0)),
        scratch_shapes=[pltpu.VMEM((2, V_ROWS, tq), F32), pltpu.VMEM((2, 1, tq), F32),
                        pltpu.VMEM((2, tk, tq), MXU_DT)],
        compiler_params=_params("parallel", "parallel", "parallel", "arbitrary"),
        name="diff_attn",
    )(lam, qt, k, vt, vt)


def _finish_body(x_ref, mhf_ref, mhb_ref, moz_ref, ao_ref, az_ref, ghf_ref, ghb_ref, gz_ref, dhf_ref, dhb_ref, dz_ref,
                 nw_ref, ind_ref, wout_ref, gt_ref, o_ref, *, lam_init):
    ind = ind_ref[...]

    def head_norm(hh, w):
        return hh * lax.rsqrt(_mm_sel_r(hh * hh, ind) * (1.0 / HEAD_DIM) + EPS) * w

    moz = moz_ref[...]
    ym = (head_norm(mhf_ref[...] + mhb_ref[...], nw_ref[0:1, :]) * jax.nn.sigmoid(moz[:, :GROUP_W])
          * jax.nn.silu(moz[:, GROUP_W:]))
    ao = jnp.concatenate([ao_ref[h] for h in range(N_HEADS)], axis=1)
    ya = head_norm(ao, nw_ref[1:2, :]) * (1.0 - lam_init) * jax.nn.silu(az_ref[...])
    yg = head_norm(ghf_ref[...] + ghb_ref[...], nw_ref[2:3, :]) * jax.nn.silu(gz_ref[...])
    yd = head_norm(dhf_ref[...] + dhb_ref[...], nw_ref[3:4, :]) * jax.nn.silu(dz_ref[...])
    y = jnp.concatenate([ym, ya, yg, yd], axis=1).astype(MXU_DT)
    o_ref[...] = x_ref[...] + gt_ref[...] * jnp.dot(y, wout_ref[...], preferred_element_type=F32)


def _finish(xs, mh, m_oz, ao, a_vz, gh, g_vz, dh, d_z, nw, ind64, wout, gtsel, lam_init, n_ctx_tiles, first_tile):
    b, t, d = xs.shape
    tm = TOKEN_TILE
    f0 = first_tile
    tok = lambda w, c=0: pl.BlockSpec((None, tm, w), lambda bi, i: (bi, i + f0, c))
    g = tok(GROUP_W)
    full = lambda a: pl.BlockSpec(a.shape, lambda bi, i: (0,) * a.ndim)
    return pl.pallas_call(
        functools.partial(_finish_body, lam_init=lam_init),
        out_shape=jax.ShapeDtypeStruct((b, t - f0 * tm, d), F32),
        grid=(b, t // tm - f0),
        in_specs=[tok(d), g, g, tok(2 * GROUP_W),
                  pl.BlockSpec((None, N_HEADS, tm, HEAD_DIM), lambda bi, i: (bi, 0, i + f0, 0)),
                  tok(GROUP_W, 1), g, g, tok(GROUP_W, 1), g, g, g,
                  full(nw), full(ind64), full(wout),
                  pl.BlockSpec((None, None, 1, d),
                               lambda bi, i: (bi, (i + f0 >= n_ctx_tiles).astype(jnp.int32), 0, 0))],
        out_specs=pl.BlockSpec((None, tm, d), lambda bi, i: (bi, i, 0)),
        compiler_params=_params("parallel", "parallel"),
        name="finish",
    )(xs, mh[0], mh[1], m_oz, ao, a_vz, gh[0], gh[1], g_vz, dh[0], dh[1], d_z, nw, ind64, wout, gtsel)


def kernel(x, c, ctx, c_ctx, norm_w, w_mod, b_mod, w_in, w_out, mlstm_b_i, mlstm_b_f, mlstm_norm, diff_q_norm,
           diff_k_norm, diff_lambda, diff_norm, gla_w_up, gla_b, gla_norm, gdn_conv, gdn_a_log, gdn_dt_bias,
           gdn_norm):
    bsz, n_lat, d = x.shape
    n_ctx = ctx.shape[1]
    depth = w_in.shape[0]
    t = n_ctx + n_lat
    assert n_ctx % TOKEN_TILE == 0 and n_lat % TOKEN_TILE == 0 and n_lat % GRID_W == 0
    assert n_lat % ATTN_TQ == 0 and t % ATTN_TK == 0 and n_ctx % STEP == 0
    n_ctx_tiles = n_ctx // TOKEN_TILE
    n_ctx_steps = n_ctx // STEP

    cc = jnp.concatenate([c, c_ctx[None, :], jnp.zeros((8 - (bsz + 1) % 8, d), F32)], axis=0)
    mod = _modulation(cc, w_mod, b_mod)

    wide = _wide_consts()
    gla_consts = _gla_consts()
    ind64 = _group_ones(GROUP_W, HEAD_DIM)
    ind32 = _group_ones(GROUP_W, DA_QK)
    tables = _rope_tables(n_ctx, n_lat)
    pad = lambda a, n: jnp.concatenate([a, jnp.zeros(a.shape[:-1] + (n - a.shape[-1],), a.dtype)], axis=-1)

    xs = jnp.concatenate([ctx, x], axis=1)
    for l in range(depth):
        lam_init = 0.8 - 0.6 * math.exp(-0.3 * l)
        sh, sc, gt = jnp.split(mod[l], 3, axis=-1)
        lat = jnp.stack([sh[:bsz], sc[:bsz]], axis=1)
        cx = jnp.broadcast_to(jnp.stack([sh[bsz], sc[bsz]], axis=0)[None], (bsz, 2, d))
        modsel = jnp.stack([cx, lat], axis=1)
        gtsel = jnp.stack([jnp.broadcast_to(gt[bsz][None], (bsz, d)), gt[:bsz]], axis=1)[:, :, None, :]
        wp, wg = _permute_w_in(w_in[l])
        pm = pad(jnp.concatenate([mlstm_b_i[l].reshape(1, 8), mlstm_b_f[l].reshape(1, 8)], axis=1), 128)
        pm = jnp.concatenate([pm, jnp.zeros((7, 128), F32)], axis=0)
        z8 = jnp.zeros((1, 8), F32)
        pd = jnp.concatenate([pad(jnp.concatenate([z8, gdn_a_log[l].reshape(1, 8)], axis=1), 128),
                              pad(jnp.concatenate([z8, gdn_dt_bias[l].reshape(1, 8)], axis=1), 128),
                              jnp.zeros((6, 128), F32)], axis=0)
        wup = jnp.zeros((128, 256), F32)
        wup = wup.at[0:GLA_RANK, 0:128].set(gla_w_up[l, 0]).at[GLA_RANK:2 * GLA_RANK, 128:256].set(gla_w_up[l, 1])
        bgk = gla_b[l].reshape(1, 256)

        (m_qkv, m_oz, m_g, m_gT, a_qk, a_vz, g_qk, g_vz, g_la, d_qkv, d_z, d_g, d_gT) = _inproj(
            xs, norm_w[l][None, :], modsel, wp, wg, pm, pd, wup.astype(MXU_DT), bgk, n_ctx_tiles)

        mh = _mlstm(m_qkv, m_g, m_gT, wide, ind64, n_ctx_steps)

        qkn = jnp.stack([jnp.tile(diff_q_norm[l], GROUP_W // DA_QK), jnp.tile(diff_k_norm[l], GROUP_W // DA_QK)])
        aqt, ak, avt = _diff_prep(a_qk, a_vz, tables, qkn, ind32)
        lam = pad(diff_lambda[l], 128)
        ao_ctx = _diff_attn(lam, aqt[:, :, :n_ctx], ak, avt, lam_init, n_ctx, n_ctx, n_ctx)
        ao_lat = _diff_attn(lam, aqt[:, :, n_ctx:], ak, avt, lam_init, t, ATTN_TQ, ATTN_TK)
        ao = jnp.concatenate([ao_ctx, ao_lat], axis=2)

        gh = _gla(g_qk, g_vz, g_la, wide, gla_consts, n_ctx_steps)

        dq = _gdn_prep(d_qkv, gdn_conv[l], ind64, n_ctx_tiles)
        dh = _gdn(dq, d_g, d_gT, wide, n_ctx_steps)

        nws = jnp.stack([mlstm_norm[l], diff_norm[l], gla_norm[l], gdn_norm[l]])
        xs = _finish(xs, mh, m_oz, ao, a_vz, gh, g_vz, dh, d_z, nws, ind64, w_out[l].astype(MXU_DT),
                     gtsel, lam_init, n_ctx_tiles, n_ctx_tiles if l == depth - 1 else 0)
    return xs
```

```python
import functools
import math

import jax
import jax.numpy as jnp
import numpy as np
from jax import lax
from jax.experimental import pallas as pl
from jax.experimental.pallas import tpu as pltpu

F32 = jnp.float32
MXU_DT = jnp.bfloat16
EPS = 1e-6
N_HEADS = 4
HEAD_DIM = 64
GROUP_W = N_HEADS * HEAD_DIM
CHUNK = 64
STEP = 2 * CHUNK
DA_QK = 32
GLA_DK = 32
GLA_RANK = 16
GLA_TAU = 16.0
GDN_CONV = 5
GRID_W = 64
ROPE_BASE = 10000.0
TOKEN_TILE = 256
ATTN_TQ = 2048
ATTN_TK = 768
V_ROWS = 80
VMEM_LIMIT = 48 * 1024 * 1024
NEG_BIG = -1e30


def _mm(a, b):
    return jnp.dot(a.astype(MXU_DT), b.astype(MXU_DT), preferred_element_type=F32)


def _mm_nt(a, b):
    return lax.dot_general(a.astype(MXU_DT), b.astype(MXU_DT), (((1,), (1,)), ((), ())),
                           preferred_element_type=F32)


def _mm_tn(a, b):
    return lax.dot_general(a.astype(MXU_DT), b.astype(MXU_DT), (((0,), (0,)), ((), ())),
                           preferred_element_type=F32)


def _split2(x):
    hi = x.astype(MXU_DT)
    lo = (x - hi.astype(F32)).astype(MXU_DT)
    return hi, lo


def _mm_sel(sel, x):
    hi, lo = _split2(x)
    s = sel.astype(MXU_DT)
    return (jnp.dot(s, hi, preferred_element_type=F32) + jnp.dot(s, lo, preferred_element_type=F32))


def _mm_sel_r(x, sel):
    hi, lo = _split2(x)
    s = sel.astype(MXU_DT)
    return (jnp.dot(hi, s, preferred_element_type=F32) + jnp.dot(lo, s, preferred_element_type=F32))


def _mm_sel_nt(x, sel):
    hi, lo = _split2(x)
    s = sel.astype(MXU_DT)
    dn = (((1,), (1,)), ((), ()))
    return (lax.dot_general(hi, s, dn, preferred_element_type=F32)
            + lax.dot_general(lo, s, dn, preferred_element_type=F32))


def _group_ones(width, group):
    g = np.arange(width) // group
    return jnp.asarray((g[:, None] == g[None, :]).astype(np.float32))


def _params(*sem):
    return pltpu.CompilerParams(dimension_semantics=sem, vmem_limit_bytes=VMEM_LIMIT)


def _mod_body(cc_ref, w_ref, b_ref, o_ref):
    o_ref[...] = _mm(jax.nn.silu(cc_ref[...]), w_ref[...]) + b_ref[...]


def _modulation(cc, w_mod, b_mod):
    depth, d, d3 = w_mod.shape
    tn = 1024
    return pl.pallas_call(
        _mod_body,
        out_shape=jax.ShapeDtypeStruct((depth, cc.shape[0], d3), F32),
        grid=(depth, d3 // tn),
        in_specs=[pl.BlockSpec(cc.shape, lambda l, j: (0, 0)),
                  pl.BlockSpec((None, d, tn), lambda l, j: (l, 0, j)),
                  pl.BlockSpec((None, 1, tn), lambda l, j: (l, 0, j))],
        out_specs=pl.BlockSpec((None, cc.shape[0], tn), lambda l, j: (l, 0, j)),
        compiler_params=_params("parallel", "parallel"),
        name="modulation",
    )(cc, w_mod, b_mod.reshape(depth, 1, d3))


_SEG = dict(m_qkv=(0, 768), m_oz=(768, 1280), m_g=(1280, 1408), a_qk=(1408, 1920), a_vz=(1920, 2432),
            g_qk=(2432, 2688), g_vz=(2688, 3200), g_r=(3200, 3328), d_qkv=(3328, 4096), d_z=(4096, 4352),
            d_g=(4352, 4480))
P_PAD = 4480


def _permute_w_in(w):
    d = w.shape[0]
    z = lambda n: jnp.zeros((d, n), w.dtype)
    m0, a0, g0, d0 = 0, 1296, 2320, 3120
    cols = [w[:, m0:m0 + 1280], w[:, m0 + 1280:m0 + 1296], z(112),
            w[:, a0:a0 + 512], w[:, a0 + 512:a0 + 1024],
            w[:, g0:g0 + 256], w[:, g0 + 256:g0 + 768], w[:, g0 + 768:g0 + 800], z(96),
            w[:, d0:d0 + 768], w[:, d0 + 768:d0 + 1024], w[:, d0 + 1024:d0 + 1040], z(112)]
    wp = jnp.concatenate(cols, axis=1)
    wg = jnp.concatenate([wp[:, 1280:1408], wp[:, 4352:4480]], axis=1).T
    return wp.astype(MXU_DT), wg.astype(MXU_DT)


def _gate_fns(xm, xd, pm, pd, lane):
    gm = jnp.where(lane < 8, xm + pm[0], jax.nn.log_sigmoid(xm + pm[0]))
    gd = jnp.where(lane < 8, jax.nn.sigmoid(xd), -jnp.exp(pd[0]) * jax.nn.softplus(xd + pd[1]))
    return gm, gd


def _inproj_body(x_ref, nw_ref, mod_ref, wp_ref, wg_ref, pm_ref, pd_ref, pmT_ref, pdT_ref, wup_ref, bgk_ref,
                 cos_ref, sa_ref, sb_ref, qkn_ref, ind32_ref,
                 m_qkv, m_oz, m_g, m_gT, a_qt, a_k, a_vt, a_z, g_qk, g_vz, g_la, d_qkv, d_z, d_g, d_gT):
    x = x_ref[...]
    h = x * lax.rsqrt(jnp.mean(x * x, axis=-1, keepdims=True) + EPS) * nw_ref[...]
    h = (h * (1.0 + mod_ref[1:2, :]) + mod_ref[0:1, :]).astype(MXU_DT)
    u = jnp.dot(h, wp_ref[...], preferred_element_type=F32)
    seg = lambda name: u[:, _SEG[name][0]:_SEG[name][1]]
    m_qkv[...] = seg("m_qkv")
    m_oz[...] = seg("m_oz")
    avz = seg("a_vz")
    a_z[...] = avz[:, GROUP_W:]
    _diff_operands(seg("a_qk"), avz[:, :GROUP_W], cos_ref[...], sa_ref[...], sb_ref[...], qkn_ref, ind32_ref[...],
                   a_qt, a_k, a_vt)
    g_qk[...] = seg("g_qk")
    g_vz[...] = seg("g_vz")
    d_qkv[...] = seg("d_qkv")
    d_z[...] = seg("d_z")
    lane = lax.broadcasted_iota(jnp.int32, (1, 128), 1)
    gm, gd = _gate_fns(seg("m_g"), seg("d_g"), (pm_ref[0:1, :],), (pd_ref[0:1, :], pd_ref[1:2, :]), lane)
    m_g[...] = gm
    d_g[...] = gd
    ut = lax.dot_general(wg_ref[...], h, (((1,), (1,)), ((), ())), preferred_element_type=F32)
    row = lax.broadcasted_iota(jnp.int32, (128, 1), 0)
    gmt, gdt = _gate_fns(ut[0:128], ut[128:256], (pmT_ref[:, 0:1],), (pdT_ref[:, 0:1], pdT_ref[:, 1:2]), row)
    m_gT[...] = gmt
    d_gT[...] = gdt
    gk = _mm(seg("g_r"), wup_ref[...]) + bgk_ref[...]
    g_la[...] = jax.nn.log_sigmoid(gk) / GLA_TAU


def _inproj(xs, nw, modsel, wp, wg, pm, pd, wup, bgk, tables, qkn, ind32, n_ctx_tiles):
    b, t, d = xs.shape
    tm = TOKEN_TILE
    tok = lambda w: pl.BlockSpec((None, tm, w), lambda bi, i: (bi, i, 0))
    tokT = pl.BlockSpec((None, 128, tm), lambda bi, i: (bi, 0, i))
    full = lambda a: pl.BlockSpec(a.shape, lambda bi, i: (0,) * a.ndim)
    tab = pl.BlockSpec((tm, GROUP_W), lambda bi, i: (i, 0))
    attn_shapes = [jax.ShapeDtypeStruct((b, GROUP_W, t), MXU_DT),
                   jax.ShapeDtypeStruct((b, N_HEADS, t, HEAD_DIM), MXU_DT),
                   jax.ShapeDtypeStruct((b, N_HEADS, V_ROWS, t), MXU_DT)]
    attn_specs = [pl.BlockSpec((None, GROUP_W, tm), lambda bi, i: (bi, 0, i)),
                  pl.BlockSpec((None, N_HEADS, tm, HEAD_DIM), lambda bi, i: (bi, 0, i, 0)),
                  pl.BlockSpec((None, N_HEADS, V_ROWS, tm), lambda bi, i: (bi, 0, 0, i))]
    widths = [768, 512, 128, None, "attn", 256, 256, 512, 256, 768, 256, 128, None]
    out_shape, out_specs = [], []
    for w in widths:
        if w is None:
            out_shape.append(jax.ShapeDtypeStruct((b, 128, t), F32))
            out_specs.append(tokT)
        elif w == "attn":
            out_shape += attn_shapes
            out_specs += attn_specs
        else:
            out_shape.append(jax.ShapeDtypeStruct((b, t, w), F32))
            out_specs.append(tok(w))
    pmT, pdT = pm.T, pd.T
    cos, sa, sb = tables
    return pl.pallas_call(
        _inproj_body,
        out_shape=out_shape,
        grid=(b, t // tm),
        in_specs=[tok(d), full(nw),
                  pl.BlockSpec((None, None, 2, d), lambda bi, i: (bi, (i >= n_ctx_tiles).astype(jnp.int32), 0, 0)),
                  full(wp), full(wg), full(pm), full(pd), full(pmT), full(pdT), full(wup), full(bgk),
                  tab, tab, tab, full(qkn), full(ind32)],
        out_specs=out_specs,
        compiler_params=_params("parallel", "parallel"),
        name="inproj",
    )(xs, nw, modsel, wp, wg, pm, pd, pmT, pdT, wup, bgk, cos, sa, sb, qkn, ind32)


def _cum_mats_np():
    i = np.arange(CHUNK)
    fwd = (i[None, :] <= i[:, None]).astype(np.float32)
    return np.stack([fwd, fwd.T])


def _step_block(d, n, n_ctx_steps, n_steps):
    bwd = jnp.where(n < n_ctx_steps, n_ctx_steps - 1 - n, n_steps - 1 - (n - n_ctx_steps))
    return jnp.where(d == 0, n, bwd)


def _head_rowmax(x):
    return jnp.concatenate(
        [jnp.broadcast_to(jnp.max(x[:, h * HEAD_DIM:(h + 1) * HEAD_DIM], axis=1, keepdims=True), (CHUNK, HEAD_DIM))
         for h in range(N_HEADS)], axis=1)


def _mlstm_body(qkv_f, qkv_b, g_f, g_b, gT_f, gT_b, cm2_ref, cmw_ref, bdm_ref, elo_ref, ehi_ref, ind_ref,
                o_f, o_b, cn_ref, m_ref):
    @pl.when(pl.program_id(0) == 0)
    def _():
        cn_ref[...] = jnp.zeros_like(cn_ref)
        m_ref[...] = jnp.zeros_like(m_ref)

    nb = qkv_f.shape[0]
    streams = [(b, d) for b in range(nb) for d in range(2)]
    dirs = [d for _, d in streams]
    bdm = bdm_ref[...]
    bdm2 = jnp.concatenate([bdm, bdm], axis=1).astype(F32)
    ind = ind_ref[...].astype(MXU_DT)
    ones = jnp.ones((CHUNK, GROUP_W), F32)
    dup = lambda a: jnp.concatenate([a, a], axis=1)
    cm2 = [cm2_ref[d] for d in dirs]
    cmw = [cmw_ref[d] for d in dirs]
    elo = [elo_ref[d] for d in dirs]
    ehi = [ehi_ref[d] for d in dirs]
    g_all = [(g_f if d == 0 else g_b)[b] for b, d in streams]
    gt_all = [(gT_f if d == 0 else gT_b)[b] for b, d in streams]
    bc_all = _each(_mm_sel, cm2, g_all)
    br_all = _each(_mm_sel_nt, gt_all, cm2)
    bw_all = _each(_mm_sel_r, bc_all, ehi)
    iw_all = _each(_mm_sel_r, g_all, elo)
    for j in range(2):
        offs = [(j if d == 0 else 1 - j) * CHUNK for d in dirs]
        qkv = [(qkv_f if d == 0 else qkv_b)[b, o:o + CHUNK, :] for (b, d), o in zip(streams, offs)]
        gt = [x[:, o:o + CHUNK] for x, o in zip(gt_all, offs)]
        br = [x[:, o:o + CHUNK] for x, o in zip(br_all, offs)]
        bw = [x[o:o + CHUNK] for x, o in zip(bw_all, offs)]
        iw = [x[o:o + CHUNK] for x, o in zip(iw_all, offs)]
        rw = [jnp.concatenate([a[4 * d + h:4 * d + h + 1, :] - c[8 + 4 * d + h:9 + 4 * d + h, :]
                               for h in range(N_HEADS)], axis=1) for a, c, d in zip(gt, br, dirs)]
        dmat = [jnp.where(m > 0, a + r, -jnp.inf) for m, a, r in zip(cmw, bw, rw)]
        m_prev = [m_ref[i] for i in range(len(streams))]
        inter = _each(lambda a, m: a + m, bw, m_prev)
        m_t = [jnp.maximum(a, _head_rowmax(x)) for a, x in zip(inter, dmat)]
        q = [x[:, :GROUP_W] for x in qkv]
        k = [x[:, GROUP_W:2 * GROUP_W] * HEAD_DIM ** -0.5 for x in qkv]
        v = [x[:, 2 * GROUP_W:] for x in qkv]
        bdk = [jnp.concatenate([x.T.astype(MXU_DT)] * N_HEADS, axis=1) * bdm for x in k]
        s = [_mm(qq, w) * jnp.exp(x - m) for qq, w, x, m in zip(q, bdk, dmat, m_t)]
        w_inter = [jnp.exp(a - m) for a, m in zip(inter, m_t)]
        cn = [cn_ref[i] for i in range(len(streams))]
        acc = [dup(wi) * _mm(qq, c) + _mm(ss, jnp.concatenate([_bd_rows(vv, bdm), ind], axis=1))
               for wi, qq, c, ss, vv in zip(w_inter, q, cn, s, v)]
        hout = [a[:, :GROUP_W] / jnp.maximum(jnp.abs(a[:, GROUP_W:]), jnp.exp(-m)) for a, m in zip(acc, m_t)]
        last = [CHUNK - 1 if d == 0 else 0 for d in dirs]
        b_last = [a[r:r + 1, :] for a, r in zip(bw, last)]
        ds = [bl - a + i_ for bl, a, i_ in zip(b_last, bw, iw)]
        m_new = [jnp.maximum(bl + m, jnp.max(x, axis=0, keepdims=True)) for bl, m, x in zip(b_last, m_prev, ds)]
        for i, ((b, d), off) in enumerate(zip(streams, offs)):
            (o_f if d == 0 else o_b)[b, off:off + CHUNK, :] = hout[i]
            w_s = jnp.exp(ds[i] - m_new[i])
            w_c = jnp.exp(b_last[i] + m_prev[i] - m_new[i])
            upd = _mm_tn(k[i] * w_s, jnp.concatenate([v[i], ones], axis=1))
            cn_ref[i] = dup(w_c) * cn[i] + bdm2 * upd
            m_ref[i] = m_new[i]


def _scan_specs(b, n_ctx_steps, n_steps):
    fwd = lambda n: n
    bwd = lambda n: _step_block(1, n, n_ctx_steps, n_steps)
    tok = lambda w, f, c=0: pl.BlockSpec((b, STEP, w), lambda n: (0, f(n), c))
    tokT = lambda f: pl.BlockSpec((b, 128, STEP), lambda n: (0, 0, f(n)))
    full = lambda a: pl.BlockSpec(a.shape, lambda n: (0,) * a.ndim)
    return fwd, bwd, tok, tokT, full


def _mlstm(m_qkv, m_g, m_gT, wc, ind64, n_ctx_steps):
    b, t, _ = m_qkv.shape
    n_steps = t // STEP
    fwd, bwd, tok, tokT, full = _scan_specs(b, n_ctx_steps, n_steps)
    consts = [wc[k] for k in ("cm2", "cmw", "bdm", "e_lo", "e_hi")] + [ind64]
    out = jax.ShapeDtypeStruct((b, t, GROUP_W), F32)
    return pl.pallas_call(
        _mlstm_body,
        out_shape=[out, out],
        grid=(n_steps,),
        in_specs=[tok(3 * GROUP_W, fwd), tok(3 * GROUP_W, bwd), tok(128, fwd), tok(128, bwd), tokT(fwd), tokT(bwd)]
                 + [full(a) for a in consts],
        out_specs=[tok(GROUP_W, fwd), tok(GROUP_W, bwd)],
        scratch_shapes=[pltpu.VMEM((2 * b, GROUP_W, 2 * GROUP_W), F32), pltpu.VMEM((2 * b, 1, GROUP_W), F32)],
        compiler_params=_params("arbitrary"),
        name="mlstm",
    )(m_qkv, m_qkv, m_g, m_g, m_gT, m_gT, *consts)


def _gla_level_mats_np():
    L = CHUNK
    t = np.arange(L)
    sels, valids, pairs = [], [], []
    for rev in (False, True):
        pos_all = t if not rev else L - 1 - t
        eqs, eks, vq, vk, pm = [], [], [], [], []
        for j in range(6):
            w = 32 >> j
            par = pos_all // (2 * w)
            pos = pos_all % (2 * w)
            same = par[:, None] == par[None, :]
            late = pos >= w
            eq = same & late[:, None] & late[None, :] & (pos[None, :] <= pos[:, None])
            ek = same & (~late)[:, None] & (~late)[None, :] & (pos[None, :] > pos[:, None])
            eqs.append(eq)
            eks.append(ek)
            vq.append(late)
            vk.append(~late)
            pm.append(same & late[:, None] & (~late)[None, :])
        pm.append(np.eye(L, dtype=bool))
        sels.append(np.concatenate(eqs + eks, axis=0))
        valids.append(np.concatenate(vq + vk)[:, None])
        pairs.append(np.concatenate(pm, axis=0))
    f = lambda a: np.stack(a).astype(np.float32)
    return f(sels), f(valids), f(pairs)


def _gla_consts():
    sel, val, pair = _gla_level_mats_np()
    nq = 6 * CHUNK
    tile = lambda a: np.concatenate([a] * N_HEADS, axis=-1)
    selq, selk = sel[:, :nq], sel[:, nq:]
    selkt = np.stack([np.concatenate([selk[d, lv * CHUNK:(lv + 1) * CHUNK].T for lv in range(6)], axis=1)
                      for d in range(2)])
    valkt = np.stack([val[d, nq:, 0][None, :] for d in range(2)])
    rh = np.arange(N_HEADS * GLA_DK) // GLA_DK
    ch = np.arange(GROUP_W) // HEAD_DIM
    bdg = (rh[:, None] == ch[None, :]).astype(np.float32)
    return dict(selq=jnp.asarray(selq), valq=jnp.asarray(val[:, :nq]), selkt=jnp.asarray(selkt),
                valkt=jnp.asarray(valkt), pairw=jnp.asarray(tile(pair)), bdg=jnp.asarray(bdg, MXU_DT))


def _gla_body(qk_f, qk_b, v_f, v_b, la_f, la_b, cm_ref, selq_ref, valq_ref, selkt_ref, valkt_ref, pairw_ref,
              bdg_ref, bdm_ref, o_f, o_b, s_ref):
    @pl.when(pl.program_id(0) == 0)
    def _():
        s_ref[...] = jnp.zeros_like(s_ref)

    nb = qk_f.shape[0]
    streams = [(b, d) for b in range(nb) for d in range(2)]
    dirs = [d for _, d in streams]
    nk = N_HEADS * GLA_DK
    bdg = bdg_ref[...]
    bdm = bdm_ref[...]
    bdk = lambda xt: jnp.concatenate([xt.astype(MXU_DT)] * N_HEADS, axis=1) * bdg
    cm = [cm_ref[d] for d in dirs]
    selq = [selq_ref[d] for d in dirs]
    valq = [valq_ref[d] for d in dirs]
    selkt = [selkt_ref[d] for d in dirs]
    valkt = [valkt_ref[d] for d in dirs]
    pairw = [[pairw_ref[d, lv * CHUNK:(lv + 1) * CHUNK, :] for d in dirs] for lv in range(7)]
    ns = len(streams)
    items = [(si, j) for j in range(2) for si in range(ns)]
    idir = [dirs[si] for si, _ in items]
    offs = [(j if dirs[si] == 0 else 1 - j) * CHUNK for si, j in items]
    per = lambda lst: [lst[si] for si, _ in items]
    src = [(qk_f, v_f, la_f) if d == 0 else (qk_b, v_b, la_b) for d in idir]
    bidx = [streams[si][0] for si, _ in items]
    qk = [r[0][b, o:o + CHUNK, :] for b, r, o in zip(bidx, src, offs)]
    v = [r[1][b, o:o + CHUNK, :] for b, r, o in zip(bidx, src, offs)]
    la = [r[2][b, o:o + CHUNK, d * nk:(d + 1) * nk] for b, d, r, o in zip(bidx, idir, src, offs)]
    q = [x[:, :nk] * GLA_DK ** -0.5 for x in qk]
    k = [x[:, nk:] for x in qk]
    lat = [x.T for x in la]
    kt = [x.T for x in k]
    bcum = _each(_mm_sel, per(cm), la)
    exq = [jnp.exp(_mm_sel(s, x)) * vq for s, x, vq in zip(per(selq), la, per(valq))]
    exk = [jnp.exp(_mm_sel_r(x, s)) * vk for s, x, vk in zip(per(selkt), lat, per(valkt))]
    a = [p * _mm(qq, bdk(x)) for p, qq, x in zip(per(pairw[6]), q, kt)]
    for lv in range(6):
        sl = slice(lv * CHUNK, (lv + 1) * CHUNK)
        a = [aa + p * _mm(qq * eq[sl], bdk(x * ek[:, sl]))
             for aa, p, qq, eq, x, ek in zip(a, per(pairw[lv]), q, exq, kt, exk)]
    q_in = [qq * jnp.exp(bc) for qq, bc in zip(q, bcum)]
    b_last = [x[(CHUNK - 1 if d == 0 else 0):(CHUNK if d == 0 else 1), :] for x, d in zip(bcum, idir)]
    upd = [bdg.astype(F32) * _mm_tn(kk * jnp.exp(bl - bc), vv) for kk, bl, bc, vv in zip(k, b_last, bcum, v)]
    decay_t = [jnp.exp(jnp.sum(x, axis=1, keepdims=True)) for x in lat]
    st = [s_ref[si] for si in range(ns)]
    for j in range(2):
        cur = range(j * ns, (j + 1) * ns)
        o = [_mm(jnp.concatenate([q_in[i], a[i]], axis=1),
                 jnp.concatenate([s.astype(MXU_DT), _bd_rows(v[i], bdm)], axis=0)) for i, s in zip(cur, st)]
        for i, (b, d), oo in zip(cur, streams, o):
            (o_f if d == 0 else o_b)[b, offs[i]:offs[i] + CHUNK, :] = oo
        st = [decay_t[i] * s + upd[i] for i, s in zip(cur, st)]
    for si in range(ns):
        s_ref[si] = st[si]


def _gla(g_qk, g_vz, g_la, wc, gc, n_ctx_steps):
    b, t, _ = g_qk.shape
    n_steps = t // STEP
    fwd, bwd, tok, tokT, full = _scan_specs(b, n_ctx_steps, n_steps)
    consts = [wc["cm"]] + [gc[k] for k in ("selq", "valq", "selkt", "valkt", "pairw", "bdg")] + [wc["bdm"]]
    out = jax.ShapeDtypeStruct((b, t, GROUP_W), F32)
    w = 2 * N_HEADS * GLA_DK
    return pl.pallas_call(
        _gla_body,
        out_shape=[out, out],
        grid=(n_steps,),
        in_specs=[tok(w, fwd), tok(w, bwd), tok(GROUP_W, fwd), tok(GROUP_W, bwd),
                  tok(w, fwd), tok(w, bwd)] + [full(a) for a in consts],
        out_specs=[tok(GROUP_W, fwd), tok(GROUP_W, bwd)],
        scratch_shapes=[pltpu.VMEM((2 * b, N_HEADS * GLA_DK, GROUP_W), F32)],
        compiler_params=_params("arbitrary"),
        name="gla",
    )(g_qk, g_qk, g_vz, g_vz, g_la, g_la, *consts)


def _gdn_prep_body(prev_ref, cur_ref, next_ref, w_ref, ind_ref, o_ref, ext_ref, *, n_ctx_tiles, n_tiles):
    i = pl.program_id(1)
    tm = cur_ref.shape[0]
    first = (i == 0) | (i == n_ctx_tiles)
    last = (i == n_ctx_tiles - 1) | (i == n_tiles - 1)
    ext_ref[0:8, :] = jnp.where(first, 0.0, prev_ref[...])
    ext_ref[8:8 + tm, :] = cur_ref[...]
    ext_ref[8 + tm:16 + tm, :] = jnp.where(last, 0.0, next_ref[...])
    half = GDN_CONV // 2
    y = jnp.zeros(cur_ref.shape, F32)
    for j in range(GDN_CONV):
        y = y + ext_ref[8 - half + j:8 - half + j + tm, :] * w_ref[j:j + 1, :]
    y = jax.nn.silu(y)
    ind = ind_ref[...]
    q, k = y[:, :GROUP_W], y[:, GROUP_W:2 * GROUP_W]
    qn = q * lax.rsqrt(_mm_sel_r(q * q, ind) + EPS) * HEAD_DIM ** -0.5
    kn = k * lax.rsqrt(_mm_sel_r(k * k, ind) + EPS)
    o_ref[:, :GROUP_W] = qn
    o_ref[:, GROUP_W:2 * GROUP_W] = kn
    o_ref[:, 2 * GROUP_W:] = y[:, 2 * GROUP_W:]


def _gdn_prep(d_qkv, conv_w, ind64, n_ctx_tiles):
    b, t, w = d_qkv.shape
    tm = TOKEN_TILE
    n_tiles = t // tm
    r = tm // 8
    n8 = t // 8
    return pl.pallas_call(
        functools.partial(_gdn_prep_body, n_ctx_tiles=n_ctx_tiles, n_tiles=n_tiles),
        out_shape=jax.ShapeDtypeStruct((b, t, w), F32),
        grid=(b, n_tiles),
        in_specs=[pl.BlockSpec((None, 8, w), lambda bi, i: (bi, jnp.maximum(i * r - 1, 0), 0)),
                  pl.BlockSpec((None, tm, w), lambda bi, i: (bi, i, 0)),
                  pl.BlockSpec((None, 8, w), lambda bi, i: (bi, jnp.minimum((i + 1) * r, n8 - 1), 0)),
                  pl.BlockSpec(conv_w.shape, lambda bi, i: (0, 0)),
                  pl.BlockSpec(ind64.shape, lambda bi, i: (0, 0))],
        out_specs=pl.BlockSpec((None, tm, w), lambda bi, i: (bi, i, 0)),
        scratch_shapes=[pltpu.VMEM((tm + 16, w), F32)],
        compiler_params=_params("parallel", "parallel"),
        name="gdn_prep",
    )(d_qkv, d_qkv, d_qkv, conv_w, ind64)


def _wide_consts():
    cm = _cum_mats_np()
    eye = np.eye(CHUNK, dtype=np.float32)
    pair = _gla_level_mats_np()[2].reshape(2, 7, CHUNK, CHUNK)[:, :6]
    tile = lambda a: np.concatenate([a] * N_HEADS, axis=-1)
    hb = np.arange(GROUP_W) // HEAD_DIM
    bdm = (hb[:, None] == hb[None, :]).astype(np.float32)
    e_lo = np.zeros((2, 128, GROUP_W), np.float32)
    e_hi = np.zeros((2, 128, GROUP_W), np.float32)
    for d in range(2):
        for h in range(N_HEADS):
            e_lo[d, d * 4 + h, h * HEAD_DIM:(h + 1) * HEAD_DIM] = 1.0
            e_hi[d, 8 + d * 4 + h, h * HEAD_DIM:(h + 1) * HEAD_DIM] = 1.0
    z = np.zeros_like(cm)
    cm2 = np.concatenate([np.concatenate([cm, z], axis=2), np.concatenate([z, cm], axis=2)], axis=1)
    return dict(cm=jnp.asarray(cm), cm2=jnp.asarray(cm2), cmw=jnp.asarray(tile(cm)), strictw=jnp.asarray(tile(cm - eye)),
                eyew=jnp.asarray(tile(eye)), pairw=jnp.asarray(tile(pair).reshape(2, 6 * CHUNK, GROUP_W)),
                bdm=jnp.asarray(bdm, MXU_DT), e_lo=jnp.asarray(e_lo), e_hi=jnp.asarray(e_hi))


def _bd_rows(x, bdm):
    xb = x.astype(MXU_DT)
    return jnp.concatenate([xb] * N_HEADS, axis=0) * bdm


def _each(f, *lists):
    return [f(*a) for a in zip(*lists)]


def _gdn_body(qkv_f, qkv_b, g_f, g_b, gT_f, gT_b, cm2_ref, cmw_ref, strictw_ref, eyew_ref, pairw_ref, bdm_ref,
              elo_ref, ehi_ref, o_f, o_b, s_ref):
    @pl.when(pl.program_id(0) == 0)
    def _():
        s_ref[...] = jnp.zeros_like(s_ref)

    nb = qkv_f.shape[0]
    streams = [(b, d) for b in range(nb) for d in range(2)]
    dirs = [d for _, d in streams]
    bdm = bdm_ref[...]
    bdm2 = jnp.concatenate([bdm, bdm], axis=1)
    eyew = eyew_ref[...]
    cm2 = [cm2_ref[d] for d in dirs]
    cmw = [cmw_ref[d] for d in dirs]
    strictw = [strictw_ref[d] for d in dirs]
    pairw = [[pairw_ref[d, lv * CHUNK:(lv + 1) * CHUNK, :] for d in dirs] for lv in range(6)]
    elo = [elo_ref[d] for d in dirs]
    ehi = [ehi_ref[d] for d in dirs]
    g_all = [(g_f if d == 0 else g_b)[b] for b, d in streams]
    gt_all = [(gT_f if d == 0 else gT_b)[b] for b, d in streams]
    gc_all = _each(_mm_sel, cm2, g_all)
    gr_all = _each(_mm_sel_nt, gt_all, cm2)
    beta_all = _each(_mm_sel_r, g_all, elo)
    gw_all = _each(_mm_sel_r, gc_all, ehi)
    ns = len(streams)
    items = [(si, j) for j in range(2) for si in range(ns)]
    idir = [dirs[si] for si, _ in items]
    offs = [(j if dirs[si] == 0 else 1 - j) * CHUNK for si, j in items]
    per = lambda lst: [lst[si] for si, _ in items]
    qkv = [(qkv_f if d == 0 else qkv_b)[streams[si][0], o:o + CHUNK, :] for (si, _), d, o in zip(items, idir, offs)]
    gr = [x[:, o:o + CHUNK] for x, o in zip(per(gr_all), offs)]
    beta = [x[o:o + CHUNK] for x, o in zip(per(beta_all), offs)]
    gw = [x[o:o + CHUNK] for x, o in zip(per(gw_all), offs)]
    grow = [jnp.concatenate([x[8 + 4 * d + h:9 + 4 * d + h, :] for h in range(N_HEADS)], axis=1)
            for x, d in zip(gr, idir)]
    decay = [jnp.exp(jnp.where(m > 0, a - r, -jnp.inf)) for m, a, r in zip(per(cmw), gw, grow)]
    q = [x[:, :GROUP_W] for x in qkv]
    k = [x[:, GROUP_W:2 * GROUP_W] for x in qkv]
    v = [x[:, 2 * GROUP_W:] for x in qkv]
    kb = _each(lambda a, b_: a * b_, k, beta)
    bdk = [jnp.concatenate([x.T.astype(MXU_DT)] * N_HEADS, axis=1) * bdm for x in k]
    gq = _each(lambda a, b_, w: _mm(jnp.concatenate([a, b_], axis=0), w), kb, q, bdk)
    lower = [s * x[:CHUNK] * dc for s, x, dc in zip(per(strictw), gq, decay)]
    attn = [x[CHUNK:] * dc for x, dc in zip(gq, decay)]
    tw = [eyew - lo * p for lo, p in zip(lower, per(pairw[5]))]
    for lv in range(4, -1, -1):
        y = [_mm(lo * p, _bd_rows(t, bdm)) for lo, p, t in zip(lower, per(pairw[lv]), tw)]
        tw = [t - _mm(t, _bd_rows(yy, bdm)) for t, yy in zip(tw, y)]
    rhs = [jnp.concatenate([vv * b_, kk * jnp.exp(a)], axis=1).astype(MXU_DT)
           for vv, b_, kk, a in zip(v, beta, kb, gw)]
    uw = [_mm(t, jnp.concatenate([r] * N_HEADS, axis=0) * bdm2) for t, r in zip(tw, rhs)]
    q_in = [qq * jnp.exp(a) for qq, a in zip(q, gw)]
    g_last = [a[(CHUNK - 1 if d == 0 else 0):(CHUNK if d == 0 else 1), :] for a, d in zip(gw, idir)]
    k_out = [kk * jnp.exp(gl - a) for kk, gl, a in zip(k, g_last, gw)]
    st = [s_ref[si] for si in range(ns)]
    for j in range(2):
        cur = range(j * ns, (j + 1) * ns)
        v_new = [uw[i][:, :GROUP_W] - _mm(uw[i][:, GROUP_W:], s) for i, s in zip(cur, st)]
        o = [_mm(jnp.concatenate([q_in[i], attn[i]], axis=1),
                 jnp.concatenate([s.astype(MXU_DT), _bd_rows(vn, bdm)], axis=0))
             for i, s, vn in zip(cur, st, v_new)]
        for i, (b, d), oo in zip(cur, streams, o):
            (o_f if d == 0 else o_b)[b, offs[i]:offs[i] + CHUNK, :] = oo
        st = [jnp.exp(g_last[i]) * s + bdm.astype(F32) * _mm_tn(k_out[i], vn) for i, s, vn in zip(cur, st, v_new)]
    for si in range(ns):
        s_ref[si] = st[si]


def _gdn(d_qkv, d_g, d_gT, wc, n_ctx_steps):
    b, t, _ = d_qkv.shape
    n_steps = t // STEP
    fwd, bwd, tok, tokT, full = _scan_specs(b, n_ctx_steps, n_steps)
    consts = [wc[k] for k in ("cm2", "cmw", "strictw", "eyew", "pairw", "bdm", "e_lo", "e_hi")]
    out = jax.ShapeDtypeStruct((b, t, GROUP_W), F32)
    return pl.pallas_call(
        _gdn_body,
        out_shape=[out, out],
        grid=(n_steps,),
        in_specs=[tok(3 * GROUP_W, fwd), tok(3 * GROUP_W, bwd), tok(128, fwd), tok(128, bwd), tokT(fwd), tokT(bwd)]
                 + [full(a) for a in consts],
        out_specs=[tok(GROUP_W, fwd), tok(GROUP_W, bwd)],
        scratch_shapes=[pltpu.VMEM((2 * b, GROUP_W, GROUP_W), F32)],
        compiler_params=_params("arbitrary"),
        name="gdn",
    )(d_qkv, d_qkv, d_g, d_g, d_gT, d_gT, *consts)


def _rope_tables(n_ctx, n_lat):
    rows = n_lat // GRID_W
    row = np.repeat(np.arange(rows), GRID_W).astype(np.float32)
    col = np.tile(np.arange(GRID_W), rows).astype(np.float32)
    half = DA_QK // 2
    inv = np.power(ROPE_BASE, -np.arange(0, half, 2, dtype=np.float32) / half).astype(np.float32)

    def tab(p):
        ang = p[:, None] * inv
        ang = np.concatenate([ang, ang], axis=-1)
        return np.cos(ang), np.sin(ang)

    cr, sr = tab(row)
    cc, sc = tab(col)
    cos = np.concatenate([cr, cc], -1)
    sin = np.concatenate([sr, sc], -1)
    cos = np.concatenate([np.ones((n_ctx, DA_QK), np.float32), cos], 0)
    sin = np.concatenate([np.zeros((n_ctx, DA_QK), np.float32), sin], 0)
    reps = GROUP_W // DA_QK
    cos, sin = np.tile(cos, (1, reps)), np.tile(sin, (1, reps))
    first = (np.arange(GROUP_W) % 16) < 8
    sin_a = np.where(first, -sin, 0.0)
    sin_b = np.where(first, 0.0, sin)
    return jnp.asarray(cos, F32), jnp.asarray(sin_a, F32), jnp.asarray(sin_b, F32)


def _diff_operands(qk, v, cos, sa, sb, nw_ref, ind, qt_out, k_out, vt_out):
    def norm_rope(x, w):
        xn = x * lax.rsqrt(_mm_sel_r(x * x, ind) * (1.0 / DA_QK) + EPS) * w
        return xn * cos + pltpu.roll(xn, GROUP_W - 8, 1) * sa + pltpu.roll(xn, 8, 1) * sb

    q = norm_rope(qk[:, :GROUP_W], nw_ref[0:1, :]) * (DA_QK ** -0.5 * math.log2(math.e))
    k = norm_rope(qk[:, GROUP_W:], nw_ref[1:2, :])
    qt_out[...] = q.T.astype(qt_out.dtype)
    for h in range(N_HEADS):
        k_out[h] = k[:, h * HEAD_DIM:(h + 1) * HEAD_DIM].astype(k_out.dtype)
    vt = v.T
    tm = vt.shape[1]
    tail = jnp.concatenate([jnp.ones((8, tm), F32), jnp.zeros((V_ROWS - HEAD_DIM - 8, tm), F32)], axis=0)
    for h in range(N_HEADS):
        vt_out[h] = jnp.concatenate([vt[h * HEAD_DIM:(h + 1) * HEAD_DIM], tail], axis=0).astype(vt_out.dtype)


def _diff_attn_body(lam_ref, qt_ref, k_ref, vp_ref, vc_ref, o_ref, acc_ref, m_ref, e_ref, *, lam_init):
    j = pl.program_id(3)

    @pl.when(j == 0)
    def _():
        acc_ref[...] = jnp.zeros_like(acc_ref)
        m_ref[...] = jnp.full_like(m_ref, -jnp.inf)
        e_ref[...] = jnp.zeros_like(e_ref)

    vp = vp_ref[...]
    kk = k_ref[...]
    qt = qt_ref[...]
    part = lax.broadcasted_iota(jnp.int32, (2 * DA_QK, 1), 0) // DA_QK
    s = [jnp.dot(kk, jnp.where(part == p, qt, jnp.zeros_like(qt)), preferred_element_type=F32)
         for p in range(2)]
    pv = [jnp.dot(vp, e_ref[p], preferred_element_type=F32) for p in range(2)]
    for p in range(2):
        m_old = m_ref[p]
        m_new = jnp.maximum(m_old, jnp.max(s[p], axis=0, keepdims=True))
        e_ref[p] = jnp.exp2(s[p] - m_new).astype(e_ref.dtype)
        acc_ref[p] = jnp.exp2(m_old - m_new) * (acc_ref[p] + pv[p])
        m_ref[p] = m_new

    @pl.when(j == pl.num_programs(3) - 1)
    def _():
        lam = lam_ref[...]
        lam_full = (jnp.exp(jnp.sum(lam[0:1] * lam[1:2], keepdims=True))
                    - jnp.exp(jnp.sum(lam[2:3] * lam[3:4], keepdims=True)) + lam_init)
        vc = vc_ref[...]
        a0 = acc_ref[0] + jnp.dot(vc, e_ref[0], preferred_element_type=F32)
        a1 = acc_ref[1] + jnp.dot(vc, e_ref[1], preferred_element_type=F32)
        ot = (a0[:HEAD_DIM] / a0[HEAD_DIM:HEAD_DIM + 1]
              - lam_full * (a1[:HEAD_DIM] / a1[HEAD_DIM:HEAD_DIM + 1]))
        o_ref[...] = ot.T


def _diff_attn(lam, qt, k, vt, lam_init, n_keys, tq, tk):
    b, _, n_q = qt.shape
    return pl.pallas_call(
        functools.partial(_diff_attn_body, lam_init=lam_init),
        out_shape=jax.ShapeDtypeStruct((b, N_HEADS, n_q, HEAD_DIM), F32),
        grid=(b, N_HEADS, n_q // tq, n_keys // tk),
        in_specs=[pl.BlockSpec(lam.shape, lambda bi, h, i, j: (0, 0)),
                  pl.BlockSpec((None, 2 * DA_QK, tq), lambda bi, h, i, j: (bi, h, i)),
                  pl.BlockSpec((None, None, tk, HEAD_DIM), lambda bi, h, i, j: (bi, h, j, 0)),
                  pl.BlockSpec((None, None, V_ROWS, tk), lambda bi, h, i, j: (bi, h, 0, jnp.maximum(j - 1, 0))),
                  pl.BlockSpec((None, None, V_ROWS, tk), lambda bi, h, i, j: (bi, h, 0, j))],
        out_specs=pl.BlockSpec((None, None, tq, HEAD_DIM), lambda bi, h, i, j: (bi, h, i, 0)),
        scratch_shapes=[pltpu.VMEM((2, V_ROWS, tq), F32), pltpu.VMEM((2, 1, tq), F32),
                        pltpu.VMEM((2, tk, tq), MXU_DT)],
        compiler_params=_params("parallel", "parallel", "parallel", "arbitrary"),
        name="diff_attn",
    )(lam, qt, k, vt, vt)


def _finish_body(x_ref, mhf_ref, mhb_ref, moz_ref, ao_ref, az_ref, ghf_ref, ghb_ref, gz_ref, dhf_ref, dhb_ref, dz_ref,
                 nw_ref, ind_ref, wout_ref, gt_ref, o_ref, *, lam_init):
    ind = ind_ref[...]

    def head_norm(hh, w):
        return hh * lax.rsqrt(_mm_sel_r(hh * hh, ind) * (1.0 / HEAD_DIM) + EPS) * w

    moz = moz_ref[...]
    ym = (head_norm(mhf_ref[...] + mhb_ref[...], nw_ref[0:1, :]) * jax.nn.sigmoid(moz[:, :GROUP_W])
          * jax.nn.silu(moz[:, GROUP_W:]))
    ao = jnp.concatenate([ao_ref[h] for h in range(N_HEADS)], axis=1)
    ya = head_norm(ao, nw_ref[1:2, :]) * (1.0 - lam_init) * jax.nn.silu(az_ref[...])
    yg = head_norm(ghf_ref[...] + ghb_ref[...], nw_ref[2:3, :]) * jax.nn.silu(gz_ref[...])
    yd = head_norm(dhf_ref[...] + dhb_ref[...], nw_ref[3:4, :]) * jax.nn.silu(dz_ref[...])
    y = jnp.concatenate([ym, ya, yg, yd], axis=1).astype(MXU_DT)
    o_ref[...] = x_ref[...] + gt_ref[...] * jnp.dot(y, wout_ref[...], preferred_element_type=F32)


def _finish(xs, mh, m_oz, ao, a_z, gh, g_vz, dh, d_z, nw, ind64, wout, gtsel, lam_init, n_ctx_tiles, first_tile):
    b, t, d = xs.shape
    tm = TOKEN_TILE
    f0 = first_tile
    tok = lambda w, c=0: pl.BlockSpec((None, tm, w), lambda bi, i: (bi, i + f0, c))
    g = tok(GROUP_W)
    full = lambda a: pl.BlockSpec(a.shape, lambda bi, i: (0,) * a.ndim)
    return pl.pallas_call(
        functools.partial(_finish_body, lam_init=lam_init),
        out_shape=jax.ShapeDtypeStruct((b, t - f0 * tm, d), F32),
        grid=(b, t // tm - f0),
        in_specs=[tok(d), g, g, tok(2 * GROUP_W),
                  pl.BlockSpec((None, N_HEADS, tm, HEAD_DIM), lambda bi, i: (bi, 0, i + f0, 0)),
                  g, g, g, tok(GROUP_W, 1), g, g, g,
                  full(nw), full(ind64), full(wout),
                  pl.BlockSpec((None, None, 1, d),
                               lambda bi, i: (bi, (i + f0 >= n_ctx_tiles).astype(jnp.int32), 0, 0))],
        out_specs=pl.BlockSpec((None, tm, d), lambda bi, i: (bi, i, 0)),
        compiler_params=_params("parallel", "parallel"),
        name="finish",
    )(xs, mh[0], mh[1], m_oz, ao, a_z, gh[0], gh[1], g_vz, dh[0], dh[1], d_z, nw, ind64, wout, gtsel)


def kernel(x, c, ctx, c_ctx, norm_w, w_mod, b_mod, w_in, w_out, mlstm_b_i, mlstm_b_f, mlstm_norm, diff_q_norm,
           diff_k_norm, diff_lambda, diff_norm, gla_w_up, gla_b, gla_norm, gdn_conv, gdn_a_log, gdn_dt_bias,
           gdn_norm):
    bsz, n_lat, d = x.shape
    n_ctx = ctx.shape[1]
    depth = w_in.shape[0]
    t = n_ctx + n_lat
    assert n_ctx % TOKEN_TILE == 0 and n_lat % TOKEN_TILE == 0 and n_lat % GRID_W == 0
    assert n_lat % ATTN_TQ == 0 and t % ATTN_TK == 0 and n_ctx % STEP == 0
    n_ctx_tiles = n_ctx // TOKEN_TILE
    n_ctx_steps = n_ctx // STEP

    cc = jnp.concatenate([c, c_ctx[None, :], jnp.zeros((8 - (bsz + 1) % 8, d), F32)], axis=0)
    mod = _modulation(cc, w_mod, b_mod)

    wide = _wide_consts()
    gla_consts = _gla_consts()
    ind64 = _group_ones(GROUP_W, HEAD_DIM)
    ind32 = _group_ones(GROUP_W, DA_QK)
    tables = _rope_tables(n_ctx, n_lat)
    pad = lambda a, n: jnp.concatenate([a, jnp.zeros(a.shape[:-1] + (n - a.shape[-1],), a.dtype)], axis=-1)

    xs = jnp.concatenate([ctx, x], axis=1)
    for l in range(depth):
        lam_init = 0.8 - 0.6 * math.exp(-0.3 * l)
        sh, sc, gt = jnp.split(mod[l], 3, axis=-1)
        lat = jnp.stack([sh[:bsz], sc[:bsz]], axis=1)
        cx = jnp.broadcast_to(jnp.stack([sh[bsz], sc[bsz]], axis=0)[None], (bsz, 2, d))
        modsel = jnp.stack([cx, lat], axis=1)
        gtsel = jnp.stack([jnp.broadcast_to(gt[bsz][None], (bsz, d)), gt[:bsz]], axis=1)[:, :, None, :]
        wp, wg = _permute_w_in(w_in[l])
        pm = pad(jnp.concatenate([mlstm_b_i[l].reshape(1, 8), mlstm_b_f[l].reshape(1, 8)], axis=1), 128)
        pm = jnp.concatenate([pm, jnp.zeros((7, 128), F32)], axis=0)
        z8 = jnp.zeros((1, 8), F32)
        pd = jnp.concatenate([pad(jnp.concatenate([z8, gdn_a_log[l].reshape(1, 8)], axis=1), 128),
                              pad(jnp.concatenate([z8, gdn_dt_bias[l].reshape(1, 8)], axis=1), 128),
                              jnp.zeros((6, 128), F32)], axis=0)
        wup = jnp.zeros((128, 256), F32)
        wup = wup.at[0:GLA_RANK, 0:128].set(gla_w_up[l, 0]).at[GLA_RANK:2 * GLA_RANK, 128:256].set(gla_w_up[l, 1])
        bgk = gla_b[l].reshape(1, 256)

        qkn = jnp.stack([jnp.tile(diff_q_norm[l], GROUP_W // DA_QK), jnp.tile(diff_k_norm[l], GROUP_W // DA_QK)])
        (m_qkv, m_oz, m_g, m_gT, aqt, ak, avt, a_z, g_qk, g_vz, g_la, d_qkv, d_z, d_g, d_gT) = _inproj(
            xs, norm_w[l][None, :], modsel, wp, wg, pm, pd, wup.astype(MXU_DT), bgk, tables, qkn, ind32,
            n_ctx_tiles)

        mh = _mlstm(m_qkv, m_g, m_gT, wide, ind64, n_ctx_steps)

        lam = pad(diff_lambda[l], 128)
        ao_ctx = _diff_attn(lam, aqt[:, :, :n_ctx], ak, avt, lam_init, n_ctx, n_ctx, n_ctx)
        ao_lat = _diff_attn(lam, aqt[:, :, n_ctx:], ak, avt, lam_init, t, ATTN_TQ, ATTN_TK)
        ao = jnp.concatenate([ao_ctx, ao_lat], axis=2)

        gh = _gla(g_qk, g_vz, g_la, wide, gla_consts, n_ctx_steps)

        dq = _gdn_prep(d_qkv, gdn_conv[l], ind64, n_ctx_tiles)
        dh = _gdn(dq, d_g, d_gT, wide, n_ctx_steps)

        nws = jnp.stack([mlstm_norm[l], diff_norm[l], gla_norm[l], gdn_norm[l]])
        xs = _finish(xs, mh, m_oz, ao, a_z, gh, g_vz, dh, d_z, nws, ind64, w_out[l].astype(MXU_DT),
                     gtsel, lam_init, n_ctx_tiles, n_ctx_tiles if l == depth - 1 else 0)
    return xs
```

```python
import functools
import math

import jax
import jax.numpy as jnp
import numpy as np
from jax import lax
from jax.experimental import pallas as pl
from jax.experimental.pallas import tpu as pltpu

F32 = jnp.float32
MXU_DT = jnp.bfloat16
EPS = 1e-6
N_HEADS = 4
HEAD_DIM = 64
GROUP_W = N_HEADS * HEAD_DIM
CHUNK = 64
STEP = 2 * CHUNK
DA_QK = 32
GLA_DK = 32
GLA_RANK = 16
GLA_TAU = 16.0
GDN_CONV = 5
GRID_W = 64
ROPE_BASE = 10000.0
TOKEN_TILE = 256
ATTN_TQ = 2048
ATTN_TK = 768
V_ROWS = 80
VMEM_LIMIT = 48 * 1024 * 1024


def _mm(a, b):
    return jnp.dot(a.astype(MXU_DT), b.astype(MXU_DT), preferred_element_type=F32)


def _mm_nt(a, b):
    return lax.dot_general(a.astype(MXU_DT), b.astype(MXU_DT), (((1,), (1,)), ((), ())),
                           preferred_element_type=F32)


def _mm_tn(a, b):
    return lax.dot_general(a.astype(MXU_DT), b.astype(MXU_DT), (((0,), (0,)), ((), ())),
                           preferred_element_type=F32)


def _split2(x):
    hi = x.astype(MXU_DT)
    lo = (x - hi.astype(F32)).astype(MXU_DT)
    return hi, lo


def _mm_sel(sel, x):
    hi, lo = _split2(x)
    s = sel.astype(MXU_DT)
    return (jnp.dot(s, hi, preferred_element_type=F32) + jnp.dot(s, lo, preferred_element_type=F32))


def _mm_sel_r(x, sel):
    hi, lo = _split2(x)
    s = sel.astype(MXU_DT)
    return (jnp.dot(hi, s, preferred_element_type=F32) + jnp.dot(lo, s, preferred_element_type=F32))


def _mm_sel_nt(x, sel):
    hi, lo = _split2(x)
    s = sel.astype(MXU_DT)
    dn = (((1,), (1,)), ((), ()))
    return (lax.dot_general(hi, s, dn, preferred_element_type=F32)
            + lax.dot_general(lo, s, dn, preferred_element_type=F32))


def _group_ones(width, group):
    g = np.arange(width) // group
    return jnp.asarray((g[:, None] == g[None, :]).astype(np.float32))


def _params(*sem):
    return pltpu.CompilerParams(dimension_semantics=sem, vmem_limit_bytes=VMEM_LIMIT)


def _mod_body(cc_ref, w_ref, b_ref, o_ref):
    o_ref[...] = _mm(jax.nn.silu(cc_ref[...]), w_ref[...]) + b_ref[...]


def _modulation(cc, w_mod, b_mod):
    depth, d, d3 = w_mod.shape
    tn = 1024
    return pl.pallas_call(
        _mod_body,
        out_shape=jax.ShapeDtypeStruct((depth, cc.shape[0], d3), F32),
        grid=(depth, d3 // tn),
        in_specs=[pl.BlockSpec(cc.shape, lambda l, j: (0, 0)),
                  pl.BlockSpec((None, d, tn), lambda l, j: (l, 0, j)),
                  pl.BlockSpec((None, 1, tn), lambda l, j: (l, 0, j))],
        out_specs=pl.BlockSpec((None, cc.shape[0], tn), lambda l, j: (l, 0, j)),
        compiler_params=_params("parallel", "parallel"),
        name="modulation",
    )(cc, w_mod, b_mod.reshape(depth, 1, d3))


_SEG = dict(m_qkv=(0, 768), m_oz=(768, 1280), m_g=(1280, 1408), a_qk=(1408, 1920), a_vz=(1920, 2432),
            g_qk=(2432, 2688), g_vz=(2688, 3200), g_r=(3200, 3328), d_qkv=(3328, 4096), d_z=(4096, 4352),
            d_g=(4352, 4480))
P_PAD = 4480


def _permute_w_in(w):
    d = w.shape[0]
    z = lambda n: jnp.zeros((d, n), w.dtype)
    m0, a0, g0, d0 = 0, 1296, 2320, 3120
    cols = [w[:, m0:m0 + 1280], w[:, m0 + 1280:m0 + 1296], z(112),
            w[:, a0:a0 + 512], w[:, a0 + 512:a0 + 1024],
            w[:, g0:g0 + 256], w[:, g0 + 256:g0 + 768], w[:, g0 + 768:g0 + 800], z(96),
            w[:, d0:d0 + 768], w[:, d0 + 768:d0 + 1024], w[:, d0 + 1024:d0 + 1040], z(112)]
    wp = jnp.concatenate(cols, axis=1)
    wg = jnp.concatenate([wp[:, 1280:1408], wp[:, 4352:4480]], axis=1).T
    return wp.astype(MXU_DT), wg.astype(MXU_DT)


def _gate_fns(xm, xd, pm, pd, lane):
    gm = jnp.where(lane < 8, xm + pm[0], jax.nn.log_sigmoid(xm + pm[0]))
    gd = jnp.where(lane < 8, jax.nn.sigmoid(xd), -jnp.exp(pd[0]) * jax.nn.softplus(xd + pd[1]))
    return gm, gd


def _inproj_body(x_ref, nw_ref, mod_ref, wp_ref, wg_ref, pm_ref, pd_ref, pmT_ref, pdT_ref, wup_ref, bgk_ref,
                 cos_ref, sa_ref, sb_ref, qkn_ref, ind32_ref,
                 m_qkv, m_oz, m_g, m_gT, a_qt, a_k, a_vt, a_z, g_qk, g_vz, g_la, d_qkv, d_z, d_g, d_gT):
    x = x_ref[...]
    h = x * lax.rsqrt(jnp.mean(x * x, axis=-1, keepdims=True) + EPS) * nw_ref[...]
    h = (h * (1.0 + mod_ref[1:2, :]) + mod_ref[0:1, :]).astype(MXU_DT)
    u = jnp.dot(h, wp_ref[...], preferred_element_type=F32)
    seg = lambda name: u[:, _SEG[name][0]:_SEG[name][1]]
    m_qkv[...] = seg("m_qkv")
    m_oz[...] = seg("m_oz")
    avz = seg("a_vz")
    a_z[...] = avz[:, GROUP_W:]
    _diff_operands(seg("a_qk"), avz[:, :GROUP_W], cos_ref[...], sa_ref[...], sb_ref[...], qkn_ref, ind32_ref[...],
                   a_qt, a_k, a_vt)
    g_qk[...] = seg("g_qk")
    g_vz[...] = seg("g_vz")
    d_qkv[...] = seg("d_qkv")
    d_z[...] = seg("d_z")
    lane = lax.broadcasted_iota(jnp.int32, (1, 128), 1)
    gm, gd = _gate_fns(seg("m_g"), seg("d_g"), (pm_ref[0:1, :],), (pd_ref[0:1, :], pd_ref[1:2, :]), lane)
    m_g[...] = gm
    d_g[...] = gd
    ut = lax.dot_general(wg_ref[...], h, (((1,), (1,)), ((), ())), preferred_element_type=F32)
    row = lax.broadcasted_iota(jnp.int32, (128, 1), 0)
    gmt, gdt = _gate_fns(ut[0:128], ut[128:256], (pmT_ref[:, 0:1],), (pdT_ref[:, 0:1], pdT_ref[:, 1:2]), row)
    m_gT[...] = gmt
    d_gT[...] = gdt
    gk = _mm(seg("g_r"), wup_ref[...]) + bgk_ref[...]
    g_la[...] = jax.nn.log_sigmoid(gk) / GLA_TAU


def _inproj(xs, nw, modsel, wp, wg, pm, pd, wup, bgk, tables, qkn, ind32, n_ctx_tiles):
    b, t, d = xs.shape
    tm = TOKEN_TILE
    tok = lambda w: pl.BlockSpec((None, tm, w), lambda bi, i: (bi, i, 0))
    tokT = pl.BlockSpec((None, 128, tm), lambda bi, i: (bi, 0, i))
    full = lambda a: pl.BlockSpec(a.shape, lambda bi, i: (0,) * a.ndim)
    tab = pl.BlockSpec((tm, GROUP_W), lambda bi, i: (i, 0))
    attn_shapes = [jax.ShapeDtypeStruct((b, GROUP_W, t), MXU_DT),
                   jax.ShapeDtypeStruct((b, N_HEADS, t, HEAD_DIM), MXU_DT),
                   jax.ShapeDtypeStruct((b, N_HEADS, V_ROWS, t), MXU_DT)]
    attn_specs = [pl.BlockSpec((None, GROUP_W, tm), lambda bi, i: (bi, 0, i)),
                  pl.BlockSpec((None, N_HEADS, tm, HEAD_DIM), lambda bi, i: (bi, 0, i, 0)),
                  pl.BlockSpec((None, N_HEADS, V_ROWS, tm), lambda bi, i: (bi, 0, 0, i))]
    widths = [768, 512, 128, None, "attn", 256, 256, 512, 256, 768, 256, 128, None]
    out_shape, out_specs = [], []
    for w in widths:
        if w is None:
            out_shape.append(jax.ShapeDtypeStruct((b, 128, t), F32))
            out_specs.append(tokT)
        elif w == "attn":
            out_shape += attn_shapes
            out_specs += attn_specs
        else:
            out_shape.append(jax.ShapeDtypeStruct((b, t, w), F32))
            out_specs.append(tok(w))
    pmT, pdT = pm.T, pd.T
    cos, sa, sb = tables
    return pl.pallas_call(
        _inproj_body,
        out_shape=out_shape,
        grid=(b, t // tm),
        in_specs=[tok(d), full(nw),
                  pl.BlockSpec((None, None, 2, d), lambda bi, i: (bi, (i >= n_ctx_tiles).astype(jnp.int32), 0, 0)),
                  full(wp), full(wg), full(pm), full(pd), full(pmT), full(pdT), full(wup), full(bgk),
                  tab, tab, tab, full(qkn), full(ind32)],
        out_specs=out_specs,
        compiler_params=_params("parallel", "parallel"),
        name="inproj",
    )(xs, nw, modsel, wp, wg, pm, pd, pmT, pdT, wup, bgk, cos, sa, sb, qkn, ind32)


def _cum_mats_np():
    i = np.arange(CHUNK)
    fwd = (i[None, :] <= i[:, None]).astype(np.float32)
    return np.stack([fwd, fwd.T])


def _step_block(d, n, n_ctx_steps, n_steps):
    bwd = jnp.where(n < n_ctx_steps, n_ctx_steps - 1 - n, n_steps - 1 - (n - n_ctx_steps))
    return jnp.where(d == 0, n, bwd)


def _head_rowmax(x):
    return jnp.concatenate(
        [jnp.broadcast_to(jnp.max(x[:, h * HEAD_DIM:(h + 1) * HEAD_DIM], axis=1, keepdims=True), (CHUNK, HEAD_DIM))
         for h in range(N_HEADS)], axis=1)


def _mlstm_body(qkv_f, qkv_b, g_f, g_b, gT_f, gT_b, cm2_ref, cmw_ref, bdm_ref, elo_ref, ehi_ref, ind_ref,
                o_f, o_b, cn_ref, m_ref):
    @pl.when(pl.program_id(0) == 0)
    def _():
        cn_ref[...] = jnp.zeros_like(cn_ref)
        m_ref[...] = jnp.zeros_like(m_ref)

    nb = qkv_f.shape[0]
    streams = [(b, d) for b in range(nb) for d in range(2)]
    dirs = [d for _, d in streams]
    bdm = bdm_ref[...]
    bdm2 = jnp.concatenate([bdm, bdm], axis=1).astype(F32)
    ind = ind_ref[...].astype(MXU_DT)
    ones = jnp.ones((CHUNK, GROUP_W), F32)
    dup = lambda a: jnp.concatenate([a, a], axis=1)
    cm2 = [cm2_ref[d] for d in dirs]
    cmw = [cmw_ref[d] for d in dirs]
    elo = [elo_ref[d] for d in dirs]
    ehi = [ehi_ref[d] for d in dirs]
    g_all = [(g_f if d == 0 else g_b)[b] for b, d in streams]
    gt_all = [(gT_f if d == 0 else gT_b)[b] for b, d in streams]
    bc_all = _each(_mm_sel, cm2, g_all)
    br_all = _each(_mm_sel_nt, gt_all, cm2)
    bw_all = _each(_mm_sel_r, bc_all, ehi)
    iw_all = _each(_mm_sel_r, g_all, elo)
    for j in range(2):
        offs = [(j if d == 0 else 1 - j) * CHUNK for d in dirs]
        qkv = [(qkv_f if d == 0 else qkv_b)[b, o:o + CHUNK, :] for (b, d), o in zip(streams, offs)]
        gt = [x[:, o:o + CHUNK] for x, o in zip(gt_all, offs)]
        br = [x[:, o:o + CHUNK] for x, o in zip(br_all, offs)]
        bw = [x[o:o + CHUNK] for x, o in zip(bw_all, offs)]
        iw = [x[o:o + CHUNK] for x, o in zip(iw_all, offs)]
        rw = [jnp.concatenate([a[4 * d + h:4 * d + h + 1, :] - c[8 + 4 * d + h:9 + 4 * d + h, :]
                               for h in range(N_HEADS)], axis=1) for a, c, d in zip(gt, br, dirs)]
        dmat = [jnp.where(m > 0, a + r, -jnp.inf) for m, a, r in zip(cmw, bw, rw)]
        m_prev = [m_ref[i] for i in range(len(streams))]
        inter = _each(lambda a, m: a + m, bw, m_prev)
        m_t = [jnp.maximum(a, _head_rowmax(x)) for a, x in zip(inter, dmat)]
        q = [x[:, :GROUP_W] for x in qkv]
        k = [x[:, GROUP_W:2 * GROUP_W] * HEAD_DIM ** -0.5 for x in qkv]
        v = [x[:, 2 * GROUP_W:] for x in qkv]
        bdk = [jnp.concatenate([x.T.astype(MXU_DT)] * N_HEADS, axis=1) * bdm for x in k]
        s = [_mm(qq, w) * jnp.exp(x - m) for qq, w, x, m in zip(q, bdk, dmat, m_t)]
        w_inter = [jnp.exp(a - m) for a, m in zip(inter, m_t)]
        cn = [cn_ref[i] for i in range(len(streams))]
        acc = [dup(wi) * _mm(qq, c) + _mm(ss, jnp.concatenate([_bd_rows(vv, bdm), ind], axis=1))
               for wi, qq, c, ss, vv in zip(w_inter, q, cn, s, v)]
        hout = [a[:, :GROUP_W] / jnp.maximum(jnp.abs(a[:, GROUP_W:]), jnp.exp(-m)) for a, m in zip(acc, m_t)]
        last = [CHUNK - 1 if d == 0 else 0 for d in dirs]
        b_last = [a[r:r + 1, :] for a, r in zip(bw, last)]
        ds = [bl - a + i_ for bl, a, i_ in zip(b_last, bw, iw)]
        m_new = [jnp.maximum(bl + m, jnp.max(x, axis=0, keepdims=True)) for bl, m, x in zip(b_last, m_prev, ds)]
        for i, ((b, d), off) in enumerate(zip(streams, offs)):
            (o_f if d == 0 else o_b)[b, off:off + CHUNK, :] = hout[i]
            w_s = jnp.exp(ds[i] - m_new[i])
            w_c = jnp.exp(b_last[i] + m_prev[i] - m_new[i])
            upd = _mm_tn(k[i] * w_s, jnp.concatenate([v[i], ones], axis=1))
            cn_ref[i] = dup(w_c) * cn[i] + bdm2 * upd
            m_ref[i] = m_new[i]


def _scan_specs(b, n_ctx_steps, n_steps):
    fwd = lambda n: n
    bwd = lambda n: _step_block(1, n, n_ctx_steps, n_steps)
    tok = lambda w, f, c=0: pl.BlockSpec((b, STEP, w), lambda n: (0, f(n), c))
    tokT = lambda f: pl.BlockSpec((b, 128, STEP), lambda n: (0, 0, f(n)))
    full = lambda a: pl.BlockSpec(a.shape, lambda n: (0,) * a.ndim)
    return fwd, bwd, tok, tokT, full


def _mlstm(m_qkv, m_g, m_gT, wc, ind64, n_ctx_steps):
    b, t, _ = m_qkv.shape
    n_steps = t // STEP
    fwd, bwd, tok, tokT, full = _scan_specs(b, n_ctx_steps, n_steps)
    consts = [wc[k] for k in ("cm2", "cmw", "bdm", "e_lo", "e_hi")] + [ind64]
    out = jax.ShapeDtypeStruct((b, t, GROUP_W), F32)
    return pl.pallas_call(
        _mlstm_body,
        out_shape=[out, out],
        grid=(n_steps,),
        in_specs=[tok(3 * GROUP_W, fwd), tok(3 * GROUP_W, bwd), tok(128, fwd), tok(128, bwd), tokT(fwd), tokT(bwd)]
                 + [full(a) for a in consts],
        out_specs=[tok(GROUP_W, fwd), tok(GROUP_W, bwd)],
        scratch_shapes=[pltpu.VMEM((2 * b, GROUP_W, 2 * GROUP_W), F32), pltpu.VMEM((2 * b, 1, GROUP_W), F32)],
        compiler_params=_params("arbitrary"),
        name="mlstm",
    )(m_qkv, m_qkv, m_g, m_g, m_gT, m_gT, *consts)


def _gla_level_mats_np():
    L = CHUNK
    t = np.arange(L)
    sels, valids, pairs = [], [], []
    for rev in (False, True):
        pos_all = t if not rev else L - 1 - t
        eqs, eks, vq, vk, pm = [], [], [], [], []
        for j in range(6):
            w = 32 >> j
            par = pos_all // (2 * w)
            pos = pos_all % (2 * w)
            same = par[:, None] == par[None, :]
            late = pos >= w
            eq = same & late[:, None] & late[None, :] & (pos[None, :] <= pos[:, None])
            ek = same & (~late)[:, None] & (~late)[None, :] & (pos[None, :] > pos[:, None])
            eqs.append(eq)
            eks.append(ek)
            vq.append(late)
            vk.append(~late)
            pm.append(same & late[:, None] & (~late)[None, :])
        pm.append(np.eye(L, dtype=bool))
        sels.append(np.concatenate(eqs + eks, axis=0))
        valids.append(np.concatenate(vq + vk)[:, None])
        pairs.append(np.concatenate(pm, axis=0))
    f = lambda a: np.stack(a).astype(np.float32)
    return f(sels), f(valids), f(pairs)


def _gla_consts():
    sel, val, pair = _gla_level_mats_np()
    nq = 6 * CHUNK
    tile = lambda a: np.concatenate([a] * N_HEADS, axis=-1)
    selq, selk = sel[:, :nq], sel[:, nq:]
    selkt = np.stack([np.concatenate([selk[d, lv * CHUNK:(lv + 1) * CHUNK].T for lv in range(6)], axis=1)
                      for d in range(2)])
    valkt = np.stack([val[d, nq:, 0][None, :] for d in range(2)])
    rh = np.arange(N_HEADS * GLA_DK) // GLA_DK
    ch = np.arange(GROUP_W) // HEAD_DIM
    bdg = (rh[:, None] == ch[None, :]).astype(np.float32)
    return dict(selq=jnp.asarray(selq), valq=jnp.asarray(val[:, :nq]), selkt=jnp.asarray(selkt),
                valkt=jnp.asarray(valkt), pairw=jnp.asarray(tile(pair)), bdg=jnp.asarray(bdg, MXU_DT))


def _gla_body(qk_f, qk_b, v_f, v_b, la_f, la_b, cm_ref, selq_ref, valq_ref, selkt_ref, valkt_ref, pairw_ref,
              bdg_ref, bdm_ref, o_f, o_b, s_ref):
    @pl.when(pl.program_id(0) == 0)
    def _():
        s_ref[...] = jnp.zeros_like(s_ref)

    nb = qk_f.shape[0]
    streams = [(b, d) for b in range(nb) for d in range(2)]
    dirs = [d for _, d in streams]
    nk = N_HEADS * GLA_DK
    bdg = bdg_ref[...]
    bdm = bdm_ref[...]
    bdk = lambda xt: jnp.concatenate([xt.astype(MXU_DT)] * N_HEADS, axis=1) * bdg
    cm = [cm_ref[d] for d in dirs]
    selq = [selq_ref[d] for d in dirs]
    valq = [valq_ref[d] for d in dirs]
    selkt = [selkt_ref[d] for d in dirs]
    valkt = [valkt_ref[d] for d in dirs]
    pairw = [[pairw_ref[d, lv * CHUNK:(lv + 1) * CHUNK, :] for d in dirs] for lv in range(7)]
    ns = len(streams)
    items = [(si, j) for j in range(2) for si in range(ns)]
    idir = [dirs[si] for si, _ in items]
    offs = [(j if dirs[si] == 0 else 1 - j) * CHUNK for si, j in items]
    per = lambda lst: [lst[si] for si, _ in items]
    src = [(qk_f, v_f, la_f) if d == 0 else (qk_b, v_b, la_b) for d in idir]
    bidx = [streams[si][0] for si, _ in items]
    qk = [r[0][b, o:o + CHUNK, :] for b, r, o in zip(bidx, src, offs)]
    v = [r[1][b, o:o + CHUNK, :] for b, r, o in zip(bidx, src, offs)]
    la = [r[2][b, o:o + CHUNK, d * nk:(d + 1) * nk] for b, d, r, o in zip(bidx, idir, src, offs)]
    q = [x[:, :nk] * GLA_DK ** -0.5 for x in qk]
    k = [x[:, nk:] for x in qk]
    lat = [x.T for x in la]
    kt = [x.T for x in k]
    bcum = _each(_mm_sel, per(cm), la)
    exq = [jnp.exp(_mm_sel(s, x)) * vq for s, x, vq in zip(per(selq), la, per(valq))]
    exk = [jnp.exp(_mm_sel_r(x, s)) * vk for s, x, vk in zip(per(selkt), lat, per(valkt))]
    a = [p * _mm(qq, bdk(x)) for p, qq, x in zip(per(pairw[6]), q, kt)]
    for lv in range(6):
        sl = slice(lv * CHUNK, (lv + 1) * CHUNK)
        a = [aa + p * _mm(qq * eq[sl], bdk(x * ek[:, sl]))
             for aa, p, qq, eq, x, ek in zip(a, per(pairw[lv]), q, exq, kt, exk)]
    q_in = [qq * jnp.exp(bc) for qq, bc in zip(q, bcum)]
    b_last = [x[(CHUNK - 1 if d == 0 else 0):(CHUNK if d == 0 else 1), :] for x, d in zip(bcum, idir)]
    upd = [bdg.astype(F32) * _mm_tn(kk * jnp.exp(bl - bc), vv) for kk, bl, bc, vv in zip(k, b_last, bcum, v)]
    decay_t = [jnp.exp(jnp.sum(x, axis=1, keepdims=True)) for x in lat]
    st = [s_ref[si] for si in range(ns)]
    for j in range(2):
        cur = range(j * ns, (j + 1) * ns)
        o = [_mm(jnp.concatenate([q_in[i], a[i]], axis=1),
                 jnp.concatenate([s.astype(MXU_DT), _bd_rows(v[i], bdm)], axis=0)) for i, s in zip(cur, st)]
        for i, (b, d), oo in zip(cur, streams, o):
            (o_f if d == 0 else o_b)[b, offs[i]:offs[i] + CHUNK, :] = oo
        st = [decay_t[i] * s + upd[i] for i, s in zip(cur, st)]
    for si in range(ns):
        s_ref[si] = st[si]


def _gla(g_qk, g_vz, g_la, wc, gc, n_ctx_steps):
    b, t, _ = g_qk.shape
    n_steps = t // STEP
    fwd, bwd, tok, tokT, full = _scan_specs(b, n_ctx_steps, n_steps)
    consts = [wc["cm"]] + [gc[k] for k in ("selq", "valq", "selkt", "valkt", "pairw", "bdg")] + [wc["bdm"]]
    out = jax.ShapeDtypeStruct((b, t, GROUP_W), F32)
    w = 2 * N_HEADS * GLA_DK
    return pl.pallas_call(
        _gla_body,
        out_shape=[out, out],
        grid=(n_steps,),
        in_specs=[tok(w, fwd), tok(w, bwd), tok(GROUP_W, fwd), tok(GROUP_W, bwd),
                  tok(w, fwd), tok(w, bwd)] + [full(a) for a in consts],
        out_specs=[tok(GROUP_W, fwd), tok(GROUP_W, bwd)],
        scratch_shapes=[pltpu.VMEM((2 * b, N_HEADS * GLA_DK, GROUP_W), F32)],
        compiler_params=_params("arbitrary"),
        name="gla",
    )(g_qk, g_qk, g_vz, g_vz, g_la, g_la, *consts)


def _gdn_prep_body(prev_ref, cur_ref, next_ref, w_ref, ind_ref, o_ref, ext_ref, *, n_ctx_tiles, n_tiles):
    i = pl.program_id(1)
    tm = cur_ref.shape[0]
    first = (i == 0) | (i == n_ctx_tiles)
    last = (i == n_ctx_tiles - 1) | (i == n_tiles - 1)
    ext_ref[0:8, :] = jnp.where(first, 0.0, prev_ref[...])
    ext_ref[8:8 + tm, :] = cur_ref[...]
    ext_ref[8 + tm:16 + tm, :] = jnp.where(last, 0.0, next_ref[...])
    half = GDN_CONV // 2
    y = jnp.zeros(cur_ref.shape, F32)
    for j in range(GDN_CONV):
        y = y + ext_ref[8 - half + j:8 - half + j + tm, :] * w_ref[j:j + 1, :]
    y = jax.nn.silu(y)
    ind = ind_ref[...]
    q, k = y[:, :GROUP_W], y[:, GROUP_W:2 * GROUP_W]
    qn = q * lax.rsqrt(_mm_sel_r(q * q, ind) + EPS) * HEAD_DIM ** -0.5
    kn = k * lax.rsqrt(_mm_sel_r(k * k, ind) + EPS)
    o_ref[:, :GROUP_W] = qn
    o_ref[:, GROUP_W:2 * GROUP_W] = kn
    o_ref[:, 2 * GROUP_W:] = y[:, 2 * GROUP_W:]


def _gdn_prep(d_qkv, conv_w, ind64, n_ctx_tiles):
    b, t, w = d_qkv.shape
    tm = TOKEN_TILE
    n_tiles = t // tm
    r = tm // 8
    n8 = t // 8
    return pl.pallas_call(
        functools.partial(_gdn_prep_body, n_ctx_tiles=n_ctx_tiles, n_tiles=n_tiles),
        out_shape=jax.ShapeDtypeStruct((b, t, w), F32),
        grid=(b, n_tiles),
        in_specs=[pl.BlockSpec((None, 8, w), lambda bi, i: (bi, jnp.maximum(i * r - 1, 0), 0)),
                  pl.BlockSpec((None, tm, w), lambda bi, i: (bi, i, 0)),
                  pl.BlockSpec((None, 8, w), lambda bi, i: (bi, jnp.minimum((i + 1) * r, n8 - 1), 0)),
                  pl.BlockSpec(conv_w.shape, lambda bi, i: (0, 0)),
                  pl.BlockSpec(ind64.shape, lambda bi, i: (0, 0))],
        out_specs=pl.BlockSpec((None, tm, w), lambda bi, i: (bi, i, 0)),
        scratch_shapes=[pltpu.VMEM((tm + 16, w), F32)],
        compiler_params=_params("parallel", "parallel"),
        name="gdn_prep",
    )(d_qkv, d_qkv, d_qkv, conv_w, ind64)


def _wide_consts():
    cm = _cum_mats_np()
    eye = np.eye(CHUNK, dtype=np.float32)
    pair = _gla_level_mats_np()[2].reshape(2, 7, CHUNK, CHUNK)[:, :6]
    tile = lambda a: np.concatenate([a] * N_HEADS, axis=-1)
    hb = np.arange(GROUP_W) // HEAD_DIM
    bdm = (hb[:, None] == hb[None, :]).astype(np.float32)
    e_lo = np.zeros((2, 128, GROUP_W), np.float32)
    e_hi = np.zeros((2, 128, GROUP_W), np.float32)
    for d in range(2):
        for h in range(N_HEADS):
            e_lo[d, d * 4 + h, h * HEAD_DIM:(h + 1) * HEAD_DIM] = 1.0
            e_hi[d, 8 + d * 4 + h, h * HEAD_DIM:(h + 1) * HEAD_DIM] = 1.0
    z = np.zeros_like(cm)
    cm2 = np.concatenate([np.concatenate([cm, z], axis=2), np.concatenate([z, cm], axis=2)], axis=1)
    return dict(cm=jnp.asarray(cm), cm2=jnp.asarray(cm2), cmw=jnp.asarray(tile(cm)), strictw=jnp.asarray(tile(cm - eye)),
                eyew=jnp.asarray(tile(eye)), pairw=jnp.asarray(tile(pair).reshape(2, 6 * CHUNK, GROUP_W)),
                bdm=jnp.asarray(bdm, MXU_DT), e_lo=jnp.asarray(e_lo), e_hi=jnp.asarray(e_hi))


def _bd_rows(x, bdm):
    xb = x.astype(MXU_DT)
    return jnp.concatenate([xb] * N_HEADS, axis=0) * bdm


def _mm_bd_pairs(x, y, bdm):
    half = GROUP_W // 2
    mask = bdm[:half, :half]
    out = []
    for p in range(2):
        yb = y[:, p * half:(p + 1) * half].astype(MXU_DT)
        w = jnp.concatenate([yb, yb], axis=0) * mask
        out.append(jnp.dot(x[:, p * half:(p + 1) * half].astype(MXU_DT), w, preferred_element_type=F32))
    return jnp.concatenate(out, axis=1)


def _each(f, *lists):
    return [f(*a) for a in zip(*lists)]


def _gdn_body(qkv_f, qkv_b, g_f, g_b, gT_f, gT_b, cm2_ref, cmw_ref, strictw_ref, eyew_ref, pairw_ref, bdm_ref,
              elo_ref, ehi_ref, o_f, o_b, s_ref):
    @pl.when(pl.program_id(0) == 0)
    def _():
        s_ref[...] = jnp.zeros_like(s_ref)

    nb = qkv_f.shape[0]
    streams = [(b, d) for b in range(nb) for d in range(2)]
    dirs = [d for _, d in streams]
    bdm = bdm_ref[...]
    eyew = eyew_ref[...]
    cm2 = [cm2_ref[d] for d in dirs]
    cmw = [cmw_ref[d] for d in dirs]
    strictw = [strictw_ref[d] for d in dirs]
    pairw = [[pairw_ref[d, lv * CHUNK:(lv + 1) * CHUNK, :] for d in dirs] for lv in range(6)]
    elo = [elo_ref[d] for d in dirs]
    ehi = [ehi_ref[d] for d in dirs]
    g_all = [(g_f if d == 0 else g_b)[b] for b, d in streams]
    gt_all = [(gT_f if d == 0 else gT_b)[b] for b, d in streams]
    gc_all = _each(_mm_sel, cm2, g_all)
    gr_all = _each(_mm_sel_nt, gt_all, cm2)
    beta_all = _each(_mm_sel_r, g_all, elo)
    gw_all = _each(_mm_sel_r, gc_all, ehi)
    ns = len(streams)
    items = [(si, j) for j in range(2) for si in range(ns)]
    idir = [dirs[si] for si, _ in items]
    offs = [(j if dirs[si] == 0 else 1 - j) * CHUNK for si, j in items]
    per = lambda lst: [lst[si] for si, _ in items]
    qkv = [(qkv_f if d == 0 else qkv_b)[streams[si][0], o:o + CHUNK, :] for (si, _), d, o in zip(items, idir, offs)]
    gr = [x[:, o:o + CHUNK] for x, o in zip(per(gr_all), offs)]
    beta = [x[o:o + CHUNK] for x, o in zip(per(beta_all), offs)]
    gw = [x[o:o + CHUNK] for x, o in zip(per(gw_all), offs)]
    grow = [jnp.concatenate([x[8 + 4 * d + h:9 + 4 * d + h, :] for h in range(N_HEADS)], axis=1)
            for x, d in zip(gr, idir)]
    decay = [jnp.exp(jnp.where(m > 0, a - r, -jnp.inf)) for m, a, r in zip(per(cmw), gw, grow)]
    q = [x[:, :GROUP_W] for x in qkv]
    k = [x[:, GROUP_W:2 * GROUP_W] for x in qkv]
    v = [x[:, 2 * GROUP_W:] for x in qkv]
    kb = _each(lambda a, b_: a * b_, k, beta)
    bdk = [jnp.concatenate([x.T.astype(MXU_DT)] * N_HEADS, axis=1) * bdm for x in k]
    gq = _each(lambda a, b_, w: _mm(jnp.concatenate([a, b_], axis=0), w), kb, q, bdk)
    lower = [s * x[:CHUNK] * dc for s, x, dc in zip(per(strictw), gq, decay)]
    attn = [x[CHUNK:] * dc for x, dc in zip(gq, decay)]
    tw = [eyew - lo * p for lo, p in zip(lower, per(pairw[5]))]
    for lv in range(4, -1, -1):
        y = [_mm_bd_pairs(lo * p, t, bdm) for lo, p, t in zip(lower, per(pairw[lv]), tw)]
        tw = [t - _mm_bd_pairs(t, yy, bdm) for t, yy in zip(tw, y)]
    u = [_mm_bd_pairs(t, vv * b_, bdm) for t, vv, b_ in zip(tw, v, beta)]
    w = [_mm_bd_pairs(t, kk * jnp.exp(a), bdm) for t, kk, a in zip(tw, kb, gw)]
    q_in = [qq * jnp.exp(a) for qq, a in zip(q, gw)]
    g_last = [a[(CHUNK - 1 if d == 0 else 0):(CHUNK if d == 0 else 1), :] for a, d in zip(gw, idir)]
    k_out = [kk * jnp.exp(gl - a) for kk, gl, a in zip(k, g_last, gw)]
    st = [s_ref[si] for si in range(ns)]
    for j in range(2):
        cur = range(j * ns, (j + 1) * ns)
        v_new = [u[i] - _mm(w[i], s) for i, s in zip(cur, st)]
        o = [_mm(jnp.concatenate([q_in[i], attn[i]], axis=1),
                 jnp.concatenate([s.astype(MXU_DT), _bd_rows(vn, bdm)], axis=0))
             for i, s, vn in zip(cur, st, v_new)]
        for i, (b, d), oo in zip(cur, streams, o):
            (o_f if d == 0 else o_b)[b, offs[i]:offs[i] + CHUNK, :] = oo
        st = [jnp.exp(g_last[i]) * s + bdm.astype(F32) * _mm_tn(k_out[i], vn) for i, s, vn in zip(cur, st, v_new)]
    for si in range(ns):
        s_ref[si] = st[si]


def _gdn(d_qkv, d_g, d_gT, wc, n_ctx_steps):
    b, t, _ = d_qkv.shape
    n_steps = t // STEP
    fwd, bwd, tok, tokT, full = _scan_specs(b, n_ctx_steps, n_steps)
    consts = [wc[k] for k in ("cm2", "cmw", "strictw", "eyew", "pairw", "bdm", "e_lo", "e_hi")]
    out = jax.ShapeDtypeStruct((b, t, GROUP_W), F32)
    return pl.pallas_call(
        _gdn_body,
        out_shape=[out, out],
        grid=(n_steps,),
        in_specs=[tok(3 * GROUP_W, fwd), tok(3 * GROUP_W, bwd), tok(128, fwd), tok(128, bwd), tokT(fwd), tokT(bwd)]
                 + [full(a) for a in consts],
        out_specs=[tok(GROUP_W, fwd), tok(GROUP_W, bwd)],
        scratch_shapes=[pltpu.VMEM((2 * b, GROUP_W, GROUP_W), F32)],
        compiler_params=_params("arbitrary"),
        name="gdn",
    )(d_qkv, d_qkv, d_g, d_g, d_gT, d_gT, *consts)


def _rope_tables(n_ctx, n_lat):
    rows = n_lat // GRID_W
    row = np.repeat(np.arange(rows), GRID_W).astype(np.float32)
    col = np.tile(np.arange(GRID_W), rows).astype(np.float32)
    half = DA_QK // 2
    inv = np.power(ROPE_BASE, -np.arange(0, half, 2, dtype=np.float32) / half).astype(np.float32)

    def tab(p):
        ang = p[:, None] * inv
        ang = np.concatenate([ang, ang], axis=-1)
        return np.cos(ang), np.sin(ang)

    cr, sr = tab(row)
    cc, sc = tab(col)
    cos = np.concatenate([cr, cc], -1)
    sin = np.concatenate([sr, sc], -1)
    cos = np.concatenate([np.ones((n_ctx, DA_QK), np.float32), cos], 0)
    sin = np.concatenate([np.zeros((n_ctx, DA_QK), np.float32), sin], 0)
    reps = GROUP_W // DA_QK
    cos, sin = np.tile(cos, (1, reps)), np.tile(sin, (1, reps))
    first = (np.arange(GROUP_W) % 16) < 8
    sin_a = np.where(first, -sin, 0.0)
    sin_b = np.where(first, 0.0, sin)
    return jnp.asarray(cos, F32), jnp.asarray(sin_a, F32), jnp.asarray(sin_b, F32)


def _diff_operands(qk, v, cos, sa, sb, nw_ref, ind, qt_out, k_out, vt_out):
    def norm_rope(x, w):
        xn = x * lax.rsqrt(_mm_sel_r(x * x, ind) * (1.0 / DA_QK) + EPS) * w
        return xn * cos + pltpu.roll(xn, GROUP_W - 8, 1) * sa + pltpu.roll(xn, 8, 1) * sb

    q = norm_rope(qk[:, :GROUP_W], nw_ref[0:1, :]) * (DA_QK ** -0.5 * math.log2(math.e))
    k = norm_rope(qk[:, GROUP_W:], nw_ref[1:2, :])
    qt_out[...] = q.T.astype(qt_out.dtype)
    for h in range(N_HEADS):
        k_out[h] = k[:, h * HEAD_DIM:(h + 1) * HEAD_DIM].astype(k_out.dtype)
    vt = v.T
    tm = vt.shape[1]
    tail = jnp.concatenate([jnp.ones((8, tm), F32), jnp.zeros((V_ROWS - HEAD_DIM - 8, tm), F32)], axis=0)
    for h in range(N_HEADS):
        vt_out[h] = jnp.concatenate([vt[h * HEAD_DIM:(h + 1) * HEAD_DIM], tail], axis=0).astype(vt_out.dtype)


def _diff_attn_body(lam_ref, qt_ref, k_ref, vp_ref, vc_ref, o_ref, acc_ref, m_ref, e_ref, *, lam_init):
    j = pl.program_id(3)

    @pl.when(j == 0)
    def _():
        acc_ref[...] = jnp.zeros_like(acc_ref)
        m_ref[...] = jnp.full_like(m_ref, -jnp.inf)
        e_ref[...] = jnp.zeros_like(e_ref)

    vp = vp_ref[...]
    kk = k_ref[...]
    qt = qt_ref[...]
    part = lax.broadcasted_iota(jnp.int32, (2 * DA_QK, 1), 0) // DA_QK
    s = [jnp.dot(kk, jnp.where(part == p, qt, jnp.zeros_like(qt)), preferred_element_type=F32)
         for p in range(2)]
    pv = [jnp.dot(vp, e_ref[p], preferred_element_type=F32) for p in range(2)]
    for p in range(2):
        m_old = m_ref[p]
        m_new = jnp.maximum(m_old, jnp.max(s[p], axis=0, keepdims=True))
        e_ref[p] = jnp.exp2(s[p] - m_new).astype(e_ref.dtype)
        acc_ref[p] = jnp.exp2(m_old - m_new) * (acc_ref[p] + pv[p])
        m_ref[p] = m_new

    @pl.when(j == pl.num_programs(3) - 1)
    def _():
        lam = lam_ref[...]
        lam_full = (jnp.exp(jnp.sum(lam[0:1] * lam[1:2], keepdims=True))
                    - jnp.exp(jnp.sum(lam[2:3] * lam[3:4], keepdims=True)) + lam_init)
        vc = vc_ref[...]
        a0 = acc_ref[0] + jnp.dot(vc, e_ref[0], preferred_element_type=F32)
        a1 = acc_ref[1] + jnp.dot(vc, e_ref[1], preferred_element_type=F32)
        ot = (a0[:HEAD_DIM] / a0[HEAD_DIM:HEAD_DIM + 1]
              - lam_full * (a1[:HEAD_DIM] / a1[HEAD_DIM:HEAD_DIM + 1]))
        o_ref[...] = ot.T


def _diff_attn(lam, qt, k, vt, lam_init, n_keys, tq, tk):
    b, _, n_q = qt.shape
    return pl.pallas_call(
        functools.partial(_diff_attn_body, lam_init=lam_init),
        out_shape=jax.ShapeDtypeStruct((b, N_HEADS, n_q, HEAD_DIM), F32),
        grid=(b, N_HEADS, n_q // tq, n_keys // tk),
        in_specs=[pl.BlockSpec(lam.shape, lambda bi, h, i, j: (0, 0)),
                  pl.BlockSpec((None, 2 * DA_QK, tq), lambda bi, h, i, j: (bi, h, i)),
                  pl.BlockSpec((None, None, tk, HEAD_DIM), lambda bi, h, i, j: (bi, h, j, 0)),
                  pl.BlockSpec((None, None, V_ROWS, tk), lambda bi, h, i, j: (bi, h, 0, jnp.maximum(j - 1, 0))),
                  pl.BlockSpec((None, None, V_ROWS, tk), lambda bi, h, i, j: (bi, h, 0, j))],
        out_specs=pl.BlockSpec((None, None, tq, HEAD_DIM), lambda bi, h, i, j: (bi, h, i, 0)),
        scratch_shapes=[pltpu.VMEM((2, V_ROWS, tq), F32), pltpu.VMEM((2, 1, tq), F32),
                        pltpu.VMEM((2, tk, tq), MXU_DT)],
        compiler_params=_params("parallel", "parallel", "parallel", "arbitrary"),
        name="diff_attn",
    )(lam, qt, k, vt, vt)


def _finish_body(x_ref, mhf_ref, mhb_ref, moz_ref, ao_ref, az_ref, ghf_ref, ghb_ref, gz_ref, dhf_ref, dhb_ref, dz_ref,
                 nw_ref, ind_ref, wout_ref, gt_ref, o_ref, *, lam_init):
    ind = ind_ref[...]

    def head_norm(hh, w):
        return hh * lax.rsqrt(_mm_sel_r(hh * hh, ind) * (1.0 / HEAD_DIM) + EPS) * w

    moz = moz_ref[...]
    ym = (head_norm(mhf_ref[...] + mhb_ref[...], nw_ref[0:1, :]) * jax.nn.sigmoid(moz[:, :GROUP_W])
          * jax.nn.silu(moz[:, GROUP_W:]))
    ao = jnp.concatenate([ao_ref[h] for h in range(N_HEADS)], axis=1)
    ya = head_norm(ao, nw_ref[1:2, :]) * (1.0 - lam_init) * jax.nn.silu(az_ref[...])
    yg = head_norm(ghf_ref[...] + ghb_ref[...], nw_ref[2:3, :]) * jax.nn.silu(gz_ref[...])
    yd = head_norm(dhf_ref[...] + dhb_ref[...], nw_ref[3:4, :]) * jax.nn.silu(dz_ref[...])
    y = jnp.concatenate([ym, ya, yg, yd], axis=1).astype(MXU_DT)
    o_ref[...] = x_ref[...] + gt_ref[...] * jnp.dot(y, wout_ref[...], preferred_element_type=F32)


def _finish(xs, mh, m_oz, ao, a_z, gh, g_vz, dh, d_z, nw, ind64, wout, gtsel, lam_init, n_ctx_tiles, first_tile):
    b, t, d = xs.shape
    tm = TOKEN_TILE
    f0 = first_tile
    tok = lambda w, c=0: pl.BlockSpec((None, tm, w), lambda bi, i: (bi, i + f0, c))
    g = tok(GROUP_W)
    full = lambda a: pl.BlockSpec(a.shape, lambda bi, i: (0,) * a.ndim)
    return pl.pallas_call(
        functools.partial(_finish_body, lam_init=lam_init),
        out_shape=jax.ShapeDtypeStruct((b, t - f0 * tm, d), F32),
        grid=(b, t // tm - f0),
        in_specs=[tok(d), g, g, tok(2 * GROUP_W),
                  pl.BlockSpec((None, N_HEADS, tm, HEAD_DIM), lambda bi, i: (bi, 0, i + f0, 0)),
                  g, g, g, tok(GROUP_W, 1), g, g, g,
                  full(nw), full(ind64), full(wout),
                  pl.BlockSpec((None, None, 1, d),
                               lambda bi, i: (bi, (i + f0 >= n_ctx_tiles).astype(jnp.int32), 0, 0))],
        out_specs=pl.BlockSpec((None, tm, d), lambda bi, i: (bi, i, 0)),
        compiler_params=_params("parallel", "parallel"),
        name="finish",
    )(xs, mh[0], mh[1], m_oz, ao, a_z, gh[0], gh[1], g_vz, dh[0], dh[1], d_z, nw, ind64, wout, gtsel)


def kernel(x, c, ctx, c_ctx, norm_w, w_mod, b_mod, w_in, w_out, mlstm_b_i, mlstm_b_f, mlstm_norm, diff_q_norm,
           diff_k_norm, diff_lambda, diff_norm, gla_w_up, gla_b, gla_norm, gdn_conv, gdn_a_log, gdn_dt_bias,
           gdn_norm):
    bsz, n_lat, d = x.shape
    n_ctx = ctx.shape[1]
    depth = w_in.shape[0]
    t = n_ctx + n_lat
    assert n_ctx % TOKEN_TILE == 0 and n_lat % TOKEN_TILE == 0 and n_lat % GRID_W == 0
    assert n_lat % ATTN_TQ == 0 and t % ATTN_TK == 0 and n_ctx % STEP == 0
    n_ctx_tiles = n_ctx // TOKEN_TILE
    n_ctx_steps = n_ctx // STEP

    cc = jnp.concatenate([c, c_ctx[None, :], jnp.zeros((8 - (bsz + 1) % 8, d), F32)], axis=0)
    mod = _modulation(cc, w_mod, b_mod)

    wide = _wide_consts()
    gla_consts = _gla_consts()
    ind64 = _group_ones(GROUP_W, HEAD_DIM)
    ind32 = _group_ones(GROUP_W, DA_QK)
    tables = _rope_tables(n_ctx, n_lat)
    pad = lambda a, n: jnp.concatenate([a, jnp.zeros(a.shape[:-1] + (n - a.shape[-1],), a.dtype)], axis=-1)

    xs = jnp.concatenate([ctx, x], axis=1)
    for l in range(depth):
        lam_init = 0.8 - 0.6 * math.exp(-0.3 * l)
        sh, sc, gt = jnp.split(mod[l], 3, axis=-1)
        lat = jnp.stack([sh[:bsz], sc[:bsz]], axis=1)
        cx = jnp.broadcast_to(jnp.stack([sh[bsz], sc[bsz]], axis=0)[None], (bsz, 2, d))
        modsel = jnp.stack([cx, lat], axis=1)
        gtsel = jnp.stack([jnp.broadcast_to(gt[bsz][None], (bsz, d)), gt[:bsz]], axis=1)[:, :, None, :]
        wp, wg = _permute_w_in(w_in[l])
        pm = pad(jnp.concatenate([mlstm_b_i[l].reshape(1, 8), mlstm_b_f[l].reshape(1, 8)], axis=1), 128)
        pm = jnp.concatenate([pm, jnp.zeros((7, 128), F32)], axis=0)
        z8 = jnp.zeros((1, 8), F32)
        pd = jnp.concatenate([pad(jnp.concatenate([z8, gdn_a_log[l].reshape(1, 8)], axis=1), 128),
                              pad(jnp.concatenate([z8, gdn_dt_bias[l].reshape(1, 8)], axis=1), 128),
                              jnp.zeros((6, 128), F32)], axis=0)
        wup = jnp.zeros((128, 256), F32)
        wup = wup.at[0:GLA_RANK, 0:128].set(gla_w_up[l, 0]).at[GLA_RANK:2 * GLA_RANK, 128:256].set(gla_w_up[l, 1])
        bgk = gla_b[l].reshape(1, 256)

        qkn = jnp.stack([jnp.tile(diff_q_norm[l], GROUP_W // DA_QK), jnp.tile(diff_k_norm[l], GROUP_W // DA_QK)])
        (m_qkv, m_oz, m_g, m_gT, aqt, ak, avt, a_z, g_qk, g_vz, g_la, d_qkv, d_z, d_g, d_gT) = _inproj(
            xs, norm_w[l][None, :], modsel, wp, wg, pm, pd, wup.astype(MXU_DT), bgk, tables, qkn, ind32,
            n_ctx_tiles)

        mh = _mlstm(m_qkv, m_g, m_gT, wide, ind64, n_ctx_steps)

        lam = pad(diff_lambda[l], 128)
        ao_ctx = _diff_attn(lam, aqt[:, :, :n_ctx], ak, avt, lam_init, n_ctx, n_ctx, n_ctx)
        ao_lat = _diff_attn(lam, aqt[:, :, n_ctx:], ak, avt, lam_init, t, ATTN_TQ, ATTN_TK)
        ao = jnp.concatenate([ao_ctx, ao_lat], axis=2)

        gh = _gla(g_qk, g_vz, g_la, wide, gla_consts, n_ctx_steps)

        dq = _gdn_prep(d_qkv, gdn_conv[l], ind64, n_ctx_tiles)
        dh = _gdn(dq, d_g, d_gT, wide, n_ctx_steps)

        nws = jnp.stack([mlstm_norm[l], diff_norm[l], gla_norm[l], gdn_norm[l]])
        xs = _finish(xs, mh, m_oz, ao, a_z, gh, g_vz, dh, d_z, nws, ind64, w_out[l].astype(MXU_DT),
                     gtsel, lam_init, n_ctx_tiles, n_ctx_tiles if l == depth - 1 else 0)
    return xs
```

```python
import functools
import math

import jax
import jax.numpy as jnp
import numpy as np
from jax import lax
from jax.experimental import pallas as pl
from jax.experimental.pallas import tpu as pltpu

F32 = jnp.float32
MXU_DT = jnp.bfloat16
EPS = 1e-6
N_HEADS = 4
HEAD_DIM = 64
GROUP_W = N_HEADS * HEAD_DIM
CHUNK = 64
STEP = 2 * CHUNK
DA_QK = 32
GLA_DK = 32
GLA_RANK = 16
GLA_TAU = 16.0
GDN_CONV = 5
GRID_W = 64
ROPE_BASE = 10000.0
TOKEN_TILE = 256
ATTN_TQ = 2048
ATTN_TK = 768
V_ROWS = 80
VMEM_LIMIT = 48 * 1024 * 1024


def _mm(a, b):
    return jnp.dot(a.astype(MXU_DT), b.astype(MXU_DT), preferred_element_type=F32)


def _mm_nt(a, b):
    return lax.dot_general(a.astype(MXU_DT), b.astype(MXU_DT), (((1,), (1,)), ((), ())),
                           preferred_element_type=F32)


def _mm_tn(a, b):
    return lax.dot_general(a.astype(MXU_DT), b.astype(MXU_DT), (((0,), (0,)), ((), ())),
                           preferred_element_type=F32)


def _split2(x):
    hi = x.astype(MXU_DT)
    lo = (x - hi.astype(F32)).astype(MXU_DT)
    return hi, lo


def _mm_sel(sel, x):
    hi, lo = _split2(x)
    s = sel.astype(MXU_DT)
    return (jnp.dot(s, hi, preferred_element_type=F32) + jnp.dot(s, lo, preferred_element_type=F32))


def _mm_sel_r(x, sel):
    hi, lo = _split2(x)
    s = sel.astype(MXU_DT)
    return (jnp.dot(hi, s, preferred_element_type=F32) + jnp.dot(lo, s, preferred_element_type=F32))


def _mm_sel_nt(x, sel):
    hi, lo = _split2(x)
    s = sel.astype(MXU_DT)
    dn = (((1,), (1,)), ((), ()))
    return (lax.dot_general(hi, s, dn, preferred_element_type=F32)
            + lax.dot_general(lo, s, dn, preferred_element_type=F32))


def _group_ones(width, group):
    g = np.arange(width) // group
    return jnp.asarray((g[:, None] == g[None, :]).astype(np.float32))


def _params(*sem):
    return pltpu.CompilerParams(dimension_semantics=sem, vmem_limit_bytes=VMEM_LIMIT)


def _mod_body(cc_ref, w_ref, b_ref, o_ref):
    o_ref[...] = _mm(jax.nn.silu(cc_ref[...]), w_ref[...]) + b_ref[...]


def _modulation(cc, w_mod, b_mod):
    depth, d, d3 = w_mod.shape
    tn = 1024
    return pl.pallas_call(
        _mod_body,
        out_shape=jax.ShapeDtypeStruct((depth, cc.shape[0], d3), F32),
        grid=(depth, d3 // tn),
        in_specs=[pl.BlockSpec(cc.shape, lambda l, j: (0, 0)),
                  pl.BlockSpec((None, d, tn), lambda l, j: (l, 0, j)),
                  pl.BlockSpec((None, 1, tn), lambda l, j: (l, 0, j))],
        out_specs=pl.BlockSpec((None, cc.shape[0], tn), lambda l, j: (l, 0, j)),
        compiler_params=_params("parallel", "parallel"),
        name="modulation",
    )(cc, w_mod, b_mod.reshape(depth, 1, d3))


_SEG = dict(m_qkv=(0, 768), m_oz=(768, 1280), m_g=(1280, 1408), a_qk=(1408, 1920), a_vz=(1920, 2432),
            g_qk=(2432, 2688), g_vz=(2688, 3200), g_r=(3200, 3328), d_qkv=(3328, 4096), d_z=(4096, 4352),
            d_g=(4352, 4480))
P_PAD = 4480


def _permute_w_in(w):
    d = w.shape[0]
    z = lambda n: jnp.zeros((d, n), w.dtype)
    m0, a0, g0, d0 = 0, 1296, 2320, 3120
    cols = [w[:, m0:m0 + 1280], w[:, m0 + 1280:m0 + 1296], z(112),
            w[:, a0:a0 + 512], w[:, a0 + 512:a0 + 1024],
            w[:, g0:g0 + 256], w[:, g0 + 256:g0 + 768], w[:, g0 + 768:g0 + 800], z(96),
            w[:, d0:d0 + 768], w[:, d0 + 768:d0 + 1024], w[:, d0 + 1024:d0 + 1040], z(112)]
    wp = jnp.concatenate(cols, axis=1)
    wg = jnp.concatenate([wp[:, 1280:1408], wp[:, 4352:4480]], axis=1).T
    return wp.astype(MXU_DT), wg.astype(MXU_DT)


def _gate_fns(xm, xd, pm, pd, lane):
    gm = jnp.where(lane < 8, xm + pm[0], jax.nn.log_sigmoid(xm + pm[0]))
    gd = jnp.where(lane < 8, jax.nn.sigmoid(xd), -jnp.exp(pd[0]) * jax.nn.softplus(xd + pd[1]))
    return gm, gd


def _inproj_body(x_ref, nw_ref, mod_ref, wp_ref, wg_ref, pm_ref, pd_ref, pmT_ref, pdT_ref, wup_ref, bgk_ref,
                 cos_ref, sa_ref, sb_ref, qkn_ref, ind32_ref,
                 m_qkv, m_oz, m_g, m_gT, a_qt, a_k, a_vt, a_z, g_qk, g_vz, g_la, d_qkv, d_z, d_g, d_gT):
    x = x_ref[...]
    h = x * lax.rsqrt(jnp.mean(x * x, axis=-1, keepdims=True) + EPS) * nw_ref[...]
    h = (h * (1.0 + mod_ref[1:2, :]) + mod_ref[0:1, :]).astype(MXU_DT)
    u = jnp.dot(h, wp_ref[...], preferred_element_type=F32)
    seg = lambda name: u[:, _SEG[name][0]:_SEG[name][1]]
    m_qkv[...] = seg("m_qkv")
    m_oz[...] = seg("m_oz")
    avz = seg("a_vz")
    a_z[...] = avz[:, GROUP_W:]
    _diff_operands(seg("a_qk"), avz[:, :GROUP_W], cos_ref[...], sa_ref[...], sb_ref[...], qkn_ref, ind32_ref[...],
                   a_qt, a_k, a_vt)
    g_qk[...] = seg("g_qk")
    g_vz[...] = seg("g_vz")
    d_qkv[...] = seg("d_qkv")
    d_z[...] = seg("d_z")
    lane = lax.broadcasted_iota(jnp.int32, (1, 128), 1)
    gm, gd = _gate_fns(seg("m_g"), seg("d_g"), (pm_ref[0:1, :],), (pd_ref[0:1, :], pd_ref[1:2, :]), lane)
    m_g[...] = gm
    d_g[...] = gd
    ut = lax.dot_general(wg_ref[...], h, (((1,), (1,)), ((), ())), preferred_element_type=F32)
    row = lax.broadcasted_iota(jnp.int32, (128, 1), 0)
    gmt, gdt = _gate_fns(ut[0:128], ut[128:256], (pmT_ref[:, 0:1],), (pdT_ref[:, 0:1], pdT_ref[:, 1:2]), row)
    m_gT[...] = gmt
    d_gT[...] = gdt
    gk = _mm(seg("g_r"), wup_ref[...]) + bgk_ref[...]
    g_la[...] = jax.nn.log_sigmoid(gk) / GLA_TAU


def _inproj(xs, nw, modsel, wp, wg, pm, pd, wup, bgk, tables, qkn, ind32, n_ctx_tiles):
    b, t, d = xs.shape
    tm = TOKEN_TILE
    tok = lambda w: pl.BlockSpec((None, tm, w), lambda bi, i: (bi, i, 0))
    tokT = pl.BlockSpec((None, 128, tm), lambda bi, i: (bi, 0, i))
    full = lambda a: pl.BlockSpec(a.shape, lambda bi, i: (0,) * a.ndim)
    tab = pl.BlockSpec((tm, GROUP_W), lambda bi, i: (i, 0))
    attn_shapes = [jax.ShapeDtypeStruct((b, GROUP_W, t), MXU_DT),
                   jax.ShapeDtypeStruct((b, N_HEADS, t, HEAD_DIM), MXU_DT),
                   jax.ShapeDtypeStruct((b, N_HEADS, V_ROWS, t), MXU_DT)]
    attn_specs = [pl.BlockSpec((None, GROUP_W, tm), lambda bi, i: (bi, 0, i)),
                  pl.BlockSpec((None, N_HEADS, tm, HEAD_DIM), lambda bi, i: (bi, 0, i, 0)),
                  pl.BlockSpec((None, N_HEADS, V_ROWS, tm), lambda bi, i: (bi, 0, 0, i))]
    widths = [768, 512, 128, None, "attn", 256, 256, 512, 256, 768, 256, 128, None]
    out_shape, out_specs = [], []
    for w in widths:
        if w is None:
            out_shape.append(jax.ShapeDtypeStruct((b, 128, t), F32))
            out_specs.append(tokT)
        elif w == "attn":
            out_shape += attn_shapes
            out_specs += attn_specs
        else:
            out_shape.append(jax.ShapeDtypeStruct((b, t, w), F32))
            out_specs.append(tok(w))
    pmT, pdT = pm.T, pd.T
    cos, sa, sb = tables
    return pl.pallas_call(
        _inproj_body,
        out_shape=out_shape,
        grid=(b, t // tm),
        in_specs=[tok(d), full(nw),
                  pl.BlockSpec((None, None, 2, d), lambda bi, i: (bi, (i >= n_ctx_tiles).astype(jnp.int32), 0, 0)),
                  full(wp), full(wg), full(pm), full(pd), full(pmT), full(pdT), full(wup), full(bgk),
                  tab, tab, tab, full(qkn), full(ind32)],
        out_specs=out_specs,
        compiler_params=_params("parallel", "parallel"),
        name="inproj",
    )(xs, nw, modsel, wp, wg, pm, pd, pmT, pdT, wup, bgk, cos, sa, sb, qkn, ind32)


def _cum_mats_np():
    i = np.arange(CHUNK)
    fwd = (i[None, :] <= i[:, None]).astype(np.float32)
    return np.stack([fwd, fwd.T])


def _step_block(d, n, n_ctx_steps, n_steps):
    bwd = jnp.where(n < n_ctx_steps, n_ctx_steps - 1 - n, n_steps - 1 - (n - n_ctx_steps))
    return jnp.where(d == 0, n, bwd)


def _head_rowmax(x):
    return jnp.concatenate(
        [jnp.broadcast_to(jnp.max(x[:, h * HEAD_DIM:(h + 1) * HEAD_DIM], axis=1, keepdims=True), (CHUNK, HEAD_DIM))
         for h in range(N_HEADS)], axis=1)


def _mlstm_body(qkv_f, qkv_b, g_f, g_b, gT_f, gT_b, cm2_ref, cmw_ref, bdm_ref, elo_ref, ehi_ref, ind_ref,
                o_f, o_b, cn_ref, m_ref):
    nb = qkv_f.shape[0]
    streams = [(b, d) for b in range(nb) for d in range(2)]
    dirs = [d for _, d in streams]
    bdm = bdm_ref[...]
    bdm2 = jnp.concatenate([bdm, bdm], axis=1).astype(F32)
    ind = ind_ref[...].astype(MXU_DT)
    ones = jnp.ones((CHUNK, GROUP_W), F32)
    dup = lambda a: jnp.concatenate([a, a], axis=1)
    cm2 = [cm2_ref[d] for d in dirs]
    cmw = [cmw_ref[d] for d in dirs]
    elo = [elo_ref[d] for d in dirs]
    ehi = [ehi_ref[d] for d in dirs]
    g_all = [(g_f if d == 0 else g_b)[b] for b, d in streams]
    gt_all = [(gT_f if d == 0 else gT_b)[b] for b, d in streams]
    bc_all = _each(_mm_sel, cm2, g_all)
    br_all = _each(_mm_sel_nt, gt_all, cm2)
    bw_all = _each(_mm_sel_r, bc_all, ehi)
    iw_all = _each(_mm_sel_r, g_all, elo)
    for j in range(2):
        offs = [(j if d == 0 else 1 - j) * CHUNK for d in dirs]
        qkv = [(qkv_f if d == 0 else qkv_b)[b, o:o + CHUNK, :] for (b, d), o in zip(streams, offs)]
        gt = [x[:, o:o + CHUNK] for x, o in zip(gt_all, offs)]
        br = [x[:, o:o + CHUNK] for x, o in zip(br_all, offs)]
        bw = [x[o:o + CHUNK] for x, o in zip(bw_all, offs)]
        iw = [x[o:o + CHUNK] for x, o in zip(iw_all, offs)]
        rw = [jnp.concatenate([a[4 * d + h:4 * d + h + 1, :] - c[8 + 4 * d + h:9 + 4 * d + h, :]
                               for h in range(N_HEADS)], axis=1) for a, c, d in zip(gt, br, dirs)]
        dmat = [jnp.where(m > 0, a + r, -jnp.inf) for m, a, r in zip(cmw, bw, rw)]
        m_prev = [m_ref[i] for i in range(len(streams))]
        inter = _each(lambda a, m: a + m, bw, m_prev)
        m_t = [jnp.maximum(a, _head_rowmax(x)) for a, x in zip(inter, dmat)]
        q = [x[:, :GROUP_W] for x in qkv]
        k = [x[:, GROUP_W:2 * GROUP_W] * HEAD_DIM ** -0.5 for x in qkv]
        v = [x[:, 2 * GROUP_W:] for x in qkv]
        bdk = [jnp.concatenate([x.T.astype(MXU_DT)] * N_HEADS, axis=1) * bdm for x in k]
        s = [_mm(qq, w) * jnp.exp(x - m) for qq, w, x, m in zip(q, bdk, dmat, m_t)]
        w_inter = [jnp.exp(a - m) for a, m in zip(inter, m_t)]
        cn = [cn_ref[i] for i in range(len(streams))]
        acc = [dup(wi) * _mm(qq, c) + _mm(ss, jnp.concatenate([_bd_rows(vv, bdm), ind], axis=1))
               for wi, qq, c, ss, vv in zip(w_inter, q, cn, s, v)]
        hout = [a[:, :GROUP_W] / jnp.maximum(jnp.abs(a[:, GROUP_W:]), jnp.exp(-m)) for a, m in zip(acc, m_t)]
        last = [CHUNK - 1 if d == 0 else 0 for d in dirs]
        b_last = [a[r:r + 1, :] for a, r in zip(bw, last)]
        ds = [bl - a + i_ for bl, a, i_ in zip(b_last, bw, iw)]
        m_new = [jnp.maximum(bl + m, jnp.max(x, axis=0, keepdims=True)) for bl, m, x in zip(b_last, m_prev, ds)]
        for i, ((b, d), off) in enumerate(zip(streams, offs)):
            (o_f if d == 0 else o_b)[b, off:off + CHUNK, :] = hout[i]
            w_s = jnp.exp(ds[i] - m_new[i])
            w_c = jnp.exp(b_last[i] + m_prev[i] - m_new[i])
            upd = _mm_tn(k[i] * w_s, jnp.concatenate([v[i], ones], axis=1))
            cn_ref[i] = dup(w_c) * cn[i] + bdm2 * upd
            m_ref[i] = m_new[i]


def _scan_specs(b, n_ctx_steps, n_steps):
    fwd = lambda n: n
    bwd = lambda n: _step_block(1, n, n_ctx_steps, n_steps)
    tok = lambda w, f, c=0: pl.BlockSpec((b, STEP, w), lambda n: (0, f(n), c))
    tokT = lambda f: pl.BlockSpec((b, 128, STEP), lambda n: (0, 0, f(n)))
    full = lambda a: pl.BlockSpec(a.shape, lambda n: (0,) * a.ndim)
    return fwd, bwd, tok, tokT, full


def _gla_level_mats_np():
    L = CHUNK
    t = np.arange(L)
    sels, valids, pairs = [], [], []
    for rev in (False, True):
        pos_all = t if not rev else L - 1 - t
        eqs, eks, vq, vk, pm = [], [], [], [], []
        for j in range(6):
            w = 32 >> j
            par = pos_all // (2 * w)
            pos = pos_all % (2 * w)
            same = par[:, None] == par[None, :]
            late = pos >= w
            eq = same & late[:, None] & late[None, :] & (pos[None, :] <= pos[:, None])
            ek = same & (~late)[:, None] & (~late)[None, :] & (pos[None, :] > pos[:, None])
            eqs.append(eq)
            eks.append(ek)
            vq.append(late)
            vk.append(~late)
            pm.append(same & late[:, None] & (~late)[None, :])
        pm.append(np.eye(L, dtype=bool))
        sels.append(np.concatenate(eqs + eks, axis=0))
        valids.append(np.concatenate(vq + vk)[:, None])
        pairs.append(np.concatenate(pm, axis=0))
    f = lambda a: np.stack(a).astype(np.float32)
    return f(sels), f(valids), f(pairs)


def _gla_consts():
    sel, val, pair = _gla_level_mats_np()
    nq = 6 * CHUNK
    tile = lambda a: np.concatenate([a] * N_HEADS, axis=-1)
    selq, selk = sel[:, :nq], sel[:, nq:]
    selkt = np.stack([np.concatenate([selk[d, lv * CHUNK:(lv + 1) * CHUNK].T for lv in range(6)], axis=1)
                      for d in range(2)])
    valkt = np.stack([val[d, nq:, 0][None, :] for d in range(2)])
    rh = np.arange(N_HEADS * GLA_DK) // GLA_DK
    ch = np.arange(GROUP_W) // HEAD_DIM
    bdg = (rh[:, None] == ch[None, :]).astype(np.float32)
    return dict(selq=jnp.asarray(selq), valq=jnp.asarray(val[:, :nq]), selkt=jnp.asarray(selkt),
                valkt=jnp.asarray(valkt), pairw=jnp.asarray(tile(pair)), bdg=jnp.asarray(bdg, MXU_DT))


def _gla_body(qk_f, qk_b, v_f, v_b, la_f, la_b, cm_ref, selq_ref, valq_ref, selkt_ref, valkt_ref, pairw_ref,
              bdg_ref, bdm_ref, o_f, o_b, s_ref):
    nb = qk_f.shape[0]
    streams = [(b, d) for b in range(nb) for d in range(2)]
    dirs = [d for _, d in streams]
    nk = N_HEADS * GLA_DK
    bdg = bdg_ref[...]
    bdm = bdm_ref[...]
    bdk = lambda xt: jnp.concatenate([xt.astype(MXU_DT)] * N_HEADS, axis=1) * bdg
    cm = [cm_ref[d] for d in dirs]
    selq = [selq_ref[d] for d in dirs]
    valq = [valq_ref[d] for d in dirs]
    selkt = [selkt_ref[d] for d in dirs]
    valkt = [valkt_ref[d] for d in dirs]
    pairw = [[pairw_ref[d, lv * CHUNK:(lv + 1) * CHUNK, :] for d in dirs] for lv in range(7)]
    ns = len(streams)
    items = [(si, j) for j in range(2) for si in range(ns)]
    idir = [dirs[si] for si, _ in items]
    offs = [(j if dirs[si] == 0 else 1 - j) * CHUNK for si, j in items]
    per = lambda lst: [lst[si] for si, _ in items]
    src = [(qk_f, v_f, la_f) if d == 0 else (qk_b, v_b, la_b) for d in idir]
    bidx = [streams[si][0] for si, _ in items]
    qk = [r[0][b, o:o + CHUNK, :] for b, r, o in zip(bidx, src, offs)]
    v = [r[1][b, o:o + CHUNK, :] for b, r, o in zip(bidx, src, offs)]
    la = [r[2][b, o:o + CHUNK, d * nk:(d + 1) * nk] for b, d, r, o in zip(bidx, idir, src, offs)]
    q = [x[:, :nk] * GLA_DK ** -0.5 for x in qk]
    k = [x[:, nk:] for x in qk]
    lat = [x.T for x in la]
    kt = [x.T for x in k]
    bcum = _each(_mm_sel, per(cm), la)
    exq = [jnp.exp(_mm_sel(s, x)) * vq for s, x, vq in zip(per(selq), la, per(valq))]
    exk = [jnp.exp(_mm_sel_r(x, s)) * vk for s, x, vk in zip(per(selkt), lat, per(valkt))]
    a = [p * _mm(qq, bdk(x)) for p, qq, x in zip(per(pairw[6]), q, kt)]
    for lv in range(6):
        sl = slice(lv * CHUNK, (lv + 1) * CHUNK)
        a = [aa + p * _mm(qq * eq[sl], bdk(x * ek[:, sl]))
             for aa, p, qq, eq, x, ek in zip(a, per(pairw[lv]), q, exq, kt, exk)]
    q_in = [qq * jnp.exp(bc) for qq, bc in zip(q, bcum)]
    b_last = [x[(CHUNK - 1 if d == 0 else 0):(CHUNK if d == 0 else 1), :] for x, d in zip(bcum, idir)]
    upd = [bdg.astype(F32) * _mm_tn(kk * jnp.exp(bl - bc), vv) for kk, bl, bc, vv in zip(k, b_last, bcum, v)]
    decay_t = [jnp.exp(jnp.sum(x, axis=1, keepdims=True)) for x in lat]
    st = [s_ref[si] for si in range(ns)]
    for j in range(2):
        cur = range(j * ns, (j + 1) * ns)
        o = [_mm(jnp.concatenate([q_in[i], a[i]], axis=1),
                 jnp.concatenate([s.astype(MXU_DT), _bd_rows(v[i], bdm)], axis=0)) for i, s in zip(cur, st)]
        for i, (b, d), oo in zip(cur, streams, o):
            (o_f if d == 0 else o_b)[b, offs[i]:offs[i] + CHUNK, :] = oo
        st = [decay_t[i] * s + upd[i] for i, s in zip(cur, st)]
    for si in range(ns):
        s_ref[si] = st[si]


def _gdn_prep_body(prev_ref, cur_ref, next_ref, w_ref, ind_ref, o_ref, ext_ref, *, n_ctx_tiles, n_tiles):
    i = pl.program_id(1)
    tm = cur_ref.shape[0]
    first = (i == 0) | (i == n_ctx_tiles)
    last = (i == n_ctx_tiles - 1) | (i == n_tiles - 1)
    ext_ref[0:8, :] = jnp.where(first, 0.0, prev_ref[...])
    ext_ref[8:8 + tm, :] = cur_ref[...]
    ext_ref[8 + tm:16 + tm, :] = jnp.where(last, 0.0, next_ref[...])
    half = GDN_CONV // 2
    y = jnp.zeros(cur_ref.shape, F32)
    for j in range(GDN_CONV):
        y = y + ext_ref[8 - half + j:8 - half + j + tm, :] * w_ref[j:j + 1, :]
    y = jax.nn.silu(y)
    ind = ind_ref[...]
    q, k = y[:, :GROUP_W], y[:, GROUP_W:2 * GROUP_W]
    qn = q * lax.rsqrt(_mm_sel_r(q * q, ind) + EPS) * HEAD_DIM ** -0.5
    kn = k * lax.rsqrt(_mm_sel_r(k * k, ind) + EPS)
    o_ref[:, :GROUP_W] = qn
    o_ref[:, GROUP_W:2 * GROUP_W] = kn
    o_ref[:, 2 * GROUP_W:] = y[:, 2 * GROUP_W:]


def _gdn_prep(d_qkv, conv_w, ind64, n_ctx_tiles):
    b, t, w = d_qkv.shape
    tm = TOKEN_TILE
    n_tiles = t // tm
    r = tm // 8
    n8 = t // 8
    return pl.pallas_call(
        functools.partial(_gdn_prep_body, n_ctx_tiles=n_ctx_tiles, n_tiles=n_tiles),
        out_shape=jax.ShapeDtypeStruct((b, t, w), F32),
        grid=(b, n_tiles),
        in_specs=[pl.BlockSpec((None, 8, w), lambda bi, i: (bi, jnp.maximum(i * r - 1, 0), 0)),
                  pl.BlockSpec((None, tm, w), lambda bi, i: (bi, i, 0)),
                  pl.BlockSpec((None, 8, w), lambda bi, i: (bi, jnp.minimum((i + 1) * r, n8 - 1), 0)),
                  pl.BlockSpec(conv_w.shape, lambda bi, i: (0, 0)),
                  pl.BlockSpec(ind64.shape, lambda bi, i: (0, 0))],
        out_specs=pl.BlockSpec((None, tm, w), lambda bi, i: (bi, i, 0)),
        scratch_shapes=[pltpu.VMEM((tm + 16, w), F32)],
        compiler_params=_params("parallel", "parallel"),
        name="gdn_prep",
    )(d_qkv, d_qkv, d_qkv, conv_w, ind64)


def _wide_consts():
    cm = _cum_mats_np()
    eye = np.eye(CHUNK, dtype=np.float32)
    pair = _gla_level_mats_np()[2].reshape(2, 7, CHUNK, CHUNK)[:, :6]
    tile = lambda a: np.concatenate([a] * N_HEADS, axis=-1)
    hb = np.arange(GROUP_W) // HEAD_DIM
    bdm = (hb[:, None] == hb[None, :]).astype(np.float32)
    e_lo = np.zeros((2, 128, GROUP_W), np.float32)
    e_hi = np.zeros((2, 128, GROUP_W), np.float32)
    for d in range(2):
        for h in range(N_HEADS):
            e_lo[d, d * 4 + h, h * HEAD_DIM:(h + 1) * HEAD_DIM] = 1.0
            e_hi[d, 8 + d * 4 + h, h * HEAD_DIM:(h + 1) * HEAD_DIM] = 1.0
    z = np.zeros_like(cm)
    cm2 = np.concatenate([np.concatenate([cm, z], axis=2), np.concatenate([z, cm], axis=2)], axis=1)
    return dict(cm=jnp.asarray(cm), cm2=jnp.asarray(cm2), cmw=jnp.asarray(tile(cm)), strictw=jnp.asarray(tile(cm - eye)),
                eyew=jnp.asarray(tile(eye)), pairw=jnp.asarray(tile(pair).reshape(2, 6 * CHUNK, GROUP_W)),
                bdm=jnp.asarray(bdm, MXU_DT), e_lo=jnp.asarray(e_lo), e_hi=jnp.asarray(e_hi))


def _bd_rows(x, bdm):
    xb = x.astype(MXU_DT)
    return jnp.concatenate([xb] * N_HEADS, axis=0) * bdm


def _mm_bd_pairs(x, y, bdm):
    half = GROUP_W // 2
    mask = bdm[:half, :half]
    out = []
    for p in range(2):
        yb = y[:, p * half:(p + 1) * half].astype(MXU_DT)
        w = jnp.concatenate([yb, yb], axis=0) * mask
        out.append(jnp.dot(x[:, p * half:(p + 1) * half].astype(MXU_DT), w, preferred_element_type=F32))
    return jnp.concatenate(out, axis=1)


def _each(f, *lists):
    return [f(*a) for a in zip(*lists)]


def _gdn_body(qkv_f, qkv_b, g_f, g_b, gT_f, gT_b, cm2_ref, cmw_ref, strictw_ref, eyew_ref, pairw_ref, bdm_ref,
              elo_ref, ehi_ref, o_f, o_b, s_ref):
    nb = qkv_f.shape[0]
    streams = [(b, d) for b in range(nb) for d in range(2)]
    dirs = [d for _, d in streams]
    bdm = bdm_ref[...]
    eyew = eyew_ref[...]
    cm2 = [cm2_ref[d] for d in dirs]
    cmw = [cmw_ref[d] for d in dirs]
    strictw = [strictw_ref[d] for d in dirs]
    pairw = [[pairw_ref[d, lv * CHUNK:(lv + 1) * CHUNK, :] for d in dirs] for lv in range(6)]
    elo = [elo_ref[d] for d in dirs]
    ehi = [ehi_ref[d] for d in dirs]
    g_all = [(g_f if d == 0 else g_b)[b] for b, d in streams]
    gt_all = [(gT_f if d == 0 else gT_b)[b] for b, d in streams]
    gc_all = _each(_mm_sel, cm2, g_all)
    gr_all = _each(_mm_sel_nt, gt_all, cm2)
    beta_all = _each(_mm_sel_r, g_all, elo)
    gw_all = _each(_mm_sel_r, gc_all, ehi)
    ns = len(streams)
    items = [(si, j) for j in range(2) for si in range(ns)]
    idir = [dirs[si] for si, _ in items]
    offs = [(j if dirs[si] == 0 else 1 - j) * CHUNK for si, j in items]
    per = lambda lst: [lst[si] for si, _ in items]
    qkv = [(qkv_f if d == 0 else qkv_b)[streams[si][0], o:o + CHUNK, :] for (si, _), d, o in zip(items, idir, offs)]
    gr = [x[:, o:o + CHUNK] for x, o in zip(per(gr_all), offs)]
    beta = [x[o:o + CHUNK] for x, o in zip(per(beta_all), offs)]
    gw = [x[o:o + CHUNK] for x, o in zip(per(gw_all), offs)]
    grow = [jnp.concatenate([x[8 + 4 * d + h:9 + 4 * d + h, :] for h in range(N_HEADS)], axis=1)
            for x, d in zip(gr, idir)]
    decay = [jnp.exp(jnp.where(m > 0, a - r, -jnp.inf)) for m, a, r in zip(per(cmw), gw, grow)]
    q = [x[:, :GROUP_W] for x in qkv]
    k = [x[:, GROUP_W:2 * GROUP_W] for x in qkv]
    v = [x[:, 2 * GROUP_W:] for x in qkv]
    kb = _each(lambda a, b_: a * b_, k, beta)
    bdk = [jnp.concatenate([x.T.astype(MXU_DT)] * N_HEADS, axis=1) * bdm for x in k]
    gq = _each(lambda a, b_, w: _mm(jnp.concatenate([a, b_], axis=0), w), kb, q, bdk)
    lower = [s * x[:CHUNK] * dc for s, x, dc in zip(per(strictw), gq, decay)]
    attn = [x[CHUNK:] * dc for x, dc in zip(gq, decay)]
    tw = [eyew - lo * p for lo, p in zip(lower, per(pairw[5]))]
    for lv in range(4, -1, -1):
        y = [_mm_bd_pairs(lo * p, t, bdm) for lo, p, t in zip(lower, per(pairw[lv]), tw)]
        tw = [t - _mm_bd_pairs(t, yy, bdm) for t, yy in zip(tw, y)]
    u = [_mm_bd_pairs(t, vv * b_, bdm) for t, vv, b_ in zip(tw, v, beta)]
    w = [_mm_bd_pairs(t, kk * jnp.exp(a), bdm) for t, kk, a in zip(tw, kb, gw)]
    q_in = [qq * jnp.exp(a) for qq, a in zip(q, gw)]
    g_last = [a[(CHUNK - 1 if d == 0 else 0):(CHUNK if d == 0 else 1), :] for a, d in zip(gw, idir)]
    k_out = [kk * jnp.exp(gl - a) for kk, gl, a in zip(k, g_last, gw)]
    st = [s_ref[si] for si in range(ns)]
    for j in range(2):
        cur = range(j * ns, (j + 1) * ns)
        v_new = [u[i] - _mm(w[i], s) for i, s in zip(cur, st)]
        o = [_mm(jnp.concatenate([q_in[i], attn[i]], axis=1),
                 jnp.concatenate([s.astype(MXU_DT), _bd_rows(vn, bdm)], axis=0))
             for i, s, vn in zip(cur, st, v_new)]
        for i, (b, d), oo in zip(cur, streams, o):
            (o_f if d == 0 else o_b)[b, offs[i]:offs[i] + CHUNK, :] = oo
        st = [jnp.exp(g_last[i]) * s + bdm.astype(F32) * _mm_tn(k_out[i], vn) for i, s, vn in zip(cur, st, v_new)]
    for si in range(ns):
        s_ref[si] = st[si]


N_IN_MLSTM, N_IN_GLA, N_IN_GDN = 12, 14, 14


def _scans_body(*refs):
    n_in = N_IN_MLSTM + N_IN_GLA + N_IN_GDN
    ins, outs, scr = refs[:n_in], refs[n_in:n_in + 6], refs[n_in + 6:]

    @pl.when(pl.program_id(0) == 0)
    def _():
        for r in scr:
            r[...] = jnp.zeros_like(r)

    a, b = N_IN_MLSTM, N_IN_MLSTM + N_IN_GLA
    _gdn_body(*ins[b:], outs[4], outs[5], scr[3])
    _mlstm_body(*ins[:a], outs[0], outs[1], scr[0], scr[1])
    _gla_body(*ins[a:b], outs[2], outs[3], scr[2])


def _scans(m_qkv, m_g, m_gT, g_qk, g_vz, g_la, d_qkv, d_g, d_gT, wc, gc, ind64, n_ctx_steps):
    b, t, _ = m_qkv.shape
    n_steps = t // STEP
    fwd, bwd, tok, tokT, full = _scan_specs(b, n_ctx_steps, n_steps)
    both = lambda f: [f(fwd), f(bwd)]
    gated = both(lambda d: tok(3 * GROUP_W, d)) + both(lambda d: tok(128, d)) + both(tokT)
    wg = 2 * N_HEADS * GLA_DK
    c_m = [wc[k] for k in ("cm2", "cmw", "bdm", "e_lo", "e_hi")] + [ind64]
    c_g = [wc["cm"]] + [gc[k] for k in ("selq", "valq", "selkt", "valkt", "pairw", "bdg")] + [wc["bdm"]]
    c_d = [wc[k] for k in ("cm2", "cmw", "strictw", "eyew", "pairw", "bdm", "e_lo", "e_hi")]
    in_specs = (gated + [full(x) for x in c_m]
                + both(lambda d: tok(wg, d)) + both(lambda d: tok(GROUP_W, d))
                + both(lambda d: tok(wg, d)) + [full(x) for x in c_g]
                + gated + [full(x) for x in c_d])
    args = ([m_qkv, m_qkv, m_g, m_g, m_gT, m_gT] + c_m + [g_qk, g_qk, g_vz, g_vz, g_la, g_la] + c_g
            + [d_qkv, d_qkv, d_g, d_g, d_gT, d_gT] + c_d)
    assert (len(c_m) + 6, len(c_g) + 6, len(c_d) + 6) == (N_IN_MLSTM, N_IN_GLA, N_IN_GDN)
    out = jax.ShapeDtypeStruct((b, t, GROUP_W), F32)
    res = pl.pallas_call(
        _scans_body,
        out_shape=[out] * 6,
        grid=(n_steps,),
        in_specs=in_specs,
        out_specs=both(lambda d: tok(GROUP_W, d)) * 3,
        scratch_shapes=[pltpu.VMEM((2 * b, GROUP_W, 2 * GROUP_W), F32), pltpu.VMEM((2 * b, 1, GROUP_W), F32),
                        pltpu.VMEM((2 * b, N_HEADS * GLA_DK, GROUP_W), F32),
                        pltpu.VMEM((2 * b, GROUP_W, GROUP_W), F32)],
        compiler_params=_params("arbitrary"),
        name="scans",
    )(*args)
    return res[0:2], res[2:4], res[4:6]


def _rope_tables(n_ctx, n_lat):
    rows = n_lat // GRID_W
    row = np.repeat(np.arange(rows), GRID_W).astype(np.float32)
    col = np.tile(np.arange(GRID_W), rows).astype(np.float32)
    half = DA_QK // 2
    inv = np.power(ROPE_BASE, -np.arange(0, half, 2, dtype=np.float32) / half).astype(np.float32)

    def tab(p):
        ang = p[:, None] * inv
        ang = np.concatenate([ang, ang], axis=-1)
        return np.cos(ang), np.sin(ang)

    cr, sr = tab(row)
    cc, sc = tab(col)
    cos = np.concatenate([cr, cc], -1)
    sin = np.concatenate([sr, sc], -1)
    cos = np.concatenate([np.ones((n_ctx, DA_QK), np.float32), cos], 0)
    sin = np.concatenate([np.zeros((n_ctx, DA_QK), np.float32), sin], 0)
    reps = GROUP_W // DA_QK
    cos, sin = np.tile(cos, (1, reps)), np.tile(sin, (1, reps))
    first = (np.arange(GROUP_W) % 16) < 8
    sin_a = np.where(first, -sin, 0.0)
    sin_b = np.where(first, 0.0, sin)
    return jnp.asarray(cos, F32), jnp.asarray(sin_a, F32), jnp.asarray(sin_b, F32)


def _diff_operands(qk, v, cos, sa, sb, nw_ref, ind, qt_out, k_out, vt_out):
    def norm_rope(x, w):
        xn = x * lax.rsqrt(_mm_sel_r(x * x, ind) * (1.0 / DA_QK) + EPS) * w
        return xn * cos + pltpu.roll(xn, GROUP_W - 8, 1) * sa + pltpu.roll(xn, 8, 1) * sb

    q = norm_rope(qk[:, :GROUP_W], nw_ref[0:1, :]) * (DA_QK ** -0.5 * math.log2(math.e))
    k = norm_rope(qk[:, GROUP_W:], nw_ref[1:2, :])
    qt_out[...] = q.T.astype(qt_out.dtype)
    for h in range(N_HEADS):
        k_out[h] = k[:, h * HEAD_DIM:(h + 1) * HEAD_DIM].astype(k_out.dtype)
    vt = v.T
    tm = vt.shape[1]
    tail = jnp.concatenate([jnp.ones((8, tm), F32), jnp.zeros((V_ROWS - HEAD_DIM - 8, tm), F32)], axis=0)
    for h in range(N_HEADS):
        vt_out[h] = jnp.concatenate([vt[h * HEAD_DIM:(h + 1) * HEAD_DIM], tail], axis=0).astype(vt_out.dtype)


def _diff_attn_body(lam_ref, qt_ref, k_ref, vp_ref, vc_ref, o_ref, acc_ref, m_ref, e_ref, *, lam_init):
    j = pl.program_id(3)

    @pl.when(j == 0)
    def _():
        acc_ref[...] = jnp.zeros_like(acc_ref)
        m_ref[...] = jnp.full_like(m_ref, -jnp.inf)
        e_ref[...] = jnp.zeros_like(e_ref)

    vp = vp_ref[...]
    kk = k_ref[...]
    qt = qt_ref[...]
    part = lax.broadcasted_iota(jnp.int32, (2 * DA_QK, 1), 0) // DA_QK
    s = [jnp.dot(kk, jnp.where(part == p, qt, jnp.zeros_like(qt)), preferred_element_type=F32)
         for p in range(2)]
    pv = [jnp.dot(vp, e_ref[p], preferred_element_type=F32) for p in range(2)]
    for p in range(2):
        m_old = m_ref[p]
        m_new = jnp.maximum(m_old, jnp.max(s[p], axis=0, keepdims=True))
        e_ref[p] = jnp.exp2(s[p] - m_new).astype(e_ref.dtype)
        acc_ref[p] = jnp.exp2(m_old - m_new) * (acc_ref[p] + pv[p])
        m_ref[p] = m_new

    @pl.when(j == pl.num_programs(3) - 1)
    def _():
        lam = lam_ref[...]
        lam_full = (jnp.exp(jnp.sum(lam[0:1] * lam[1:2], keepdims=True))
                    - jnp.exp(jnp.sum(lam[2:3] * lam[3:4], keepdims=True)) + lam_init)
        vc = vc_ref[...]
        a0 = acc_ref[0] + jnp.dot(vc, e_ref[0], preferred_element_type=F32)
        a1 = acc_ref[1] + jnp.dot(vc, e_ref[1], preferred_element_type=F32)
        ot = (a0[:HEAD_DIM] / a0[HEAD_DIM:HEAD_DIM + 1]
              - lam_full * (a1[:HEAD_DIM] / a1[HEAD_DIM:HEAD_DIM + 1]))
        o_ref[...] = ot.T


def _diff_attn(lam, qt, k, vt, lam_init, n_keys, tq, tk):
    b, _, n_q = qt.shape
    return pl.pallas_call(
        functools.partial(_diff_attn_body, lam_init=lam_init),
        out_shape=jax.ShapeDtypeStruct((b, N_HEADS, n_q, HEAD_DIM), F32),
        grid=(b, N_HEADS, n_q // tq, n_keys // tk),
        in_specs=[pl.BlockSpec(lam.shape, lambda bi, h, i, j: (0, 0)),
                  pl.BlockSpec((None, 2 * DA_QK, tq), lambda bi, h, i, j: (bi, h, i)),
                  pl.BlockSpec((None, None, tk, HEAD_DIM), lambda bi, h, i, j: (bi, h, j, 0)),
                  pl.BlockSpec((None, None, V_ROWS, tk), lambda bi, h, i, j: (bi, h, 0, jnp.maximum(j - 1, 0))),
                  pl.BlockSpec((None, None, V_ROWS, tk), lambda bi, h, i, j: (bi, h, 0, j))],
        out_specs=pl.BlockSpec((None, None, tq, HEAD_DIM), lambda bi, h, i, j: (bi, h, i, 0)),
        scratch_shapes=[pltpu.VMEM((2, V_ROWS, tq), F32), pltpu.VMEM((2, 1, tq), F32),
                        pltpu.VMEM((2, tk, tq), MXU_DT)],
        compiler_params=_params("parallel", "parallel", "parallel", "arbitrary"),
        name="diff_attn",
    )(lam, qt, k, vt, vt)


def _finish_body(x_ref, mhf_ref, mhb_ref, moz_ref, ao_ref, az_ref, ghf_ref, ghb_ref, gz_ref, dhf_ref, dhb_ref, dz_ref,
                 nw_ref, ind_ref, wout_ref, gt_ref, o_ref, *, lam_init):
    ind = ind_ref[...]

    def head_norm(hh, w):
        return hh * lax.rsqrt(_mm_sel_r(hh * hh, ind) * (1.0 / HEAD_DIM) + EPS) * w

    moz = moz_ref[...]
    ym = (head_norm(mhf_ref[...] + mhb_ref[...], nw_ref[0:1, :]) * jax.nn.sigmoid(moz[:, :GROUP_W])
          * jax.nn.silu(moz[:, GROUP_W:]))
    ao = jnp.concatenate([ao_ref[h] for h in range(N_HEADS)], axis=1)
    ya = head_norm(ao, nw_ref[1:2, :]) * (1.0 - lam_init) * jax.nn.silu(az_ref[...])
    yg = head_norm(ghf_ref[...] + ghb_ref[...], nw_ref[2:3, :]) * jax.nn.silu(gz_ref[...])
    yd = head_norm(dhf_ref[...] + dhb_ref[...], nw_ref[3:4, :]) * jax.nn.silu(dz_ref[...])
    y = jnp.concatenate([ym, ya, yg, yd], axis=1).astype(MXU_DT)
    o_ref[...] = x_ref[...] + gt_ref[...] * jnp.dot(y, wout_ref[...], preferred_element_type=F32)


def _finish(xs, mh, m_oz, ao, a_z, gh, g_vz, dh, d_z, nw, ind64, wout, gtsel, lam_init, n_ctx_tiles, first_tile):
    b, t, d = xs.shape
    tm = TOKEN_TILE
    f0 = first_tile
    tok = lambda w, c=0: pl.BlockSpec((None, tm, w), lambda bi, i: (bi, i + f0, c))
    g = tok(GROUP_W)
    full = lambda a: pl.BlockSpec(a.shape, lambda bi, i: (0,) * a.ndim)
    return pl.pallas_call(
        functools.partial(_finish_body, lam_init=lam_init),
        out_shape=jax.ShapeDtypeStruct((b, t - f0 * tm, d), F32),
        grid=(b, t // tm - f0),
        in_specs=[tok(d), g, g, tok(2 * GROUP_W),
                  pl.BlockSpec((None, N_HEADS, tm, HEAD_DIM), lambda bi, i: (bi, 0, i + f0, 0)),
                  g, g, g, tok(GROUP_W, 1), g, g, g,
                  full(nw), full(ind64), full(wout),
                  pl.BlockSpec((None, None, 1, d),
                               lambda bi, i: (bi, (i + f0 >= n_ctx_tiles).astype(jnp.int32), 0, 0))],
        out_specs=pl.BlockSpec((None, tm, d), lambda bi, i: (bi, i, 0)),
        compiler_params=_params("parallel", "parallel"),
        name="finish",
    )(xs, mh[0], mh[1], m_oz, ao, a_z, gh[0], gh[1], g_vz, dh[0], dh[1], d_z, nw, ind64, wout, gtsel)


def kernel(x, c, ctx, c_ctx, norm_w, w_mod, b_mod, w_in, w_out, mlstm_b_i, mlstm_b_f, mlstm_norm, diff_q_norm,
           diff_k_norm, diff_lambda, diff_norm, gla_w_up, gla_b, gla_norm, gdn_conv, gdn_a_log, gdn_dt_bias,
           gdn_norm):
    bsz, n_lat, d = x.shape
    n_ctx = ctx.shape[1]
    depth = w_in.shape[0]
    t = n_ctx + n_lat
    assert n_ctx % TOKEN_TILE == 0 and n_lat % TOKEN_TILE == 0 and n_lat % GRID_W == 0
    assert n_lat % ATTN_TQ == 0 and t % ATTN_TK == 0 and n_ctx % STEP == 0
    n_ctx_tiles = n_ctx // TOKEN_TILE
    n_ctx_steps = n_ctx // STEP

    cc = jnp.concatenate([c, c_ctx[None, :], jnp.zeros((8 - (bsz + 1) % 8, d), F32)], axis=0)
    mod = _modulation(cc, w_mod, b_mod)

    wide = _wide_consts()
    gla_consts = _gla_consts()
    ind64 = _group_ones(GROUP_W, HEAD_DIM)
    ind32 = _group_ones(GROUP_W, DA_QK)
    tables = _rope_tables(n_ctx, n_lat)
    pad = lambda a, n: jnp.concatenate([a, jnp.zeros(a.shape[:-1] + (n - a.shape[-1],), a.dtype)], axis=-1)

    xs = jnp.concatenate([ctx, x], axis=1)
    for l in range(depth):
        lam_init = 0.8 - 0.6 * math.exp(-0.3 * l)
        sh, sc, gt = jnp.split(mod[l], 3, axis=-1)
        lat = jnp.stack([sh[:bsz], sc[:bsz]], axis=1)
        cx = jnp.broadcast_to(jnp.stack([sh[bsz], sc[bsz]], axis=0)[None], (bsz, 2, d))
        modsel = jnp.stack([cx, lat], axis=1)
        gtsel = jnp.stack([jnp.broadcast_to(gt[bsz][None], (bsz, d)), gt[:bsz]], axis=1)[:, :, None, :]
        wp, wg = _permute_w_in(w_in[l])
        pm = pad(jnp.concatenate([mlstm_b_i[l].reshape(1, 8), mlstm_b_f[l].reshape(1, 8)], axis=1), 128)
        pm = jnp.concatenate([pm, jnp.zeros((7, 128), F32)], axis=0)
        z8 = jnp.zeros((1, 8), F32)
        pd = jnp.concatenate([pad(jnp.concatenate([z8, gdn_a_log[l].reshape(1, 8)], axis=1), 128),
                              pad(jnp.concatenate([z8, gdn_dt_bias[l].reshape(1, 8)], axis=1), 128),
                              jnp.zeros((6, 128), F32)], axis=0)
        wup = jnp.zeros((128, 256), F32)
        wup = wup.at[0:GLA_RANK, 0:128].set(gla_w_up[l, 0]).at[GLA_RANK:2 * GLA_RANK, 128:256].set(gla_w_up[l, 1])
        bgk = gla_b[l].reshape(1, 256)

        qkn = jnp.stack([jnp.tile(diff_q_norm[l], GROUP_W // DA_QK), jnp.tile(diff_k_norm[l], GROUP_W // DA_QK)])
        (m_qkv, m_oz, m_g, m_gT, aqt, ak, avt, a_z, g_qk, g_vz, g_la, d_qkv, d_z, d_g, d_gT) = _inproj(
            xs, norm_w[l][None, :], modsel, wp, wg, pm, pd, wup.astype(MXU_DT), bgk, tables, qkn, ind32,
            n_ctx_tiles)

        lam = pad(diff_lambda[l], 128)
        ao_ctx = _diff_attn(lam, aqt[:, :, :n_ctx], ak, avt, lam_init, n_ctx, n_ctx, n_ctx)
        ao_lat = _diff_attn(lam, aqt[:, :, n_ctx:], ak, avt, lam_init, t, ATTN_TQ, ATTN_TK)
        ao = jnp.concatenate([ao_ctx, ao_lat], axis=2)

        dq = _gdn_prep(d_qkv, gdn_conv[l], ind64, n_ctx_tiles)
        mh, gh, dh = _scans(m_qkv, m_g, m_gT, g_qk, g_vz, g_la, dq, d_g, d_gT, wide, gla_consts, ind64,
                            n_ctx_steps)

        nws = jnp.stack([mlstm_norm[l], diff_norm[l], gla_norm[l], gdn_norm[l]])
        xs = _finish(xs, mh, m_oz, ao, a_z, gh, g_vz, dh, d_z, nws, ind64, w_out[l].astype(MXU_DT),
                     gtsel, lam_init, n_ctx_tiles, n_ctx_tiles if l == depth - 1 else 0)
    return xs
```
